```python
import jax, jax.numpy as jnp
from jax import lax
import numpy as np

D_MODEL = 1024
BATCH = 16
SEQ = 256
DEPTH = 1
DEC_BATCH = 8
DEC_SEQ = 2048
PAST_LEN = 512

GRID_W = 64
HEAD_DIM = 64
N_HEADS = D_MODEL // HEAD_DIM
N_KV_HEADS = N_HEADS // 4
KV_GROUP = N_HEADS // N_KV_HEADS
ROPE_THETA = 10000.0
Q_BLOCK = 128
R_DK = 128
R_HEADS = D_MODEL // R_DK
R_DV = D_MODEL // R_HEADS
CHUNK = 64
D_FF = 2816
EPS = 1e-6
ATT_W = N_HEADS * HEAD_DIM
KV_W = N_KV_HEADS * HEAD_DIM
REC_K = R_HEADS * R_DK
REC_V = R_HEADS * R_DV
IN_WIDTHS = (ATT_W, KV_W, KV_W, REC_K, REC_K, REC_K, REC_V, REC_V, D_MODEL, D_MODEL)
IN_W = sum(IN_WIDTHS)
IN_SPLITS = tuple(int(s) for s in np.cumsum(IN_WIDTHS)[:-1])

kernel_name = "hybrid_gqa_hgrn2_diffusion_step"


def rmsnorm(x, gain):
    xf = x.astype(jnp.float32)
    y = xf * lax.rsqrt(jnp.mean(xf * xf, axis=-1, keepdims=True) + EPS)
    return (y * gain.astype(jnp.float32)).astype(x.dtype)


def axial_angles(n_tokens):
    rows = n_tokens // GRID_W
    row = jnp.repeat(jnp.arange(rows, dtype=jnp.float32), GRID_W)
    col = jnp.tile(jnp.arange(GRID_W, dtype=jnp.float32), rows)
    half = HEAD_DIM // 2
    inv_freq = 1.0 / (ROPE_THETA ** (jnp.arange(0, half, 2, dtype=jnp.float32) / half))
    return row[:, None] * inv_freq, col[:, None] * inv_freq


def _rotate(xa, ang):
    x1, x2 = jnp.split(xa, 2, axis=-1)
    cos = jnp.cos(ang)[None, :, None, :]
    sin = jnp.sin(ang)[None, :, None, :]
    return jnp.concatenate([x1 * cos - x2 * sin, x1 * sin + x2 * cos], axis=-1)


def apply_axial_rope(x, ang_row, ang_col):
    xf = x.astype(jnp.float32)
    half = HEAD_DIM // 2
    return jnp.concatenate([_rotate(xf[..., :half], ang_row), _rotate(xf[..., half:], ang_col)], axis=-1).astype(x.dtype)


def block_attention(q, k, v):
    B, Lq = q.shape[0], q.shape[1]
    nb = Lq // Q_BLOCK
    qb = q.reshape(B, nb, Q_BLOCK, N_KV_HEADS, KV_GROUP, HEAD_DIM).transpose(1, 0, 2, 3, 4, 5)
    kf = k.astype(jnp.float32)
    vf = v.astype(jnp.float32)
    scale = HEAD_DIM ** -0.5

    def one_block(qblk):
        s = jnp.einsum("bqkgd,bskd->bkgqs", qblk.astype(jnp.float32), kf) * scale
        p = jax.nn.softmax(s, axis=-1)
        return jnp.einsum("bkgqs,bskd->bqkgd", p, vf)

    o = lax.map(one_block, qb)
    return o.transpose(1, 0, 2, 3, 4, 5).reshape(B, Lq, ATT_W).astype(q.dtype)


def chunk_recurrence(q, k, v, g, s0):
    B, L, H, _ = q.shape
    DV = v.shape[-1]
    nc = L // CHUNK

    def to_chunks(a):
        return a.reshape(B, nc, CHUNK, H, a.shape[-1]).transpose(1, 0, 3, 2, 4)

    causal = jnp.tril(jnp.ones((CHUNK, CHUNK), jnp.float32))

    def step(S, inp):
        qc, kc, vc, gc = inp
        b = jnp.cumsum(gc, axis=2)
        b_last = b[:, :, -1:, :]
        q_dec = qc * jnp.exp(b)
        k_dec = kc * jnp.exp(-b)
        scores = jnp.einsum("bhtk,bhsk->bhts", q_dec, k_dec) * causal
        o = jnp.einsum("bhts,bhsv->bhtv", scores, vc) + jnp.einsum("bhtk,bhkv->bhtv", q_dec, S)
        S_new = jnp.exp(b_last[:, :, 0, :])[..., None] * S + jnp.einsum("bhsk,bhsv->bhkv", kc * jnp.exp(b_last - b), vc)
        return S_new, o

    S_fin, o = lax.scan(step, s0.astype(jnp.float32), (to_chunks(q), to_chunks(k), to_chunks(v), to_chunks(g)))
    return o.transpose(1, 0, 3, 2, 4).reshape(B, L, H, DV), S_fin


def token_mixer(h, w_in, w_o, q_gain, k_gain, r_gain, lb, ang_row, ang_col, ctx_k, ctx_v, s0_fwd, s0_bwd):
    B, L, _ = h.shape
    f32 = jnp.float32
    aq, ak, av, rq, rf_fwd, rf_bwd, ri, rg, za, zr = jnp.split(h @ w_in, IN_SPLITS, axis=-1)
    q = rmsnorm(aq.reshape(B, L, N_HEADS, HEAD_DIM), q_gain)
    k = rmsnorm(ak.reshape(B, L, N_KV_HEADS, HEAD_DIM), k_gain)
    v = av.reshape(B, L, N_KV_HEADS, HEAD_DIM)
    if ctx_k is None:
        k_all, v_all = k, v
    else:
        q = apply_axial_rope(q, ang_row, ang_col)
        k = apply_axial_rope(k, ang_row, ang_col)
        k_all = jnp.concatenate([k, ctx_k.astype(k.dtype)], axis=1)
        v_all = jnp.concatenate([v, ctx_v.astype(v.dtype)], axis=1)
    attn = block_attention(q, k_all, v_all)
    rqf = jax.nn.silu(rq.astype(f32)).reshape(B, L, R_HEADS, R_DK) * (R_DK ** -0.5)
    rif = ri.astype(f32).reshape(B, L, R_HEADS, R_DV)

    def gates(zf, lb_d):
        f = lb_d + (1.0 - lb_d) * jax.nn.sigmoid(zf.astype(f32).reshape(B, L, R_HEADS, R_DK))
        return 1.0 - f, jnp.log(f)

    k_f, g_f = gates(rf_fwd, lb[0])
    k_b, g_b = gates(rf_bwd, lb[1])
    if s0_fwd is None:
        s0_fwd = jnp.zeros((B, R_HEADS, R_DK, R_DV), f32)
        s0_bwd = jnp.zeros((B, R_HEADS, R_DK, R_DV), f32)
    o_f, S_f = chunk_recurrence(rqf, k_f, rif, g_f, s0_fwd)
    o_b, S_b = chunk_recurrence(rqf[:, ::-1], k_b[:, ::-1], rif[:, ::-1], g_b[:, ::-1], s0_bwd)
    o_rec = rmsnorm(o_f + o_b[:, ::-1], r_gain) * jax.nn.silu(rg.astype(f32)).reshape(B, L, R_HEADS, R_DV)
    rec = o_rec.reshape(B, L, REC_V).astype(h.dtype)
    merged = jax.nn.sigmoid(za) * attn + jax.nn.sigmoid(zr) * rec
    return merged @ w_o, k, v, S_f, S_b


def conv_ffn(h, w_up, conv_w, conv_b, w_down):
    u = h @ w_up
    up = jnp.pad(u, ((0, 0), (1, 1), (0, 0)))
    u = up[:, :-2] * conv_w[0] + up[:, 1:-1] * conv_w[1] + up[:, 2:] * conv_w[2] + conv_b
    a, b = jnp.split(u, 2, axis=-1)
    return (jax.nn.silu(a) * b) @ w_down


def adaln(cond, w, b):
    mod = (jax.nn.silu(cond) @ w + b).reshape(cond.shape[0], 6, D_MODEL)
    return tuple(mod[:, i][:, None, :] for i in range(6))


def setup_inputs(seed: int = 0) -> dict:
    key = jax.random.key(seed)
    ks = jax.random.split(key, 24)
    d = D_MODEL

    def nrm(k, shape, s):
        return jax.random.normal(k, shape, jnp.float32) * s

    return {
        "x_prompt": nrm(ks[0], (BATCH, SEQ, d), 1.0),
        "x_sample": nrm(ks[1], (DEC_BATCH, DEC_SEQ, d), 1.0),
        "c": nrm(ks[2], (DEC_BATCH, d), 1.0),
        "cache_k": nrm(ks[3], (DEC_BATCH, DEPTH, PAST_LEN, N_KV_HEADS, HEAD_DIM), 1.0),
        "cache_v": nrm(ks[4], (DEC_BATCH, DEPTH, PAST_LEN, N_KV_HEADS, HEAD_DIM), 1.0),
        "state_hgrn": nrm(ks[5], (DEC_BATCH, DEPTH, 2, R_HEADS, R_DK, R_DV), 0.5),
        "c_ctx": nrm(ks[6], (d,), 1.0),
        "ada_w": nrm(ks[7], (DEPTH, d, 6 * d), 0.5 * d ** -0.5),
        "ada_b": nrm(ks[8], (DEPTH, 6 * d), 0.02),
        "norm1": 1.0 + nrm(ks[9], (DEPTH, d), 0.02),
        "norm2": 1.0 + nrm(ks[10], (DEPTH, d), 0.02),
        "w_in": nrm(ks[11], (DEPTH, d, IN_W), d ** -0.5),
        "q_norm": 1.0 + nrm(ks[12], (DEPTH, HEAD_DIM), 0.02),
        "k_norm": 1.0 + nrm(ks[13], (DEPTH, HEAD_DIM), 0.02),
        "hgrn_lb_logits": nrm(ks[14], (2, DEPTH + 1, REC_K), 0.1),
        "hgrn_norm": 1.0 + nrm(ks[15], (DEPTH, R_DV), 0.02),
        "w_o": nrm(ks[16], (DEPTH, d, d), d ** -0.5),
        "w_up": nrm(ks[17], (DEPTH, d, 2 * D_FF), d ** -0.5),
        "conv_w": nrm(ks[18], (DEPTH, 3, 2 * D_FF), 3 ** -0.5),
        "conv_b": nrm(ks[19], (DEPTH, 2 * D_FF), 0.02),
        "w_down": nrm(ks[20], (DEPTH, D_FF, d), D_FF ** -0.5),
        "final_norm": 1.0 + nrm(ks[21], (d,), 0.02),
    }


def reference(x_prompt, x_sample, c, cache_k, cache_v, state_hgrn, c_ctx, ada_w, ada_b, norm1, norm2,
              w_in, q_norm, k_norm, hgrn_lb_logits, hgrn_norm, w_o, w_up, conv_w, conv_b, w_down, final_norm):
    lb_all = jnp.cumsum(jax.nn.softmax(hgrn_lb_logits.astype(jnp.float32), axis=1), axis=1)
    ang_row, ang_col = axial_angles(x_sample.shape[1])
    xp, xs = x_prompt, x_sample
    new_k, new_v, new_s = [], [], []
    for l in range(DEPTH):
        lb = lb_all[:, l].reshape(2, R_HEADS, R_DK)
        sh1, sc1, g1, sh2, sc2, g2 = adaln(c_ctx[None, :], ada_w[l], ada_b[l])
        h = rmsnorm(xp, norm1[l]) * (1.0 + sc1) + sh1
        out, k_c, v_c, s_f, s_b = token_mixer(h, w_in[l], w_o[l], q_norm[l], k_norm[l], hgrn_norm[l], lb,
                                              None, None, None, None, None, None)
        xp = xp + g1 * out
        h = rmsnorm(xp, norm2[l]) * (1.0 + sc2) + sh2
        xp = xp + g2 * conv_ffn(h, w_up[l], conv_w[l], conv_b[l], w_down[l])
        new_k.append(k_c)
        new_v.append(v_c)
        new_s.append(jnp.stack([s_f, s_b], axis=1).astype(xp.dtype))
        sh1, sc1, g1, sh2, sc2, g2 = adaln(c, ada_w[l], ada_b[l])
        h = rmsnorm(xs, norm1[l]) * (1.0 + sc1) + sh1
        out, _, _, _, _ = token_mixer(h, w_in[l], w_o[l], q_norm[l], k_norm[l], hgrn_norm[l], lb,
                                      ang_row, ang_col, cache_k[:, l], cache_v[:, l],
                                      state_hgrn[:, l, 0], state_hgrn[:, l, 1])
        xs = xs + g1 * out
        h = rmsnorm(xs, norm2[l]) * (1.0 + sc2) + sh2
        xs = xs + g2 * conv_ffn(h, w_up[l], conv_w[l], conv_b[l], w_down[l])
    y_prompt = rmsnorm(xp, final_norm)
    y_sample = rmsnorm(xs, final_norm)
    new_cache_k = jnp.stack(new_k, axis=1)
    new_cache_v = jnp.stack(new_v, axis=1)
    new_state_hgrn = jnp.stack(new_s, axis=1)
    return (y_prompt, y_sample, new_cache_k, new_cache_v, new_state_hgrn)
```

```python
import functools

import jax
import jax.numpy as jnp
import numpy as np
from jax import lax
from jax.experimental import pallas as pl
from jax.experimental.pallas import tpu as pltpu

F32 = jnp.float32
BF16 = jnp.bfloat16

D_MODEL = 1024
GRID_W = 64
HEAD_DIM = 64
N_HEADS = 16
N_KV_HEADS = 4
KV_W = N_KV_HEADS * HEAD_DIM
ROPE_THETA = 10000.0
R_DK = 128
R_HEADS = 8
CHUNK = 64
D_FF = 2816
EPS = 1e-6
IN_W = 8704

LANES = 128
HALO = 16
VMEM_LIMIT = 56 * 1024 * 1024

_Q0, _K0, _V0, _RQ0, _RFF0, _RFB0, _RI0, _RG0, _ZA0, _ZR0 = (
    0, 1024, 1280, 1536, 2560, 3584, 4608, 5632, 6656, 7680)


def _sigmoid(x):
    return 1.0 / (1.0 + jnp.exp(-x))


def _cparams(sem):
    return pltpu.CompilerParams(dimension_semantics=sem, vmem_limit_bytes=VMEM_LIMIT)


def _resident(shape):
    nd = len(shape)
    return pl.BlockSpec(shape, lambda *_: (0,) * nd, pipeline_mode=pl.Buffered(1))


def _mod_kernel(c_ref, w_ref, b_ref, o_ref):
    c = c_ref[...]
    s = c * _sigmoid(c)
    o_ref[...] = jnp.dot(s, w_ref[...], preferred_element_type=F32,
                         precision=lax.Precision.HIGHEST) + b_ref[...]


def _mods(cond, ada_w, ada_b):
    rows, d = cond.shape
    n = ada_w.shape[1]
    tn = 1536
    return pl.pallas_call(
        _mod_kernel,
        name="mods",
        grid=(n // tn,),
        in_specs=[pl.BlockSpec((rows, d), lambda j: (0, 0)),
                  pl.BlockSpec((d, tn), lambda j: (0, j)),
                  pl.BlockSpec((1, tn), lambda j: (0, j))],
        out_specs=pl.BlockSpec((rows, tn), lambda j: (0, j)),
        out_shape=jax.ShapeDtypeStruct((rows, n), F32),
        compiler_params=_cparams(("arbitrary",)),
    )(cond, ada_w, ada_b.reshape(1, n))


def _inproj_kernel(*refs, rope):
    if rope:
        (x_ref, mod_ref, n1_ref, w_ref, qg_ref, kg_ref, lbl_ref, cs_ref, sn_ref,
         q_ref, k_ref, v_ref, rq_ref, g_ref, ri_ref, rg_ref, gt_ref) = refs
    else:
        (x_ref, mod_ref, n1_ref, w_ref, qg_ref, kg_ref, lbl_ref,
         q_ref, k_ref, v_ref, rq_ref, g_ref, ri_ref, rg_ref, gt_ref) = refs
        cs_ref = sn_ref = None
    tm = x_ref.shape[0]

    x = x_ref[...]
    ms = jnp.mean(x * x, axis=-1, keepdims=True)
    h = x * lax.rsqrt(ms + EPS) * n1_ref[...]
    h = h * (1.0 + mod_ref[1:2, :]) + mod_ref[0:1, :]
    hb = h.astype(BF16)

    def proj(a, b):
        return jnp.dot(hb, w_ref[:, a:b], preferred_element_type=F32)

    gr = lax.broadcasted_iota(jnp.int32, (LANES, LANES), 0) // HEAD_DIM
    gc = lax.broadcasted_iota(jnp.int32, (LANES, LANES), 1) // HEAD_DIM
    gmat = jnp.where(gr == gc, 1.0 / HEAD_DIM, 0.0).astype(BF16)

    def headnorm(a, gain):
        ss = jnp.dot((a * a).astype(BF16), gmat, preferred_element_type=F32)
        return a * lax.rsqrt(ss + EPS) * gain

    if rope:
        lane = lax.broadcasted_iota(jnp.int32, (tm, LANES), 1)
        first = (lane % 32) < 16
        cs = cs_ref[...]
        sn = sn_ref[...]

        def rot(a):
            sw = jnp.where(first, pltpu.roll(a, LANES - 16, 1), pltpu.roll(a, 16, 1))
            return a * cs + sw * sn
    else:
        def rot(a):
            return a

    scale = HEAD_DIM ** -0.5
    for c0 in range(0, 1024, 512):
        acc = proj(_Q0 + c0, _Q0 + c0 + 512)
        for s in range(4):
            col = c0 + s * LANES
            a = headnorm(acc[:, s * LANES:(s + 1) * LANES], qg_ref[:, col:col + LANES] * scale)
            q_ref[:, col:col + LANES] = rot(a).astype(BF16)

    acc = proj(_K0, _K0 + 512)
    for s in range(2):
        col = s * LANES
        a = headnorm(acc[:, col:col + LANES], kg_ref[:, col:col + LANES])
        k_ref[:, col:col + LANES] = rot(a)
    v_ref[...] = acc[:, 256:512]

    rscale = R_DK ** -0.5
    for c0 in range(0, 1024, 512):
        acc = proj(_RQ0 + c0, _RQ0 + c0 + 512)
        rq_ref[:, c0:c0 + 512] = (acc * _sigmoid(acc) * rscale).astype(BF16)

    l = lbl_ref[...]
    for d in range(2):
        a0 = l[2 * d:2 * d + 1, :]
        a1 = l[2 * d + 1:2 * d + 2, :]
        mx = jnp.maximum(a0, a1)
        e0 = jnp.exp(a0 - mx)
        e1 = jnp.exp(a1 - mx)
        lb = e0 / (e0 + e1)
        for c0 in range(0, 1024, 512):
            acc = proj(_RFF0 + d * 1024 + c0, _RFF0 + d * 1024 + c0 + 512)
            lbc = lb[:, c0:c0 + 512]
            f = lbc + (1.0 - lbc) * _sigmoid(acc)
            g_ref[:, d * 1024 + c0:d * 1024 + c0 + 512] = jnp.log(f)

    for c0 in range(0, 1024, 512):
        ri_ref[:, c0:c0 + 512] = proj(_RI0 + c0, _RI0 + c0 + 512).astype(BF16)
    for c0 in range(0, 1024, 512):
        acc = proj(_RG0 + c0, _RG0 + c0 + 512)
        rg_ref[:, c0:c0 + 512] = (acc * _sigmoid(acc)).astype(BF16)
    for c0 in range(0, 2048, 512):
        acc = proj(_ZA0 + c0, _ZA0 + c0 + 512)
        gt_ref[:, c0:c0 + 512] = _sigmoid(acc).astype(BF16)


def _inproj(x2d, mods, mod_row, seq_len, n1, w_in, qg, kg, lbl, cs, sn, tm):
    t, d = x2d.shape
    tiles_per_seq = seq_len // tm
    rope = cs is not None

    def tok(w):
        return pl.BlockSpec((tm, w), lambda i: (i, 0))

    in_specs = [tok(d),
                pl.BlockSpec((None, 6, d), lambda i: (mod_row(i // tiles_per_seq), 0, 0)),
                _resident((1, d)),
                _resident((d, IN_W)),
                _resident((1, 1024)),
                _resident((1, KV_W)),
                _resident((4, 1024))]
    args = [x2d, mods, n1, w_in, qg, kg, lbl]
    if rope:
        in_specs += [pl.BlockSpec((tm, LANES), lambda i: (i % tiles_per_seq, 0))] * 2
        args += [cs, sn]
    out_shape = [jax.ShapeDtypeStruct((t, 1024), BF16),
                 jax.ShapeDtypeStruct((t, KV_W), F32),
                 jax.ShapeDtypeStruct((t, KV_W), F32),
                 jax.ShapeDtypeStruct((t, 1024), BF16),
                 jax.ShapeDtypeStruct((t, 2048), F32),
                 jax.ShapeDtypeStruct((t, 1024), BF16),
                 jax.ShapeDtypeStruct((t, 1024), BF16),
                 jax.ShapeDtypeStruct((t, 2048), BF16)]
    out_specs = [tok(s.shape[1]) for s in out_shape]
    return pl.pallas_call(
        functools.partial(_inproj_kernel, rope=rope),
        name="inproj_rope" if rope else "inproj",
        grid=(t // tm,),
        in_specs=in_specs, out_specs=out_specs, out_shape=out_shape,
        compiler_params=_cparams(("arbitrary",)),
    )(*args)


def _attn_kernel(*refs, n_lat, n_ctx, tq, pps, kchunk):
    if n_ctx:
        q_ref, k_ref, v_ref, ck_ref, cv_ref, o_ref, kpad, vt, s_scr, p_scr = refs
    else:
        q_ref, k_ref, v_ref, o_ref, kpad, vt, s_scr, p_scr = refs
        ck_ref = cv_ref = None
    lk = n_lat + n_ctx
    first_step = jnp.logical_and(pl.program_id(1) == 0, pl.program_id(2) == 0)

    @pl.when(first_step)
    def _build():
        def place(dst, rows, ks):
            lane = lax.broadcasted_iota(jnp.int32, ks.shape, 1)
            lo = jnp.where(lane < HEAD_DIM, ks, 0.0)
            hi = jnp.where(lane >= HEAD_DIM, ks, 0.0)
            kpad[dst + 0, rows, :] = lo.astype(BF16)
            kpad[dst + 1, rows, :] = pltpu.roll(lo, HEAD_DIM, 1).astype(BF16)
            kpad[dst + 2, rows, :] = pltpu.roll(hi, HEAD_DIM, 1).astype(BF16)
            kpad[dst + 3, rows, :] = hi.astype(BF16)

        for s in range(2):
            cols = slice(s * LANES, (s + 1) * LANES)
            place(4 * s, slice(0, n_lat), k_ref[:, cols])
            vt[s, :, 0:n_lat] = v_ref[:, cols].T.astype(BF16)
            if n_ctx:
                place(4 * s, slice(n_lat, lk), ck_ref[:, cols])
                vt[s, :, n_lat:lk] = cv_ref[:, cols].T.astype(BF16)

    pair0 = pl.program_id(2) * pps
    for j in range(pps):
        pair = pair0 + j
        kv = pair // 2
        slab = pair // 4
        kvpar = kv % 2
        q2 = q_ref[:, j * LANES:(j + 1) * LANES]
        halves = []
        for parity in range(2):
            mx = jnp.full((8, tq), -jnp.inf, F32)
            for r0 in range(0, lk, kchunk):
                s = lax.dot_general(kpad[2 * kv + parity, r0:r0 + kchunk, :], q2,
                                    (((1,), (1,)), ((), ())), preferred_element_type=F32)
                s_scr[r0:r0 + kchunk, :] = s
                mx = jnp.maximum(mx, jnp.max(s.reshape(kchunk // 8, 8, tq), axis=0))
            m = jnp.max(mx, axis=0, keepdims=True)
            ls = jnp.zeros((8, tq), F32)
            for r0 in range(0, lk, kchunk):
                p = jnp.exp(s_scr[r0:r0 + kchunk, :] - m)
                ls = ls + jnp.sum(p.reshape(kchunk // 8, 8, tq), axis=0)
                p_scr[r0:r0 + kchunk, :] = p.astype(BF16)
            l = jnp.sum(ls, axis=0, keepdims=True)
            vrow = pl.multiple_of(kvpar * HEAD_DIM, HEAD_DIM)
            ot = jnp.dot(vt[slab, pl.ds(vrow, HEAD_DIM), :], p_scr[...],
                         preferred_element_type=F32)
            halves.append(ot * (1.0 / l))
        o2t = jnp.concatenate(halves, axis=0)
        o_ref[:, j * LANES:(j + 1) * LANES] = o2t.T.astype(BF16)


def _attention(q, k, v, ck, cv, batch, seq_len, tq, pps):
    t = q.shape[0]
    n_ctx = 0 if ck is None else ck.shape[1]
    lk = seq_len + n_ctx
    n_qt = seq_len // tq
    n_pp = (N_HEADS // 2) // pps
    kchunk = min(lk, 512)
    in_specs = [pl.BlockSpec((tq, pps * LANES), lambda b, i, p: (b * n_qt + i, p)),
                pl.BlockSpec((seq_len, KV_W), lambda b, i, p: (b, 0)),
                pl.BlockSpec((seq_len, KV_W), lambda b, i, p: (b, 0))]
    args = [q, k, v]
    if n_ctx:
        in_specs += [pl.BlockSpec((None, n_ctx, KV_W), lambda b, i, p: (b, 0, 0))] * 2
        args += [ck, cv]
    return pl.pallas_call(
        functools.partial(_attn_kernel, n_lat=seq_len, n_ctx=n_ctx, tq=tq, pps=pps, kchunk=kchunk),
        name="attn_ctx" if n_ctx else "attn",
        grid=(batch, n_qt, n_pp),
        in_specs=in_specs,
        out_specs=pl.BlockSpec((tq, pps * LANES), lambda b, i, p: (b * n_qt + i, p)),
        out_shape=jax.ShapeDtypeStruct((t, 1024), BF16),
        scratch_shapes=[pltpu.VMEM((2 * N_KV_HEADS, lk, LANES), BF16),
                        pltpu.VMEM((2, LANES, lk), BF16),
                        pltpu.VMEM((lk, tq), F32),
                        pltpu.VMEM((lk, tq), BF16)],
        compiler_params=_cparams(("arbitrary", "arbitrary", "arbitrary")),
    )(*args)


def _rec_kernel(*refs, seq_len, has_s0, emit_state):
    refs = list(refs)
    rq_ref, gf_ref, gb_ref, ri_ref, rg_ref, gain_ref = refs[:6]
    pos = 6
    if has_s0:
        s0f_ref, s0b_ref = refs[pos:pos + 2]
        pos += 2
    rec_ref = refs[pos]
    pos += 1
    if emit_state:
        sf_ref, sb_ref = refs[pos:pos + 2]
        pos += 2
    qd_scr, kdf_scr, kdb_scr, kef_scr, keb_scr, d_scr, u_scr, st_scr = refs[pos:]
    L = seq_len
    nc = L // CHUNK
    tpos = lax.broadcasted_iota(jnp.int32, (L, R_DK), 0) % CHUNK

    def chunk_cumsum(g, reverse):
        b = g
        for s in (1, 2, 4, 8, 16, 32):
            if reverse:
                sh = pltpu.roll(b, L - s, 0)
                b = b + jnp.where(tpos < CHUNK - s, sh, 0.0)
            else:
                sh = pltpu.roll(b, s, 0)
                b = b + jnp.where(tpos >= s, sh, 0.0)
        return b

    rq = rq_ref[...].astype(F32)
    for d, (g_ref, kd_scr, ke_scr) in enumerate(((gf_ref, kdf_scr, kef_scr), (gb_ref, kdb_scr, keb_scr))):
        g = g_ref[...]
        kk = 1.0 - jnp.exp(g)
        b = chunk_cumsum(g, reverse=(d == 1))
        tot = jnp.sum(g.reshape(nc, CHUNK, R_DK), axis=1, keepdims=True)
        tot = jnp.broadcast_to(tot, (nc, CHUNK, R_DK)).reshape(L, R_DK)
        qd_scr[:, d * R_DK:(d + 1) * R_DK] = (rq * jnp.exp(b)).astype(BF16)
        kd_scr[...] = (kk * jnp.exp(-b)).astype(BF16)
        ke_scr[...] = (kk * jnp.exp(tot - b)).astype(BF16)
        d_scr[:, d * R_DK:(d + 1) * R_DK] = jnp.exp(tot)

    def rows(c):
        return pl.ds(pl.multiple_of(c * CHUNK, CHUNK), CHUNK)

    def u_body(c, carry):
        vc = ri_ref[rows(c), :]
        tn = (((0,), (0,)), ((), ()))
        u_scr[c, :, 0:R_DK] = lax.dot_general(vc, kef_scr[rows(c), :], tn, preferred_element_type=F32)
        u_scr[c, :, R_DK:2 * R_DK] = lax.dot_general(vc, keb_scr[rows(c), :], tn, preferred_element_type=F32)
        return carry

    lax.fori_loop(0, nc, u_body, 0)

    if has_s0:
        sf0 = s0f_ref[...].T
        sb0 = s0b_ref[...].T
    else:
        sf0 = jnp.zeros((R_DK, R_DK), F32)
        sb0 = jnp.zeros((R_DK, R_DK), F32)

    def scan_body(i, carry):
        sf, sb = carry
        cb = nc - 1 - i
        st_scr[i, :, 0:R_DK] = sf.astype(BF16)
        st_scr[cb, :, R_DK:2 * R_DK] = sb.astype(BF16)
        df = d_scr[pl.ds(pl.multiple_of(i * CHUNK, CHUNK), 1), 0:R_DK]
        db = d_scr[pl.ds(pl.multiple_of(cb * CHUNK, CHUNK), 1), R_DK:2 * R_DK]
        sf = df * sf + u_scr[i, :, 0:R_DK]
        sb = db * sb + u_scr[cb, :, R_DK:2 * R_DK]
        return sf, sb

    sf, sb = lax.fori_loop(0, nc, scan_body, (sf0, sb0))
    if emit_state:
        sf_ref[...] = sf.T
        sb_ref[...] = sb.T

    ti = lax.broadcasted_iota(jnp.int32, (CHUNK, CHUNK), 0)
    si = lax.broadcasted_iota(jnp.int32, (CHUNK, CHUNK), 1)
    nt = (((1,), (1,)), ((), ()))
    gain = gain_ref[...]

    def o_body(c, carry):
        qd = qd_scr[rows(c), :]
        af = lax.dot_general(qd[:, 0:R_DK], kdf_scr[rows(c), :], nt, preferred_element_type=F32)
        ab = lax.dot_general(qd[:, R_DK:2 * R_DK], kdb_scr[rows(c), :], nt, preferred_element_type=F32)
        a = jnp.where(si <= ti, af, 0.0) + jnp.where(si >= ti, ab, 0.0)
        o = jnp.dot(a.astype(BF16), ri_ref[rows(c), :], preferred_element_type=F32)
        o = o + lax.dot_general(qd, st_scr[c], nt, preferred_element_type=F32)
        ms = jnp.mean(o * o, axis=-1, keepdims=True)
        y = o * lax.rsqrt(ms + EPS) * gain * rg_ref[rows(c), :].astype(F32)
        rec_ref[rows(c), :] = y.astype(BF16)
        return carry

    lax.fori_loop(0, nc, o_body, 0)


def _recurrence(rq, g, ri, rg, gain, s0, batch, seq_len, emit_state):
    t = rq.shape[0]
    nc = seq_len // CHUNK
    has_s0 = s0 is not None

    def col(off):
        return pl.BlockSpec((seq_len, R_DK), lambda b, h: (b, h + off))

    in_specs = [col(0), col(0), col(R_HEADS), col(0), col(0), _resident((1, R_DK))]
    args = [rq, g, g, ri, rg, gain]
    if has_s0:
        in_specs += [pl.BlockSpec((None, R_DK, R_DK), lambda b, h: (b * 2 * R_HEADS + h, 0, 0)),
                     pl.BlockSpec((None, R_DK, R_DK), lambda b, h: (b * 2 * R_HEADS + R_HEADS + h, 0, 0))]
        args += [s0, s0]
    out_shape = [jax.ShapeDtypeStruct((t, 1024), BF16)]
    out_specs = [col(0)]
    if emit_state:
        st_shape = jax.ShapeDtypeStruct((batch * R_HEADS, R_DK, R_DK), F32)
        out_shape += [st_shape, st_shape]
        out_specs += [pl.BlockSpec((None, R_DK, R_DK), lambda b, h: (b * R_HEADS + h, 0, 0))] * 2
    scratch = [pltpu.VMEM((seq_len, 2 * R_DK), BF16),
               pltpu.VMEM((seq_len, R_DK), BF16),
               pltpu.VMEM((seq_len, R_DK), BF16),
               pltpu.VMEM((seq_len, R_DK), BF16),
               pltpu.VMEM((seq_len, R_DK), BF16),
               pltpu.VMEM((seq_len, 2 * R_DK), F32),
               pltpu.VMEM((nc, R_DK, 2 * R_DK), F32),
               pltpu.VMEM((nc, R_DK, 2 * R_DK), BF16)]
    return pl.pallas_call(
        functools.partial(_rec_kernel, seq_len=seq_len, has_s0=has_s0, emit_state=emit_state),
        name="rec_s0" if has_s0 else "rec",
        grid=(batch, R_HEADS),
        in_specs=in_specs, out_specs=out_specs, out_shape=out_shape,
        scratch_shapes=scratch,
        compiler_params=_cparams(("arbitrary", "arbitrary")),
    )(*args)


def _merge_kernel(x_ref, attn_ref, rec_ref, gt_ref, wo_ref, mod_ref, n2_ref, x1_ref, h2_ref):
    ga = gt_ref[:, 0:1024].astype(F32)
    gr = gt_ref[:, 1024:2048].astype(F32)
    m = ga * attn_ref[...].astype(F32) + gr * rec_ref[...].astype(F32)
    out = jnp.dot(m.astype(BF16), wo_ref[...], preferred_element_type=F32)
    x1 = x_ref[...] + mod_ref[2:3, :] * out
    x1_ref[...] = x1
    ms = jnp.mean(x1 * x1, axis=-1, keepdims=True)
    h2 = x1 * lax.rsqrt(ms + EPS) * n2_ref[...]
    h2 = h2 * (1.0 + mod_ref[4:5, :]) + mod_ref[3:4, :]
    h2_ref[...] = h2.astype(BF16)


def _merge(x2d, attn, rec, gates, w_o, mods, mod_row, seq_len, n2, tm):
    t, d = x2d.shape
    tiles_per_seq = seq_len // tm

    def tok(w):
        return pl.BlockSpec((tm, w), lambda i: (i, 0))

    return pl.pallas_call(
        _merge_kernel,
        name="merge",
        grid=(t // tm,),
        in_specs=[tok(d), tok(d), tok(d), tok(2 * d), _resident((d, d)),
                  pl.BlockSpec((None, 6, d), lambda i: (mod_row(i // tiles_per_seq), 0, 0)),
                  _resident((1, d))],
        out_specs=[tok(d), tok(d)],
        out_shape=[jax.ShapeDtypeStruct((t, d), F32), jax.ShapeDtypeStruct((t, d), BF16)],
        compiler_params=_cparams(("arbitrary",)),
    )(x2d, attn, rec, gates, w_o, mods, n2)


def _ffn_kernel(hp_ref, h_ref, hn_ref, x1_ref, wup_ref, cw_ref, cb_ref, wdn_ref, mod_ref, fn_ref,
                y_ref, hbuf, act_scr, *, tiles_per_seq, ffc):
    tm = h_ref.shape[0]
    i = pl.program_id(0)
    ts = i % tiles_per_seq
    zero = jnp.zeros((HALO, D_MODEL), BF16)
    hbuf[0:HALO, :] = jnp.where(ts == 0, zero, hp_ref[...])
    hbuf[HALO:HALO + tm, :] = h_ref[...]
    hbuf[HALO + tm:HALO + tm + HALO, :] = jnp.where(ts == tiles_per_seq - 1, zero, hn_ref[...])
    hb = hbuf[...]
    mrows = tm + 2 * HALO

    def conv(col):
        u = jnp.dot(hb, wup_ref[:, col:col + ffc], preferred_element_type=F32)
        up = pltpu.roll(u, 1, 0)[HALO:HALO + tm, :]
        un = pltpu.roll(u, mrows - 1, 0)[HALO:HALO + tm, :]
        uc = u[HALO:HALO + tm, :]
        w = cw_ref[:, col:col + ffc]
        return up * w[0:1, :] + uc * w[1:2, :] + un * w[2:3, :] + cb_ref[:, col:col + ffc]

    for c0 in range(0, D_FF, ffc):
        a = conv(c0)
        b = conv(D_FF + c0)
        act_scr[:, c0:c0 + ffc] = (a * _sigmoid(a) * b).astype(BF16)

    f = jnp.dot(act_scr[...], wdn_ref[...], preferred_element_type=F32)
    x2 = x1_ref[...] + mod_ref[5:6, :] * f
    ms = jnp.mean(x2 * x2, axis=-1, keepdims=True)
    y_ref[...] = x2 * lax.rsqrt(ms + EPS) * fn_ref[...]


def _ffn(h2, x1, w_up, conv_w, conv_b, w_down, mods, mod_row, seq_len, fnorm, tm):
    t, d = x1.shape
    tiles_per_seq = seq_len // tm
    hb = tm // HALO
    n_hb = t // HALO

    return pl.pallas_call(
        functools.partial(_ffn_kernel, tiles_per_seq=tiles_per_seq, ffc=256),
        name="ffn",
        grid=(t // tm,),
        in_specs=[pl.BlockSpec((HALO, d), lambda i: (jnp.maximum(i * hb - 1, 0), 0)),
                  pl.BlockSpec((tm, d), lambda i: (i, 0)),
                  pl.BlockSpec((HALO, d), lambda i: (jnp.minimum((i + 1) * hb, n_hb - 1), 0)),
                  pl.BlockSpec((tm, d), lambda i: (i, 0)),
                  _resident((d, 2 * D_FF)),
                  _resident((3, 2 * D_FF)),
                  _resident((1, 2 * D_FF)),
                  _resident((D_FF, d)),
                  pl.BlockSpec((None, 6, d), lambda i: (mod_row(i // tiles_per_seq), 0, 0)),
                  _resident((1, d))],
        out_specs=pl.BlockSpec((tm, d), lambda i: (i, 0)),
        out_shape=jax.ShapeDtypeStruct((t, d), F32),
        scratch_shapes=[pltpu.VMEM((tm + 2 * HALO, d), BF16),
                        pltpu.VMEM((tm, D_FF), BF16)],
        compiler_params=_cparams(("arbitrary",)),
    )(h2, h2, h2, x1, w_up, conv_w, conv_b, w_down, mods, fnorm)


def _rope_tables(n_tokens):
    rows = n_tokens // GRID_W
    row = jnp.repeat(jnp.arange(rows, dtype=F32), GRID_W)
    colp = jnp.tile(jnp.arange(GRID_W, dtype=F32), rows)
    half = HEAD_DIM // 2
    inv_freq = 1.0 / (ROPE_THETA ** (jnp.arange(0, half, 2, dtype=F32) / half))
    ar = row[:, None] * inv_freq
    ac = colp[:, None] * inv_freq
    ang = jnp.concatenate([ar, ar, ac, ac], axis=-1)
    sign = jnp.asarray(np.tile(np.repeat(np.array([-1.0, 1.0], np.float32), 16), 2))
    cs = jnp.cos(ang)
    sn = jnp.sin(ang) * sign
    return jnp.tile(cs, (1, 2)), jnp.tile(sn, (1, 2))


def _group(x, mods, mod_row, w, ctx, rope_tabs, tq, pps, tm_ffn, emit_state):
    batch, seq_len, d = x.shape
    x2d = x.reshape(batch * seq_len, d)
    cs, sn = rope_tabs if rope_tabs is not None else (None, None)
    q, k, v, rq, g, ri, rg, gates = _inproj(
        x2d, mods, mod_row, seq_len, w["n1"], w["w_in"], w["qg"], w["kg"], w["lbl"], cs, sn, tm=256)
    ck, cv, s0 = ctx
    attn = _attention(q, k, v, ck, cv, batch, seq_len, tq, pps)
    rec_out = _recurrence(rq, g, ri, rg, w["rgain"], s0, batch, seq_len, emit_state)
    rec = rec_out[0]
    x1, h2 = _merge(x2d, attn, rec, gates, w["w_o"], mods, mod_row, seq_len, w["n2"], tm=256)
    y = _ffn(h2, x1, w["w_up"], w["conv_w"], w["conv_b"], w["w_down"], mods, mod_row, seq_len,
             w["fnorm"], tm=tm_ffn)
    return y.reshape(batch, seq_len, d), k, v, rec_out[1:]


def kernel(x_prompt, x_sample, c, cache_k, cache_v, state_hgrn, c_ctx, ada_w, ada_b, norm1, norm2,
           w_in, q_norm, k_norm, hgrn_lb_logits, hgrn_norm, w_o, w_up, conv_w, conv_b, w_down, final_norm):
    bp, lp, d = x_prompt.shape
    bs, ls, _ = x_sample.shape
    n_ctx = cache_k.shape[2]

    cond = jnp.zeros((16, d), F32).at[0].set(c_ctx).at[1:1 + bs].set(c)
    mods = _mods(cond, ada_w[0], ada_b[0]).reshape(16, 6, d)

    w = dict(
        n1=norm1[0].reshape(1, d), n2=norm2[0].reshape(1, d), fnorm=final_norm.reshape(1, d),
        w_in=w_in[0].astype(BF16), w_o=w_o[0].astype(BF16),
        w_up=w_up[0].astype(BF16), w_down=w_down[0].astype(BF16),
        conv_w=conv_w[0], conv_b=conv_b[0].reshape(1, 2 * D_FF),
        qg=jnp.tile(q_norm[0], N_HEADS).reshape(1, 1024),
        kg=jnp.tile(k_norm[0], N_KV_HEADS).reshape(1, KV_W),
        lbl=hgrn_lb_logits.reshape(4, 1024),
        rgain=hgrn_norm[0].reshape(1, R_DK),
    )

    y_p, k_p, v_p, st = _group(x_prompt, mods, lambda b: 0, w, (None, None, None), None,
                               tq=lp, pps=8, tm_ffn=lp, emit_state=True)
    ctx = (cache_k[:, 0].reshape(bs, n_ctx, KV_W), cache_v[:, 0].reshape(bs, n_ctx, KV_W),
           state_hgrn[:, 0].reshape(bs * 2 * R_HEADS, R_DK, R_DK))
    y_s, _, _, _ = _group(x_sample, mods, lambda b: b + 1, w, ctx, _rope_tables(ls),
                          tq=512, pps=1, tm_ffn=512, emit_state=False)

    sf, sb = st
    new_state = jnp.stack([sf.reshape(bp, R_HEADS, R_DK, R_DK),
                           sb.reshape(bp, R_HEADS, R_DK, R_DK)], axis=1)[:, None]
    new_k = k_p.reshape(bp, 1, lp, N_KV_HEADS, HEAD_DIM)
    new_v = v_p.reshape(bp, 1, lp, N_KV_HEADS, HEAD_DIM)
    return (y_p, y_s, new_k, new_v, new_state)
```

```python
import functools

import jax
import jax.numpy as jnp
import numpy as np
from jax import lax
from jax.experimental import pallas as pl
from jax.experimental.pallas import tpu as pltpu

F32 = jnp.float32
BF16 = jnp.bfloat16

D_MODEL = 1024
GRID_W = 64
HEAD_DIM = 64
N_HEADS = 16
N_KV_HEADS = 4
KV_W = N_KV_HEADS * HEAD_DIM
ROPE_THETA = 10000.0
R_DK = 128
R_HEADS = 8
CHUNK = 64
D_FF = 2816
EPS = 1e-6
IN_W = 8704

LANES = 128
HALO = 16
VMEM_LIMIT = 56 * 1024 * 1024

_Q0, _K0, _V0, _RQ0, _RFF0, _RFB0, _RI0, _RG0, _ZA0, _ZR0 = (
    0, 1024, 1280, 1536, 2560, 3584, 4608, 5632, 6656, 7680)


def _sigmoid(x):
    return 1.0 / (1.0 + jnp.exp(-x))


def _cparams(sem):
    return pltpu.CompilerParams(dimension_semantics=sem, vmem_limit_bytes=VMEM_LIMIT)


def _resident(shape):
    nd = len(shape)
    return pl.BlockSpec(shape, lambda *_: (0,) * nd, pipeline_mode=pl.Buffered(1))


def _mod_kernel(c_ref, w_ref, b_ref, o_ref):
    c = c_ref[...]
    s = c * _sigmoid(c)
    o_ref[...] = jnp.dot(s, w_ref[...], preferred_element_type=F32,
                         precision=lax.Precision.HIGHEST) + b_ref[...]


def _mods(cond, ada_w, ada_b):
    rows, d = cond.shape
    n = ada_w.shape[1]
    tn = 1536
    return pl.pallas_call(
        _mod_kernel,
        name="mods",
        grid=(n // tn,),
        in_specs=[pl.BlockSpec((rows, d), lambda j: (0, 0)),
                  pl.BlockSpec((d, tn), lambda j: (0, j)),
                  pl.BlockSpec((1, tn), lambda j: (0, j))],
        out_specs=pl.BlockSpec((rows, tn), lambda j: (0, j)),
        out_shape=jax.ShapeDtypeStruct((rows, n), F32),
        compiler_params=_cparams(("arbitrary",)),
    )(cond, ada_w, ada_b.reshape(1, n))


def _inproj_kernel(*refs, rope):
    if rope:
        (x_ref, mod_ref, n1_ref, w_ref, qg_ref, kg_ref, lbl_ref, cs_ref, sn_ref,
         q_ref, k_ref, v_ref, rq_ref, g_ref, ri_ref, rg_ref, gt_ref) = refs
    else:
        (x_ref, mod_ref, n1_ref, w_ref, qg_ref, kg_ref, lbl_ref,
         q_ref, k_ref, v_ref, rq_ref, g_ref, ri_ref, rg_ref, gt_ref) = refs
        cs_ref = sn_ref = None
    tm = x_ref.shape[0]

    x = x_ref[...]
    ms = jnp.mean(x * x, axis=-1, keepdims=True)
    h = x * lax.rsqrt(ms + EPS) * n1_ref[...]
    h = h * (1.0 + mod_ref[1:2, :]) + mod_ref[0:1, :]
    hb = h.astype(BF16)

    def proj(a, b):
        return jnp.dot(hb, w_ref[:, a:b], preferred_element_type=F32)

    gr = lax.broadcasted_iota(jnp.int32, (LANES, LANES), 0) // HEAD_DIM
    gc = lax.broadcasted_iota(jnp.int32, (LANES, LANES), 1) // HEAD_DIM
    gmat = jnp.where(gr == gc, 1.0 / HEAD_DIM, 0.0).astype(BF16)

    def headnorm(a, gain):
        ss = jnp.dot((a * a).astype(BF16), gmat, preferred_element_type=F32)
        return a * lax.rsqrt(ss + EPS) * gain

    if rope:
        lane = lax.broadcasted_iota(jnp.int32, (tm, LANES), 1)
        first = (lane % 32) < 16
        cs = cs_ref[...]
        sn = sn_ref[...]

        def rot(a):
            sw = jnp.where(first, pltpu.roll(a, LANES - 16, 1), pltpu.roll(a, 16, 1))
            return a * cs + sw * sn
    else:
        def rot(a):
            return a

    scale = HEAD_DIM ** -0.5
    for c0 in range(0, 1024, 512):
        acc = proj(_Q0 + c0, _Q0 + c0 + 512)
        for s in range(4):
            col = c0 + s * LANES
            a = headnorm(acc[:, s * LANES:(s + 1) * LANES], qg_ref[:, col:col + LANES] * scale)
            q_ref[:, col:col + LANES] = rot(a).astype(BF16)

    acc = proj(_K0, _K0 + 512)
    for s in range(2):
        col = s * LANES
        a = headnorm(acc[:, col:col + LANES], kg_ref[:, col:col + LANES])
        k_ref[:, col:col + LANES] = rot(a)
    v_ref[...] = acc[:, 256:512]

    rscale = R_DK ** -0.5
    for c0 in range(0, 1024, 512):
        acc = proj(_RQ0 + c0, _RQ0 + c0 + 512)
        rq_ref[:, c0:c0 + 512] = (acc * _sigmoid(acc) * rscale).astype(BF16)

    l = lbl_ref[...]
    for d in range(2):
        a0 = l[2 * d:2 * d + 1, :]
        a1 = l[2 * d + 1:2 * d + 2, :]
        mx = jnp.maximum(a0, a1)
        e0 = jnp.exp(a0 - mx)
        e1 = jnp.exp(a1 - mx)
        lb = e0 / (e0 + e1)
        for c0 in range(0, 1024, 512):
            acc = proj(_RFF0 + d * 1024 + c0, _RFF0 + d * 1024 + c0 + 512)
            lbc = lb[:, c0:c0 + 512]
            f = lbc + (1.0 - lbc) * _sigmoid(acc)
            g_ref[:, d * 1024 + c0:d * 1024 + c0 + 512] = jnp.log(f)

    for c0 in range(0, 1024, 512):
        ri_ref[:, c0:c0 + 512] = proj(_RI0 + c0, _RI0 + c0 + 512).astype(BF16)
    for c0 in range(0, 1024, 512):
        acc = proj(_RG0 + c0, _RG0 + c0 + 512)
        rg_ref[:, c0:c0 + 512] = (acc * _sigmoid(acc)).astype(BF16)
    for c0 in range(0, 2048, 512):
        acc = proj(_ZA0 + c0, _ZA0 + c0 + 512)
        gt_ref[:, c0:c0 + 512] = _sigmoid(acc).astype(BF16)


def _inproj(x2d, mods, mod_row, seq_len, n1, w_in, qg, kg, lbl, cs, sn, tm):
    t, d = x2d.shape
    tiles_per_seq = seq_len // tm
    rope = cs is not None

    def tok(w):
        return pl.BlockSpec((tm, w), lambda i: (i, 0))

    in_specs = [tok(d),
                pl.BlockSpec((None, 6, d), lambda i: (mod_row(i // tiles_per_seq), 0, 0)),
                _resident((1, d)),
                _resident((d, IN_W)),
                _resident((1, 1024)),
                _resident((1, KV_W)),
                _resident((4, 1024))]
    args = [x2d, mods, n1, w_in, qg, kg, lbl]
    if rope:
        in_specs += [pl.BlockSpec((tm, LANES), lambda i: (i % tiles_per_seq, 0))] * 2
        args += [cs, sn]
    out_shape = [jax.ShapeDtypeStruct((t, 1024), BF16),
                 jax.ShapeDtypeStruct((t, KV_W), F32),
                 jax.ShapeDtypeStruct((t, KV_W), F32),
                 jax.ShapeDtypeStruct((t, 1024), BF16),
                 jax.ShapeDtypeStruct((t, 2048), F32),
                 jax.ShapeDtypeStruct((t, 1024), BF16),
                 jax.ShapeDtypeStruct((t, 1024), BF16),
                 jax.ShapeDtypeStruct((t, 2048), BF16)]
    out_specs = [tok(s.shape[1]) for s in out_shape]
    return pl.pallas_call(
        functools.partial(_inproj_kernel, rope=rope),
        name="inproj_rope" if rope else "inproj",
        grid=(t // tm,),
        in_specs=in_specs, out_specs=out_specs, out_shape=out_shape,
        compiler_params=_cparams(("arbitrary",)),
    )(*args)


def _attn_kernel(*refs, n_lat, n_ctx, tq, pps, kchunk):
    if n_ctx:
        q_ref, k_ref, v_ref, ck_ref, cv_ref, o_ref, kpad, vt, s_scr, p_scr = refs
    else:
        q_ref, k_ref, v_ref, o_ref, kpad, vt, s_scr, p_scr = refs
        ck_ref = cv_ref = None
    lk = n_lat + n_ctx
    first_step = jnp.logical_and(pl.program_id(1) == 0, pl.program_id(2) == 0)

    @pl.when(first_step)
    def _build():
        def place(dst, rows, ks):
            lane = lax.broadcasted_iota(jnp.int32, ks.shape, 1)
            lo = jnp.where(lane < HEAD_DIM, ks, 0.0)
            hi = jnp.where(lane >= HEAD_DIM, ks, 0.0)
            kpad[dst + 0, rows, :] = lo.astype(BF16)
            kpad[dst + 1, rows, :] = pltpu.roll(lo, HEAD_DIM, 1).astype(BF16)
            kpad[dst + 2, rows, :] = pltpu.roll(hi, HEAD_DIM, 1).astype(BF16)
            kpad[dst + 3, rows, :] = hi.astype(BF16)

        for s in range(2):
            cols = slice(s * LANES, (s + 1) * LANES)
            place(4 * s, slice(0, n_lat), k_ref[:, cols])
            vt[s, :, 0:n_lat] = v_ref[:, cols].T.astype(BF16)
            if n_ctx:
                place(4 * s, slice(n_lat, lk), ck_ref[:, cols])
                vt[s, :, n_lat:lk] = cv_ref[:, cols].T.astype(BF16)

    pair0 = pl.program_id(2) * pps
    for j in range(pps):
        pair = pair0 + j
        kv = pair // 2
        slab = pair // 4
        kvpar = kv % 2
        q2 = q_ref[:, j * LANES:(j + 1) * LANES]
        halves = []
        for parity in range(2):
            mx = jnp.full((8, tq), -jnp.inf, F32)
            for r0 in range(0, lk, kchunk):
                s = lax.dot_general(kpad[2 * kv + parity, r0:r0 + kchunk, :], q2,
                                    (((1,), (1,)), ((), ())), preferred_element_type=F32)
                s_scr[r0:r0 + kchunk, :] = s
                mx = jnp.maximum(mx, jnp.max(s.reshape(kchunk // 8, 8, tq), axis=0))
            m = jnp.max(mx, axis=0, keepdims=True)
            ls = jnp.zeros((8, tq), F32)
            for r0 in range(0, lk, kchunk):
                p = jnp.exp(s_scr[r0:r0 + kchunk, :] - m)
                ls = ls + jnp.sum(p.reshape(kchunk // 8, 8, tq), axis=0)
                p_scr[r0:r0 + kchunk, :] = p.astype(BF16)
            l = jnp.sum(ls, axis=0, keepdims=True)
            vrow = pl.multiple_of(kvpar * HEAD_DIM, HEAD_DIM)
            ot = jnp.dot(vt[slab, pl.ds(vrow, HEAD_DIM), :], p_scr[...],
                         preferred_element_type=F32)
            halves.append(ot * (1.0 / l))
        o2t = jnp.concatenate(halves, axis=0)
        o_ref[:, j * LANES:(j + 1) * LANES] = o2t.T.astype(BF16)


def _attention(q, k, v, ck, cv, batch, seq_len, tq, pps):
    t = q.shape[0]
    n_ctx = 0 if ck is None else ck.shape[1]
    lk = seq_len + n_ctx
    n_qt = seq_len // tq
    n_pp = (N_HEADS // 2) // pps
    kchunk = min(lk, 512)
    in_specs = [pl.BlockSpec((tq, pps * LANES), lambda b, i, p: (b * n_qt + i, p)),
                pl.BlockSpec((seq_len, KV_W), lambda b, i, p: (b, 0)),
                pl.BlockSpec((seq_len, KV_W), lambda b, i, p: (b, 0))]
    args = [q, k, v]
    if n_ctx:
        in_specs += [pl.BlockSpec((None, n_ctx, KV_W), lambda b, i, p: (b, 0, 0))] * 2
        args += [ck, cv]
    return pl.pallas_call(
        functools.partial(_attn_kernel, n_lat=seq_len, n_ctx=n_ctx, tq=tq, pps=pps, kchunk=kchunk),
        name="attn_ctx" if n_ctx else "attn",
        grid=(batch, n_qt, n_pp),
        in_specs=in_specs,
        out_specs=pl.BlockSpec((tq, pps * LANES), lambda b, i, p: (b * n_qt + i, p)),
        out_shape=jax.ShapeDtypeStruct((t, 1024), BF16),
        scratch_shapes=[pltpu.VMEM((2 * N_KV_HEADS, lk, LANES), BF16),
                        pltpu.VMEM((2, LANES, lk), BF16),
                        pltpu.VMEM((lk, tq), F32),
                        pltpu.VMEM((lk, tq), BF16)],
        compiler_params=_cparams(("arbitrary", "arbitrary", "arbitrary")),
    )(*args)


def _rec_kernel(*refs, seq_len, has_s0, emit_state):
    refs = list(refs)
    rq_ref, gf_ref, gb_ref, ri_ref, rg_ref, gain_ref = refs[:6]
    pos = 6
    if has_s0:
        s0f_ref, s0b_ref = refs[pos:pos + 2]
        pos += 2
    rec_ref = refs[pos]
    pos += 1
    if emit_state:
        sf_ref, sb_ref = refs[pos:pos + 2]
        pos += 2
    qd_scr, kdf_scr, kdb_scr, ke_scr, d_scr, u_scr, st_scr, a_scr, o_scr = refs[pos:]
    L = seq_len
    nc = L // CHUNK
    unroll = min(nc, 8)
    tpos =lax.broadcasted_iota(jnp.int32, (L, R_DK), 0) % CHUNK

    def chunk_cumsum(g, reverse):
        b = g
        for s in (1, 2, 4, 8, 16, 32):
            if reverse:
                sh = pltpu.roll(b, L - s, 0)
                b = b + jnp.where(tpos < CHUNK - s, sh, 0.0)
            else:
                sh = pltpu.roll(b, s, 0)
                b = b + jnp.where(tpos >= s, sh, 0.0)
        return b

    rq = rq_ref[...].astype(F32)
    for d, (g_ref, kd_scr) in enumerate(((gf_ref, kdf_scr), (gb_ref, kdb_scr))):
        g = g_ref[...]
        kk = 1.0 - jnp.exp(g)
        b = chunk_cumsum(g, reverse=(d == 1))
        tot = jnp.sum(g.reshape(nc, CHUNK, R_DK), axis=1, keepdims=True)
        tot = jnp.broadcast_to(tot, (nc, CHUNK, R_DK)).reshape(L, R_DK)
        qd_scr[:, d * R_DK:(d + 1) * R_DK] = (rq * jnp.exp(b)).astype(BF16)
        kd_scr[...] = (kk * jnp.exp(-b)).astype(BF16)
        ke_scr[:, d * R_DK:(d + 1) * R_DK] = (kk * jnp.exp(tot - b)).astype(BF16)
        d_scr[:, d * R_DK:(d + 1) * R_DK] = jnp.exp(tot)

    def rows(c):
        return pl.ds(pl.multiple_of(c * CHUNK, CHUNK), CHUNK)

    def u_body(c, carry):
        vc = ri_ref[rows(c), :]
        tn = (((0,), (0,)), ((), ()))
        u_scr[c] = lax.dot_general(vc, ke_scr[rows(c), :], tn, preferred_element_type=F32)
        return carry

    lax.fori_loop(0, nc, u_body, 0, unroll=unroll)

    if has_s0:
        sf0 = s0f_ref[...].T
        sb0 = s0b_ref[...].T
    else:
        sf0 = jnp.zeros((R_DK, R_DK), F32)
        sb0 = jnp.zeros((R_DK, R_DK), F32)

    def scan_body(i, carry):
        sf, sb = carry
        cb = nc - 1 - i
        st_scr[i, :, 0:R_DK] = sf.astype(BF16)
        st_scr[cb, :, R_DK:2 * R_DK] = sb.astype(BF16)
        df = d_scr[pl.ds(pl.multiple_of(i * CHUNK, CHUNK), 1), 0:R_DK]
        db = d_scr[pl.ds(pl.multiple_of(cb * CHUNK, CHUNK), 1), R_DK:2 * R_DK]
        sf = df * sf + u_scr[i, :, 0:R_DK]
        sb = db * sb + u_scr[cb, :, R_DK:2 * R_DK]
        return sf, sb

    sf, sb = lax.fori_loop(0, nc, scan_body, (sf0, sb0))
    if emit_state:
        sf_ref[...] = sf.T
        sb_ref[...] = sb.T

    ti = lax.broadcasted_iota(jnp.int32, (CHUNK, CHUNK), 0)
    si = lax.broadcasted_iota(jnp.int32, (CHUNK, CHUNK), 1)
    nt = (((1,), (1,)), ((), ()))
    gain = gain_ref[...]

    def a_body(c, carry):
        qd = qd_scr[rows(c), :]
        af = lax.dot_general(qd[:, 0:R_DK], kdf_scr[rows(c), :], nt, preferred_element_type=F32)
        ab = lax.dot_general(qd[:, R_DK:2 * R_DK], kdb_scr[rows(c), :], nt, preferred_element_type=F32)
        a = jnp.where(si <= ti, af, 0.0) + jnp.where(si >= ti, ab, 0.0)
        a_scr[rows(c), :] = a.astype(BF16)
        return carry

    lax.fori_loop(0, nc, a_body, 0, unroll=unroll)

    def o_body(c, carry):
        o = jnp.dot(a_scr[rows(c), :], ri_ref[rows(c), :], preferred_element_type=F32)
        o = o + lax.dot_general(qd_scr[rows(c), :], st_scr[c], nt, preferred_element_type=F32)
        o_scr[rows(c), :] = o
        return carry

    lax.fori_loop(0, nc, o_body, 0, unroll=unroll)

    o = o_scr[...]
    ms = jnp.mean(o * o, axis=-1, keepdims=True)
    y = o * lax.rsqrt(ms + EPS) * gain * rg_ref[...].astype(F32)
    rec_ref[...] = y.astype(BF16)


def _recurrence(rq, g, ri, rg, gain, s0, batch, seq_len, emit_state):
    t = rq.shape[0]
    nc = seq_len // CHUNK
    has_s0 = s0 is not None

    def col(off):
        return pl.BlockSpec((seq_len, R_DK), lambda b, h: (b, h + off))

    in_specs = [col(0), col(0), col(R_HEADS), col(0), col(0), _resident((1, R_DK))]
    args = [rq, g, g, ri, rg, gain]
    if has_s0:
        in_specs += [pl.BlockSpec((None, R_DK, R_DK), lambda b, h: (b * 2 * R_HEADS + h, 0, 0)),
                     pl.BlockSpec((None, R_DK, R_DK), lambda b, h: (b * 2 * R_HEADS + R_HEADS + h, 0, 0))]
        args += [s0, s0]
    out_shape = [jax.ShapeDtypeStruct((t, 1024), BF16)]
    out_specs = [col(0)]
    if emit_state:
        st_shape = jax.ShapeDtypeStruct((batch * R_HEADS, R_DK, R_DK), F32)
        out_shape += [st_shape, st_shape]
        out_specs += [pl.BlockSpec((None, R_DK, R_DK), lambda b, h: (b * R_HEADS + h, 0, 0))] * 2
    scratch = [pltpu.VMEM((seq_len, 2 * R_DK), BF16),
               pltpu.VMEM((seq_len, R_DK), BF16),
               pltpu.VMEM((seq_len, R_DK), BF16),
               pltpu.VMEM((seq_len, 2 * R_DK), BF16),
               pltpu.VMEM((seq_len, 2 * R_DK), F32),
               pltpu.VMEM((nc, R_DK, 2 * R_DK), F32),
               pltpu.VMEM((nc, R_DK, 2 * R_DK), BF16),
               pltpu.VMEM((seq_len, CHUNK), BF16),
               pltpu.VMEM((seq_len, R_DK), F32)]
    return pl.pallas_call(
        functools.partial(_rec_kernel, seq_len=seq_len, has_s0=has_s0, emit_state=emit_state),
        name="rec_s0" if has_s0 else "rec",
        grid=(batch, R_HEADS),
        in_specs=in_specs, out_specs=out_specs, out_shape=out_shape,
        scratch_shapes=scratch,
        compiler_params=_cparams(("arbitrary", "arbitrary")),
    )(*args)


def _merge_kernel(x_ref, attn_ref, rec_ref, gt_ref, wo_ref, mod_ref, n2_ref, x1_ref, h2_ref):
    ga = gt_ref[:, 0:1024].astype(F32)
    gr = gt_ref[:, 1024:2048].astype(F32)
    m = ga * attn_ref[...].astype(F32) + gr * rec_ref[...].astype(F32)
    out = jnp.dot(m.astype(BF16), wo_ref[...], preferred_element_type=F32)
    x1 = x_ref[...] + mod_ref[2:3, :] * out
    x1_ref[...] = x1
    ms = jnp.mean(x1 * x1, axis=-1, keepdims=True)
    h2 = x1 * lax.rsqrt(ms + EPS) * n2_ref[...]
    h2 = h2 * (1.0 + mod_ref[4:5, :]) + mod_ref[3:4, :]
    h2_ref[...] = h2.astype(BF16)


def _merge(x2d, attn, rec, gates, w_o, mods, mod_row, seq_len, n2, tm):
    t, d = x2d.shape
    tiles_per_seq = seq_len // tm

    def tok(w):
        return pl.BlockSpec((tm, w), lambda i: (i, 0))

    return pl.pallas_call(
        _merge_kernel,
        name="merge",
        grid=(t // tm,),
        in_specs=[tok(d), tok(d), tok(d), tok(2 * d), _resident((d, d)),
                  pl.BlockSpec((None, 6, d), lambda i: (mod_row(i // tiles_per_seq), 0, 0)),
                  _resident((1, d))],
        out_specs=[tok(d), tok(d)],
        out_shape=[jax.ShapeDtypeStruct((t, d), F32), jax.ShapeDtypeStruct((t, d), BF16)],
        compiler_params=_cparams(("arbitrary",)),
    )(x2d, attn, rec, gates, w_o, mods, n2)


def _ffn_kernel(hp_ref, h_ref, hn_ref, x1_ref, wup_ref, cw_ref, cb_ref, wdn_ref, mod_ref, fn_ref,
                y_ref, hbuf, act_scr, *, tiles_per_seq, ffc):
    tm = h_ref.shape[0]
    i = pl.program_id(0)
    ts = i % tiles_per_seq
    zero = jnp.zeros((HALO, D_MODEL), BF16)
    hbuf[0:HALO, :] = jnp.where(ts == 0, zero, hp_ref[...])
    hbuf[HALO:HALO + tm, :] = h_ref[...]
    hbuf[HALO + tm:HALO + tm + HALO, :] = jnp.where(ts == tiles_per_seq - 1, zero, hn_ref[...])
    hb = hbuf[...]
    mrows = tm + 2 * HALO

    def conv(col):
        u = jnp.dot(hb, wup_ref[:, col:col + ffc], preferred_element_type=F32)
        up = pltpu.roll(u, 1, 0)[HALO:HALO + tm, :]
        un = pltpu.roll(u, mrows - 1, 0)[HALO:HALO + tm, :]
        uc = u[HALO:HALO + tm, :]
        w = cw_ref[:, col:col + ffc]
        return up * w[0:1, :] + uc * w[1:2, :] + un * w[2:3, :] + cb_ref[:, col:col + ffc]

    for c0 in range(0, D_FF, ffc):
        a = conv(c0)
        b = conv(D_FF + c0)
        act_scr[:, c0:c0 + ffc] = (a * _sigmoid(a) * b).astype(BF16)

    f = jnp.dot(act_scr[...], wdn_ref[...], preferred_element_type=F32)
    x2 = x1_ref[...] + mod_ref[5:6, :] * f
    ms = jnp.mean(x2 * x2, axis=-1, keepdims=True)
    y_ref[...] = x2 * lax.rsqrt(ms + EPS) * fn_ref[...]


def _ffn(h2, x1, w_up, conv_w, conv_b, w_down, mods, mod_row, seq_len, fnorm, tm):
    t, d = x1.shape
    tiles_per_seq = seq_len // tm
    hb = tm // HALO
    n_hb = t // HALO

    return pl.pallas_call(
        functools.partial(_ffn_kernel, tiles_per_seq=tiles_per_seq, ffc=256),
        name="ffn",
        grid=(t // tm,),
        in_specs=[pl.BlockSpec((HALO, d), lambda i: (jnp.maximum(i * hb - 1, 0), 0)),
                  pl.BlockSpec((tm, d), lambda i: (i, 0)),
                  pl.BlockSpec((HALO, d), lambda i: (jnp.minimum((i + 1) * hb, n_hb - 1), 0)),
                  pl.BlockSpec((tm, d), lambda i: (i, 0)),
                  _resident((d, 2 * D_FF)),
                  _resident((3, 2 * D_FF)),
                  _resident((1, 2 * D_FF)),
                  _resident((D_FF, d)),
                  pl.BlockSpec((None, 6, d), lambda i: (mod_row(i // tiles_per_seq), 0, 0)),
                  _resident((1, d))],
        out_specs=pl.BlockSpec((tm, d), lambda i: (i, 0)),
        out_shape=jax.ShapeDtypeStruct((t, d), F32),
        scratch_shapes=[pltpu.VMEM((tm + 2 * HALO, d), BF16),
                        pltpu.VMEM((tm, D_FF), BF16)],
        compiler_params=_cparams(("arbitrary",)),
    )(h2, h2, h2, x1, w_up, conv_w, conv_b, w_down, mods, fnorm)


def _rope_tables(n_tokens):
    rows = n_tokens // GRID_W
    row = jnp.repeat(jnp.arange(rows, dtype=F32), GRID_W)
    colp = jnp.tile(jnp.arange(GRID_W, dtype=F32), rows)
    half = HEAD_DIM // 2
    inv_freq = 1.0 / (ROPE_THETA ** (jnp.arange(0, half, 2, dtype=F32) / half))
    ar = row[:, None] * inv_freq
    ac = colp[:, None] * inv_freq
    ang = jnp.concatenate([ar, ar, ac, ac], axis=-1)
    sign = jnp.asarray(np.tile(np.repeat(np.array([-1.0, 1.0], np.float32), 16), 2))
    cs = jnp.cos(ang)
    sn = jnp.sin(ang) * sign
    return jnp.tile(cs, (1, 2)), jnp.tile(sn, (1, 2))


def _group(x, mods, mod_row, w, ctx, rope_tabs, tq, pps, tm_ffn, emit_state):
    batch, seq_len, d = x.shape
    x2d = x.reshape(batch * seq_len, d)
    cs, sn = rope_tabs if rope_tabs is not None else (None, None)
    q, k, v, rq, g, ri, rg, gates = _inproj(
        x2d, mods, mod_row, seq_len, w["n1"], w["w_in"], w["qg"], w["kg"], w["lbl"], cs, sn, tm=256)
    ck, cv, s0 = ctx
    attn = _attention(q, k, v, ck, cv, batch, seq_len, tq, pps)
    rec_out = _recurrence(rq, g, ri, rg, w["rgain"], s0, batch, seq_len, emit_state)
    rec = rec_out[0]
    x1, h2 = _merge(x2d, attn, rec, gates, w["w_o"], mods, mod_row, seq_len, w["n2"], tm=256)
    y = _ffn(h2, x1, w["w_up"], w["conv_w"], w["conv_b"], w["w_down"], mods, mod_row, seq_len,
             w["fnorm"], tm=tm_ffn)
    return y.reshape(batch, seq_len, d), k, v, rec_out[1:]


def kernel(x_prompt, x_sample, c, cache_k, cache_v, state_hgrn, c_ctx, ada_w, ada_b, norm1, norm2,
           w_in, q_norm, k_norm, hgrn_lb_logits, hgrn_norm, w_o, w_up, conv_w, conv_b, w_down, final_norm):
    bp, lp, d = x_prompt.shape
    bs, ls, _ = x_sample.shape
    n_ctx = cache_k.shape[2]

    cond = jnp.zeros((16, d), F32).at[0].set(c_ctx).at[1:1 + bs].set(c)
    mods = _mods(cond, ada_w[0], ada_b[0]).reshape(16, 6, d)

    w = dict(
        n1=norm1[0].reshape(1, d), n2=norm2[0].reshape(1, d), fnorm=final_norm.reshape(1, d),
        w_in=w_in[0].astype(BF16), w_o=w_o[0].astype(BF16),
        w_up=w_up[0].astype(BF16), w_down=w_down[0].astype(BF16),
        conv_w=conv_w[0], conv_b=conv_b[0].reshape(1, 2 * D_FF),
        qg=jnp.tile(q_norm[0], N_HEADS).reshape(1, 1024),
        kg=jnp.tile(k_norm[0], N_KV_HEADS).reshape(1, KV_W),
        lbl=hgrn_lb_logits.reshape(4, 1024),
        rgain=hgrn_norm[0].reshape(1, R_DK),
    )

    y_p, k_p, v_p, st = _group(x_prompt, mods, lambda b: 0, w, (None, None, None), None,
                               tq=lp, pps=8, tm_ffn=lp, emit_state=True)
    ctx = (cache_k[:, 0].reshape(bs, n_ctx, KV_W), cache_v[:, 0].reshape(bs, n_ctx, KV_W),
           state_hgrn[:, 0].reshape(bs * 2 * R_HEADS, R_DK, R_DK))
    y_s, _, _, _ = _group(x_sample, mods, lambda b: b + 1, w, ctx, _rope_tables(ls),
                          tq=512, pps=1, tm_ffn=512, emit_state=False)

    sf, sb = st
    new_state = jnp.stack([sf.reshape(bp, R_HEADS, R_DK, R_DK),
                           sb.reshape(bp, R_HEADS, R_DK, R_DK)], axis=1)[:, None]
    new_k = k_p.reshape(bp, 1, lp, N_KV_HEADS, HEAD_DIM)
    new_v = v_p.reshape(bp, 1, lp, N_KV_HEADS, HEAD_DIM)
    return (y_p, y_s, new_k, new_v, new_state)
```

```python
import functools

import jax
import jax.numpy as jnp
import numpy as np
from jax import lax
from jax.experimental import pallas as pl
from jax.experimental.pallas import tpu as pltpu

F32 = jnp.float32
BF16 = jnp.bfloat16

D_MODEL = 1024
GRID_W = 64
HEAD_DIM = 64
N_HEADS = 16
N_KV_HEADS = 4
KV_W = N_KV_HEADS * HEAD_DIM
ROPE_THETA = 10000.0
R_DK = 128
R_HEADS = 8
CHUNK = 64
D_FF = 2816
EPS = 1e-6
LOG2E = 1.4426950408889634
IN_W = 8704

LANES = 128
HALO = 16
VT_ROWS = HEAD_DIM + 16
SM_ROWS = 32
VMEM_LIMIT = 56 * 1024 * 1024

_Q0, _K0, _V0, _RQ0, _RFF0, _RFB0, _RI0, _RG0, _ZA0, _ZR0 = (
    0, 1024, 1280, 1536, 2560, 3584, 4608, 5632, 6656, 7680)


def _sigmoid(x):
    return 1.0 / (1.0 + jnp.exp(-x))


def _cparams(sem):
    return pltpu.CompilerParams(dimension_semantics=sem, vmem_limit_bytes=VMEM_LIMIT)


def _resident(shape):
    nd = len(shape)
    return pl.BlockSpec(shape, lambda *_: (0,) * nd, pipeline_mode=pl.Buffered(1))


def _mod_kernel(c_ref, w_ref, b_ref, o_ref):
    c = c_ref[...]
    s = c * _sigmoid(c)
    o_ref[...] = jnp.dot(s, w_ref[...], preferred_element_type=F32,
                         precision=lax.Precision.HIGHEST) + b_ref[...]


def _mods(cond, ada_w, ada_b):
    rows, d = cond.shape
    n = ada_w.shape[1]
    tn = 1536
    return pl.pallas_call(
        _mod_kernel,
        name="mods",
        grid=(n // tn,),
        in_specs=[pl.BlockSpec((rows, d), lambda j: (0, 0)),
                  pl.BlockSpec((d, tn), lambda j: (0, j)),
                  pl.BlockSpec((1, tn), lambda j: (0, j))],
        out_specs=pl.BlockSpec((rows, tn), lambda j: (0, j)),
        out_shape=jax.ShapeDtypeStruct((rows, n), F32),
        compiler_params=_cparams(("arbitrary",)),
    )(cond, ada_w, ada_b.reshape(1, n))


def _inproj_kernel(*refs, rope):
    if rope:
        (x_ref, mod_ref, n1_ref, w_ref, qg_ref, kg_ref, lbl_ref, cs_ref, sn_ref,
         q_ref, k_ref, v_ref, rq_ref, g_ref, ri_ref, rg_ref, gt_ref) = refs
    else:
        (x_ref, mod_ref, n1_ref, w_ref, qg_ref, kg_ref, lbl_ref,
         q_ref, k_ref, v_ref, rq_ref, g_ref, ri_ref, rg_ref, gt_ref) = refs
        cs_ref = sn_ref = None
    tm = x_ref.shape[0]

    x = x_ref[...]
    ms = jnp.mean(x * x, axis=-1, keepdims=True)
    h = x * lax.rsqrt(ms + EPS) * n1_ref[...]
    h = h * (1.0 + mod_ref[1:2, :]) + mod_ref[0:1, :]
    hb = h.astype(BF16)

    def proj(a, b):
        return jnp.dot(hb, w_ref[:, a:b], preferred_element_type=F32)

    gr = lax.broadcasted_iota(jnp.int32, (LANES, LANES), 0) // HEAD_DIM
    gc = lax.broadcasted_iota(jnp.int32, (LANES, LANES), 1) // HEAD_DIM
    gmat = jnp.where(gr == gc, 1.0 / HEAD_DIM, 0.0).astype(BF16)

    def headnorm(a, gain):
        ss = jnp.dot((a * a).astype(BF16), gmat, preferred_element_type=F32)
        return a * lax.rsqrt(ss + EPS) * gain

    if rope:
        lane = lax.broadcasted_iota(jnp.int32, (tm, LANES), 1)
        first = (lane % 32) < 16
        cs = cs_ref[...]
        sn = sn_ref[...]

        def rot(a):
            sw = jnp.where(first, pltpu.roll(a, LANES - 16, 1), pltpu.roll(a, 16, 1))
            return a * cs + sw * sn
    else:
        def rot(a):
            return a

    scale = HEAD_DIM ** -0.5 * LOG2E
    for c0 in range(0, 1024, 512):
        acc = proj(_Q0 + c0, _Q0 + c0 + 512)
        for s in range(4):
            col = c0 + s * LANES
            a = headnorm(acc[:, s * LANES:(s + 1) * LANES], qg_ref[:, col:col + LANES] * scale)
            q_ref[:, col:col + LANES] = rot(a).astype(BF16)

    acc = proj(_K0, _K0 + 512)
    for s in range(2):
        col = s * LANES
        a = headnorm(acc[:, col:col + LANES], kg_ref[:, col:col + LANES])
        k_ref[:, col:col + LANES] = rot(a)
    v_ref[...] = acc[:, 256:512]

    rscale = R_DK ** -0.5
    for c0 in range(0, 1024, 512):
        acc = proj(_RQ0 + c0, _RQ0 + c0 + 512)
        rq_ref[:, c0:c0 + 512] = (acc * _sigmoid(acc) * rscale).astype(BF16)

    l = lbl_ref[...]
    for d in range(2):
        a0 = l[2 * d:2 * d + 1, :]
        a1 = l[2 * d + 1:2 * d + 2, :]
        mx = jnp.maximum(a0, a1)
        e0 = jnp.exp(a0 - mx)
        e1 = jnp.exp(a1 - mx)
        lb = e0 / (e0 + e1)
        for c0 in range(0, 1024, 512):
            acc = proj(_RFF0 + d * 1024 + c0, _RFF0 + d * 1024 + c0 + 512)
            lbc = lb[:, c0:c0 + 512]
            f = lbc + (1.0 - lbc) * _sigmoid(acc)
            g_ref[:, d * 1024 + c0:d * 1024 + c0 + 512] = jnp.log(f)

    for c0 in range(0, 1024, 512):
        ri_ref[:, c0:c0 + 512] = proj(_RI0 + c0, _RI0 + c0 + 512).astype(BF16)
    for c0 in range(0, 1024, 512):
        acc = proj(_RG0 + c0, _RG0 + c0 + 512)
        rg_ref[:, c0:c0 + 512] = (acc * _sigmoid(acc)).astype(BF16)
    for c0 in range(0, 2048, 512):
        acc = proj(_ZA0 + c0, _ZA0 + c0 + 512)
        gt_ref[:, c0:c0 + 512] = _sigmoid(acc).astype(BF16)


def _inproj(x2d, mods, mod_row, seq_len, n1, w_in, qg, kg, lbl, cs, sn, tm):
    t, d = x2d.shape
    tiles_per_seq = seq_len // tm
    rope = cs is not None

    def tok(w):
        return pl.BlockSpec((tm, w), lambda i: (i, 0))

    in_specs = [tok(d),
                pl.BlockSpec((None, 6, d), lambda i: (mod_row(i // tiles_per_seq), 0, 0)),
                _resident((1, d)),
                _resident((d, IN_W)),
                _resident((1, 1024)),
                _resident((1, KV_W)),
                _resident((4, 1024))]
    args = [x2d, mods, n1, w_in, qg, kg, lbl]
    if rope:
        in_specs += [pl.BlockSpec((tm, LANES), lambda i: (i % tiles_per_seq, 0))] * 2
        args += [cs, sn]
    out_shape = [jax.ShapeDtypeStruct((t, 1024), BF16),
                 jax.ShapeDtypeStruct((t, KV_W), F32),
                 jax.ShapeDtypeStruct((t, KV_W), F32),
                 jax.ShapeDtypeStruct((t, 1024), BF16),
                 jax.ShapeDtypeStruct((t, 2048), F32),
                 jax.ShapeDtypeStruct((t, 1024), BF16),
                 jax.ShapeDtypeStruct((t, 1024), BF16),
                 jax.ShapeDtypeStruct((t, 2048), BF16)]
    out_specs = [tok(s.shape[1]) for s in out_shape]
    return pl.pallas_call(
        functools.partial(_inproj_kernel, rope=rope),
        name="inproj_rope" if rope else "inproj",
        grid=(t // tm,),
        in_specs=in_specs, out_specs=out_specs, out_shape=out_shape,
        compiler_params=_cparams(("arbitrary",)),
    )(*args)


def _attn_kernel(*refs, n_lat, n_ctx, tq, pps, kchunk):
    if n_ctx:
        q_ref, k_ref, v_ref, ck_ref, cv_ref, o_ref, kpad, vt, p0_scr, p1_scr = refs
    else:
        q_ref, k_ref, v_ref, o_ref, kpad, vt, p0_scr, p1_scr = refs
        ck_ref = cv_ref = None
    lk = n_lat + n_ctx
    first_step = jnp.logical_and(pl.program_id(1) == 0, pl.program_id(2) == 0)

    @pl.when(first_step)
    def _build():
        def place(dst, rows, ks):
            lane = lax.broadcasted_iota(jnp.int32, ks.shape, 1)
            lo = jnp.where(lane < HEAD_DIM, ks, 0.0)
            hi = jnp.where(lane >= HEAD_DIM, ks, 0.0)
            kpad[dst + 0, rows, :] = lo.astype(BF16)
            kpad[dst + 1, rows, :] = pltpu.roll(lo, HEAD_DIM, 1).astype(BF16)
            kpad[dst + 2, rows, :] = pltpu.roll(hi, HEAD_DIM, 1).astype(BF16)
            kpad[dst + 3, rows, :] = hi.astype(BF16)

        def place_v(s, cols_out, vs):
            vtr = vs.T.astype(BF16)
            vt[2 * s, 0:HEAD_DIM, cols_out] = vtr[0:HEAD_DIM, :]
            vt[2 * s + 1, 0:HEAD_DIM, cols_out] = vtr[HEAD_DIM:2 * HEAD_DIM, :]

        ones_rows = jnp.where(lax.broadcasted_iota(jnp.int32, (VT_ROWS - HEAD_DIM, lk), 0) == 0,
                              1.0, 0.0).astype(BF16)
        for kvh in range(N_KV_HEADS):
            vt[kvh, HEAD_DIM:VT_ROWS, :] = ones_rows
        for s in range(2):
            cols = slice(s * LANES, (s + 1) * LANES)
            place(4 * s, slice(0, n_lat), k_ref[:, cols])
            place_v(s, slice(0, n_lat), v_ref[:, cols])
            if n_ctx:
                place(4 * s, slice(n_lat, lk), ck_ref[:, cols])
                place_v(s, slice(n_lat, lk), cv_ref[:, cols])

    n_chunks = lk // kchunk
    p_bufs = (p0_scr, p1_scr)
    pair0 = pl.program_id(2) * pps
    heads = [(j, parity) for j in range(pps) for parity in range(2)]

    def score_chunk(h, c):
        j, parity = heads[h]
        kidx = 2 * ((pair0 + j) // 2) + parity
        s = lax.dot_general(kpad[kidx, c * kchunk:(c + 1) * kchunk, :], q_ref[:, j * LANES:(j + 1) * LANES],
                            (((1,), (1,)), ((), ())), preferred_element_type=F32)
        return s, jnp.max(s.reshape(kchunk // 8, 8, tq), axis=0)

    def softmax_chunk(s, m8, p_scr, c):
        for r0 in range(0, kchunk, SM_ROWS):
            sb = s[r0:r0 + SM_ROWS, :].reshape(SM_ROWS // 8, 8, tq)
            p_scr[c * kchunk + r0:c * kchunk + r0 + SM_ROWS, :] = (
                jnp.exp2(sb - m8).reshape(SM_ROWS, tq).astype(BF16))

    def pv(h):
        j, _ = heads[h]
        ot = jnp.dot(vt[(pair0 + j) // 2], p_bufs[h % 2][...], preferred_element_type=F32)
        return ot[0:HEAD_DIM, :] * (1.0 / ot[HEAD_DIM:HEAD_DIM + 1, :])

    def reduce_max(parts):
        mx = parts[0]
        for part in parts[1:]:
            mx = jnp.maximum(mx, part)
        return jnp.broadcast_to(jnp.max(mx, axis=0, keepdims=True), (8, tq))

    cur = [score_chunk(0, c) for c in range(n_chunks)]
    outs = []
    for h in range(len(heads)):
        m8 = reduce_max([mx for _, mx in cur])
        nxt = []
        for c in range(n_chunks):
            if h + 1 < len(heads):
                nxt.append(score_chunk(h + 1, c))
            softmax_chunk(cur[c][0], m8, p_bufs[h % 2], c)
        outs.append(pv(h))
        cur = nxt
        if h % 2 == 1:
            j = heads[h][0]
            o2t = jnp.concatenate(outs[-2:], axis=0)
            o_ref[:, j * LANES:(j + 1) * LANES] = o2t.T.astype(BF16)


def _attention(q, k, v, ck, cv, batch, seq_len, tq, pps):
    t = q.shape[0]
    n_ctx = 0 if ck is None else ck.shape[1]
    lk = seq_len + n_ctx
    n_qt = seq_len // tq
    n_pp = (N_HEADS // 2) // pps
    kchunk = min(lk, 512)
    in_specs = [pl.BlockSpec((tq, pps * LANES), lambda b, i, p: (b * n_qt + i, p)),
                pl.BlockSpec((seq_len, KV_W), lambda b, i, p: (b, 0)),
                pl.BlockSpec((seq_len, KV_W), lambda b, i, p: (b, 0))]
    args = [q, k, v]
    if n_ctx:
        in_specs += [pl.BlockSpec((None, n_ctx, KV_W), lambda b, i, p: (b, 0, 0))] * 2
        args += [ck, cv]
    return pl.pallas_call(
        functools.partial(_attn_kernel, n_lat=seq_len, n_ctx=n_ctx, tq=tq, pps=pps, kchunk=kchunk),
        name="attn_ctx" if n_ctx else "attn",
        grid=(batch, n_qt, n_pp),
        in_specs=in_specs,
        out_specs=pl.BlockSpec((tq, pps * LANES), lambda b, i, p: (b * n_qt + i, p)),
        out_shape=jax.ShapeDtypeStruct((t, 1024), BF16),
        scratch_shapes=[pltpu.VMEM((2 * N_KV_HEADS, lk, LANES), BF16),
                        pltpu.VMEM((N_KV_HEADS, VT_ROWS, lk), BF16),
                        pltpu.VMEM((lk, tq), BF16), pltpu.VMEM((lk, tq), BF16)],
        compiler_params=_cparams(("arbitrary", "arbitrary", "arbitrary")),
    )(*args)


def _rec_kernel(*refs, seq_len, has_s0, emit_state):
    refs = list(refs)
    rq_ref, gf_ref, gb_ref, ri_ref, rg_ref, gain_ref = refs[:6]
    pos = 6
    if has_s0:
        s0f_ref, s0b_ref = refs[pos:pos + 2]
        pos += 2
    rec_ref = refs[pos]
    pos += 1
    if emit_state:
        sf_ref, sb_ref = refs[pos:pos + 2]
        pos += 2
    qd_scr, kdf_scr, kdb_scr, ke_scr, d_scr, u_scr, st_scr, a_scr, o_scr = refs[pos:]
    L = seq_len
    nc = L // CHUNK
    unroll = min(nc, 8)
    tpos =lax.broadcasted_iota(jnp.int32, (L, R_DK), 0) % CHUNK

    def chunk_cumsum(g, reverse):
        b = g
        for s in (1, 2, 4, 8, 16, 32):
            if reverse:
                sh = pltpu.roll(b, L - s, 0)
                b = b + jnp.where(tpos < CHUNK - s, sh, 0.0)
            else:
                sh = pltpu.roll(b, s, 0)
                b = b + jnp.where(tpos >= s, sh, 0.0)
        return b

    rq = rq_ref[...].astype(F32)
    for d, (g_ref, kd_scr) in enumerate(((gf_ref, kdf_scr), (gb_ref, kdb_scr))):
        g = g_ref[...]
        kk = 1.0 - jnp.exp(g)
        b = chunk_cumsum(g, reverse=(d == 1))
        tot = jnp.sum(g.reshape(nc, CHUNK, R_DK), axis=1, keepdims=True)
        tot = jnp.broadcast_to(tot, (nc, CHUNK, R_DK)).reshape(L, R_DK)
        qd_scr[:, d * R_DK:(d + 1) * R_DK] = (rq * jnp.exp(b)).astype(BF16)
        kd_scr[...] = (kk * jnp.exp(-b)).astype(BF16)
        ke_scr[:, d * R_DK:(d + 1) * R_DK] = (kk * jnp.exp(tot - b)).astype(BF16)
        d_scr[:, d * R_DK:(d + 1) * R_DK] = jnp.exp(tot)

    def rows(c):
        return pl.ds(pl.multiple_of(c * CHUNK, CHUNK), CHUNK)

    def u_body(c, carry):
        vc = ri_ref[rows(c), :]
        tn = (((0,), (0,)), ((), ()))
        u_scr[c] = lax.dot_general(vc, ke_scr[rows(c), :], tn, preferred_element_type=F32)
        return carry

    lax.fori_loop(0, nc, u_body, 0, unroll=unroll)

    if has_s0:
        sf0 = s0f_ref[...].T
        sb0 = s0b_ref[...].T
    else:
        sf0 = jnp.zeros((R_DK, R_DK), F32)
        sb0 = jnp.zeros((R_DK, R_DK), F32)

    def scan_body(i, carry):
        sf, sb = carry
        cb = nc - 1 - i
        st_scr[i, :, 0:R_DK] = sf.astype(BF16)
        st_scr[cb, :, R_DK:2 * R_DK] = sb.astype(BF16)
        df = d_scr[pl.ds(pl.multiple_of(i * CHUNK, CHUNK), 1), 0:R_DK]
        db = d_scr[pl.ds(pl.multiple_of(cb * CHUNK, CHUNK), 1), R_DK:2 * R_DK]
        sf = df * sf + u_scr[i, :, 0:R_DK]
        sb = db * sb + u_scr[cb, :, R_DK:2 * R_DK]
        return sf, sb

    sf, sb = lax.fori_loop(0, nc, scan_body, (sf0, sb0))
    if emit_state:
        sf_ref[...] = sf.T
        sb_ref[...] = sb.T

    ti = lax.broadcasted_iota(jnp.int32, (CHUNK, CHUNK), 0)
    si = lax.broadcasted_iota(jnp.int32, (CHUNK, CHUNK), 1)
    nt = (((1,), (1,)), ((), ()))
    gain = gain_ref[...]

    def a_body(c, carry):
        qd = qd_scr[rows(c), :]
        af = lax.dot_general(qd[:, 0:R_DK], kdf_scr[rows(c), :], nt, preferred_element_type=F32)
        ab = lax.dot_general(qd[:, R_DK:2 * R_DK], kdb_scr[rows(c), :], nt, preferred_element_type=F32)
        a = jnp.where(si <= ti, af, 0.0) + jnp.where(si >= ti, ab, 0.0)
        a_scr[rows(c), :] = a.astype(BF16)
        return carry

    lax.fori_loop(0, nc, a_body, 0, unroll=unroll)

    def o_body(c, carry):
        o = jnp.dot(a_scr[rows(c), :], ri_ref[rows(c), :], preferred_element_type=F32)
        o = o + lax.dot_general(qd_scr[rows(c), :], st_scr[c], nt, preferred_element_type=F32)
        o_scr[rows(c), :] = o
        return carry

    lax.fori_loop(0, nc, o_body, 0, unroll=unroll)

    o = o_scr[...]
    ms = jnp.mean(o * o, axis=-1, keepdims=True)
    y = o * lax.rsqrt(ms + EPS) * gain * rg_ref[...].astype(F32)
    rec_ref[...] = y.astype(BF16)


def _recurrence(rq, g, ri, rg, gain, s0, batch, seq_len, emit_state):
    t = rq.shape[0]
    nc = seq_len // CHUNK
    has_s0 = s0 is not None

    def col(off):
        return pl.BlockSpec((seq_len, R_DK), lambda b, h: (b, h + off))

    in_specs = [col(0), col(0), col(R_HEADS), col(0), col(0), _resident((1, R_DK))]
    args = [rq, g, g, ri, rg, gain]
    if has_s0:
        in_specs += [pl.BlockSpec((None, R_DK, R_DK), lambda b, h: (b * 2 * R_HEADS + h, 0, 0)),
                     pl.BlockSpec((None, R_DK, R_DK), lambda b, h: (b * 2 * R_HEADS + R_HEADS + h, 0, 0))]
        args += [s0, s0]
    out_shape = [jax.ShapeDtypeStruct((t, 1024), BF16)]
    out_specs = [col(0)]
    if emit_state:
        st_shape = jax.ShapeDtypeStruct((batch * R_HEADS, R_DK, R_DK), F32)
        out_shape += [st_shape, st_shape]
        out_specs += [pl.BlockSpec((None, R_DK, R_DK), lambda b, h: (b * R_HEADS + h, 0, 0))] * 2
    scratch = [pltpu.VMEM((seq_len, 2 * R_DK), BF16),
               pltpu.VMEM((seq_len, R_DK), BF16),
               pltpu.VMEM((seq_len, R_DK), BF16),
               pltpu.VMEM((seq_len, 2 * R_DK), BF16),
               pltpu.VMEM((seq_len, 2 * R_DK), F32),
               pltpu.VMEM((nc, R_DK, 2 * R_DK), F32),
               pltpu.VMEM((nc, R_DK, 2 * R_DK), BF16),
               pltpu.VMEM((seq_len, CHUNK), BF16),
               pltpu.VMEM((seq_len, R_DK), F32)]
    return pl.pallas_call(
        functools.partial(_rec_kernel, seq_len=seq_len, has_s0=has_s0, emit_state=emit_state),
        name="rec_s0" if has_s0 else "rec",
        grid=(batch, R_HEADS),
        in_specs=in_specs, out_specs=out_specs, out_shape=out_shape,
        scratch_shapes=scratch,
        compiler_params=_cparams(("arbitrary", "arbitrary")),
    )(*args)


def _merge_kernel(x_ref, attn_ref, rec_ref, gt_ref, wo_ref, mod_ref, n2_ref, x1_ref, h2_ref):
    ga = gt_ref[:, 0:1024].astype(F32)
    gr = gt_ref[:, 1024:2048].astype(F32)
    m = ga * attn_ref[...].astype(F32) + gr * rec_ref[...].astype(F32)
    out = jnp.dot(m.astype(BF16), wo_ref[...], preferred_element_type=F32)
    x1 = x_ref[...] + mod_ref[2:3, :] * out
    x1_ref[...] = x1
    ms = jnp.mean(x1 * x1, axis=-1, keepdims=True)
    h2 = x1 * lax.rsqrt(ms + EPS) * n2_ref[...]
    h2 = h2 * (1.0 + mod_ref[4:5, :]) + mod_ref[3:4, :]
    h2_ref[...] = h2.astype(BF16)


def _merge(x2d, attn, rec, gates, w_o, mods, mod_row, seq_len, n2, tm):
    t, d = x2d.shape
    tiles_per_seq = seq_len // tm

    def tok(w):
        return pl.BlockSpec((tm, w), lambda i: (i, 0))

    return pl.pallas_call(
        _merge_kernel,
        name="merge",
        grid=(t // tm,),
        in_specs=[tok(d), tok(d), tok(d), tok(2 * d), _resident((d, d)),
                  pl.BlockSpec((None, 6, d), lambda i: (mod_row(i // tiles_per_seq), 0, 0)),
                  _resident((1, d))],
        out_specs=[tok(d), tok(d)],
        out_shape=[jax.ShapeDtypeStruct((t, d), F32), jax.ShapeDtypeStruct((t, d), BF16)],
        compiler_params=_cparams(("arbitrary",)),
    )(x2d, attn, rec, gates, w_o, mods, n2)


def _ffn_kernel(hp_ref, h_ref, hn_ref, x1_ref, wup_ref, cw_ref, cb_ref, wdn_ref, mod_ref, fn_ref,
                y_ref, hbuf, act_scr, *, tiles_per_seq, ffc):
    tm = h_ref.shape[0]
    i = pl.program_id(0)
    ts = i % tiles_per_seq
    zero = jnp.zeros((HALO, D_MODEL), BF16)
    hbuf[0:HALO, :] = jnp.where(ts == 0, zero, hp_ref[...])
    hbuf[HALO:HALO + tm, :] = h_ref[...]
    hbuf[HALO + tm:HALO + tm + HALO, :] = jnp.where(ts == tiles_per_seq - 1, zero, hn_ref[...])
    hb = hbuf[...]
    mrows = tm + 2 * HALO

    def conv(col):
        u = jnp.dot(hb, wup_ref[:, col:col + ffc], preferred_element_type=F32)
        up = pltpu.roll(u, 1, 0)[HALO:HALO + tm, :]
        un = pltpu.roll(u, mrows - 1, 0)[HALO:HALO + tm, :]
        uc = u[HALO:HALO + tm, :]
        w = cw_ref[:, col:col + ffc]
        return up * w[0:1, :] + uc * w[1:2, :] + un * w[2:3, :] + cb_ref[:, col:col + ffc]

    for c0 in range(0, D_FF, ffc):
        a = conv(c0)
        b = conv(D_FF + c0)
        act_scr[:, c0:c0 + ffc] = (a * _sigmoid(a) * b).astype(BF16)

    f = jnp.dot(act_scr[...], wdn_ref[...], preferred_element_type=F32)
    x2 = x1_ref[...] + mod_ref[5:6, :] * f
    ms = jnp.mean(x2 * x2, axis=-1, keepdims=True)
    y_ref[...] = x2 * lax.rsqrt(ms + EPS) * fn_ref[...]


def _ffn(h2, x1, w_up, conv_w, conv_b, w_down, mods, mod_row, seq_len, fnorm, tm):
    t, d = x1.shape
    tiles_per_seq = seq_len // tm
    hb = tm // HALO
    n_hb = t // HALO

    return pl.pallas_call(
        functools.partial(_ffn_kernel, tiles_per_seq=tiles_per_seq, ffc=256),
        name="ffn",
        grid=(t // tm,),
        in_specs=[pl.BlockSpec((HALO, d), lambda i: (jnp.maximum(i * hb - 1, 0), 0)),
                  pl.BlockSpec((tm, d), lambda i: (i, 0)),
                  pl.BlockSpec((HALO, d), lambda i: (jnp.minimum((i + 1) * hb, n_hb - 1), 0)),
                  pl.BlockSpec((tm, d), lambda i: (i, 0)),
                  _resident((d, 2 * D_FF)),
                  _resident((3, 2 * D_FF)),
                  _resident((1, 2 * D_FF)),
                  _resident((D_FF, d)),
                  pl.BlockSpec((None, 6, d), lambda i: (mod_row(i // tiles_per_seq), 0, 0)),
                  _resident((1, d))],
        out_specs=pl.BlockSpec((tm, d), lambda i: (i, 0)),
        out_shape=jax.ShapeDtypeStruct((t, d), F32),
        scratch_shapes=[pltpu.VMEM((tm + 2 * HALO, d), BF16),
                        pltpu.VMEM((tm, D_FF), BF16)],
        compiler_params=_cparams(("arbitrary",)),
    )(h2, h2, h2, x1, w_up, conv_w, conv_b, w_down, mods, fnorm)


def _rope_tables(n_tokens):
    rows = n_tokens // GRID_W
    row = jnp.repeat(jnp.arange(rows, dtype=F32), GRID_W)
    colp = jnp.tile(jnp.arange(GRID_W, dtype=F32), rows)
    half = HEAD_DIM // 2
    inv_freq = 1.0 / (ROPE_THETA ** (jnp.arange(0, half, 2, dtype=F32) / half))
    ar = row[:, None] * inv_freq
    ac = colp[:, None] * inv_freq
    ang = jnp.concatenate([ar, ar, ac, ac], axis=-1)
    sign = jnp.asarray(np.tile(np.repeat(np.array([-1.0, 1.0], np.float32), 16), 2))
    cs = jnp.cos(ang)
    sn = jnp.sin(ang) * sign
    return jnp.tile(cs, (1, 2)), jnp.tile(sn, (1, 2))


def _group(x, mods, mod_row, w, ctx, rope_tabs, tq, pps, tm_ffn, emit_state):
    batch, seq_len, d = x.shape
    x2d = x.reshape(batch * seq_len, d)
    cs, sn = rope_tabs if rope_tabs is not None else (None, None)
    q, k, v, rq, g, ri, rg, gates = _inproj(
        x2d, mods, mod_row, seq_len, w["n1"], w["w_in"], w["qg"], w["kg"], w["lbl"], cs, sn, tm=256)
    ck, cv, s0 = ctx
    attn = _attention(q, k, v, ck, cv, batch, seq_len, tq, pps)
    rec_out = _recurrence(rq, g, ri, rg, w["rgain"], s0, batch, seq_len, emit_state)
    rec = rec_out[0]
    x1, h2 = _merge(x2d, attn, rec, gates, w["w_o"], mods, mod_row, seq_len, w["n2"], tm=256)
    y = _ffn(h2, x1, w["w_up"], w["conv_w"], w["conv_b"], w["w_down"], mods, mod_row, seq_len,
             w["fnorm"], tm=tm_ffn)
    return y.reshape(batch, seq_len, d), k, v, rec_out[1:]


def kernel(x_prompt, x_sample, c, cache_k, cache_v, state_hgrn, c_ctx, ada_w, ada_b, norm1, norm2,
           w_in, q_norm, k_norm, hgrn_lb_logits, hgrn_norm, w_o, w_up, conv_w, conv_b, w_down, final_norm):
    bp, lp, d = x_prompt.shape
    bs, ls, _ = x_sample.shape
    n_ctx = cache_k.shape[2]

    cond = jnp.zeros((16, d), F32).at[0].set(c_ctx).at[1:1 + bs].set(c)
    mods = _mods(cond, ada_w[0], ada_b[0]).reshape(16, 6, d)

    w = dict(
        n1=norm1[0].reshape(1, d), n2=norm2[0].reshape(1, d), fnorm=final_norm.reshape(1, d),
        w_in=w_in[0].astype(BF16), w_o=w_o[0].astype(BF16),
        w_up=w_up[0].astype(BF16), w_down=w_down[0].astype(BF16),
        conv_w=conv_w[0], conv_b=conv_b[0].reshape(1, 2 * D_FF),
        qg=jnp.tile(q_norm[0], N_HEADS).reshape(1, 1024),
        kg=jnp.tile(k_norm[0], N_KV_HEADS).reshape(1, KV_W),
        lbl=hgrn_lb_logits.reshape(4, 1024),
        rgain=hgrn_norm[0].reshape(1, R_DK),
    )

    y_p, k_p, v_p, st = _group(x_prompt, mods, lambda b: 0, w, (None, None, None), None,
                               tq=lp, pps=8, tm_ffn=lp, emit_state=True)
    ctx = (cache_k[:, 0].reshape(bs, n_ctx, KV_W), cache_v[:, 0].reshape(bs, n_ctx, KV_W),
           state_hgrn[:, 0].reshape(bs * 2 * R_HEADS, R_DK, R_DK))
    y_s, _, _, _ = _group(x_sample, mods, lambda b: b + 1, w, ctx, _rope_tables(ls),
                          tq=512, pps=2, tm_ffn=512, emit_state=False)

    sf, sb = st
    new_state = jnp.stack([sf.reshape(bp, R_HEADS, R_DK, R_DK),
                           sb.reshape(bp, R_HEADS, R_DK, R_DK)], axis=1)[:, None]
    new_k = k_p.reshape(bp, 1, lp, N_KV_HEADS, HEAD_DIM)
    new_v = v_p.reshape(bp, 1, lp, N_KV_HEADS, HEAD_DIM)
    return (y_p, y_s, new_k, new_v, new_state)
```

```python
import functools

import jax
import jax.numpy as jnp
import numpy as np
from jax import lax
from jax.experimental import pallas as pl
from jax.experimental.pallas import tpu as pltpu

F32 = jnp.float32
BF16 = jnp.bfloat16

D_MODEL = 1024
GRID_W = 64
HEAD_DIM = 64
N_HEADS = 16
N_KV_HEADS = 4
KV_W = N_KV_HEADS * HEAD_DIM
ROPE_THETA = 10000.0
R_DK = 128
R_HEADS = 8
CHUNK = 64
D_FF = 2816
EPS = 1e-6
LOG2E = 1.4426950408889634
IN_W = 8704

LANES = 128
HALO = 16
VT_ROWS = HEAD_DIM + 16
SM_ROWS = 32
VMEM_LIMIT = 56 * 1024 * 1024

_Q0, _K0, _V0, _RQ0, _RFF0, _RFB0, _RI0, _RG0, _ZA0, _ZR0 = (
    0, 1024, 1280, 1536, 2560, 3584, 4608, 5632, 6656, 7680)


def _sigmoid(x):
    return 1.0 / (1.0 + jnp.exp(-x))


def _cparams(sem):
    return pltpu.CompilerParams(dimension_semantics=sem, vmem_limit_bytes=VMEM_LIMIT)


def _resident(shape):
    nd = len(shape)
    return pl.BlockSpec(shape, lambda *_: (0,) * nd, pipeline_mode=pl.Buffered(1))


def _mod_kernel(c_ref, w_ref, b_ref, o_ref):
    c = c_ref[...]
    s = c * _sigmoid(c)
    o_ref[...] = jnp.dot(s, w_ref[...], preferred_element_type=F32,
                         precision=lax.Precision.HIGHEST) + b_ref[...]


def _mods(cond, ada_w, ada_b):
    rows, d = cond.shape
    n = ada_w.shape[1]
    tn = 1536
    return pl.pallas_call(
        _mod_kernel,
        name="mods",
        grid=(n // tn,),
        in_specs=[pl.BlockSpec((rows, d), lambda j: (0, 0)),
                  pl.BlockSpec((d, tn), lambda j: (0, j)),
                  pl.BlockSpec((1, tn), lambda j: (0, j))],
        out_specs=pl.BlockSpec((rows, tn), lambda j: (0, j)),
        out_shape=jax.ShapeDtypeStruct((rows, n), F32),
        compiler_params=_cparams(("arbitrary",)),
    )(cond, ada_w, ada_b.reshape(1, n))


def _inproj_kernel(*refs, rope):
    if rope:
        (x_ref, mod_ref, n1_ref, w_ref, qg_ref, kg_ref, lbl_ref, cs_ref, sn_ref,
         q_ref, k_ref, v_ref, rq_ref, g_ref, ri_ref, rg_ref, gt_ref) = refs
    else:
        (x_ref, mod_ref, n1_ref, w_ref, qg_ref, kg_ref, lbl_ref,
         q_ref, k_ref, v_ref, rq_ref, g_ref, ri_ref, rg_ref, gt_ref) = refs
        cs_ref = sn_ref = None
    tm = x_ref.shape[0]

    x = x_ref[...]
    ms = jnp.mean(x * x, axis=-1, keepdims=True)
    h = x * lax.rsqrt(ms + EPS) * n1_ref[...]
    h = h * (1.0 + mod_ref[1:2, :]) + mod_ref[0:1, :]
    hb = h.astype(BF16)

    def proj(a, b):
        return jnp.dot(hb, w_ref[:, a:b], preferred_element_type=F32)

    gr = lax.broadcasted_iota(jnp.int32, (LANES, LANES), 0) // HEAD_DIM
    gc = lax.broadcasted_iota(jnp.int32, (LANES, LANES), 1) // HEAD_DIM
    gmat = jnp.where(gr == gc, 1.0 / HEAD_DIM, 0.0).astype(BF16)

    def headnorm(a, gain):
        ss = jnp.dot((a * a).astype(BF16), gmat, preferred_element_type=F32)
        return a * lax.rsqrt(ss + EPS) * gain

    if rope:
        lane = lax.broadcasted_iota(jnp.int32, (tm, LANES), 1)
        first = (lane % 32) < 16
        cs = cs_ref[...]
        sn = sn_ref[...]

        def rot(a):
            sw = jnp.where(first, pltpu.roll(a, LANES - 16, 1), pltpu.roll(a, 16, 1))
            return a * cs + sw * sn
    else:
        def rot(a):
            return a

    scale = HEAD_DIM ** -0.5 * LOG2E
    for c0 in range(0, 1024, 512):
        acc = proj(_Q0 + c0, _Q0 + c0 + 512)
        for s in range(4):
            col = c0 + s * LANES
            a = headnorm(acc[:, s * LANES:(s + 1) * LANES], qg_ref[:, col:col + LANES] * scale)
            q_ref[:, col:col + LANES] = rot(a).astype(BF16)

    acc = proj(_K0, _K0 + 512)
    for s in range(2):
        col = s * LANES
        a = headnorm(acc[:, col:col + LANES], kg_ref[:, col:col + LANES])
        k_ref[:, col:col + LANES] = rot(a)
    v_ref[...] = acc[:, 256:512]

    rscale = R_DK ** -0.5
    for c0 in range(0, 1024, 512):
        acc = proj(_RQ0 + c0, _RQ0 + c0 + 512)
        rq_ref[:, c0:c0 + 512] = (acc * _sigmoid(acc) * rscale).astype(BF16)

    l = lbl_ref[...]
    for d in range(2):
        a0 = l[2 * d:2 * d + 1, :]
        a1 = l[2 * d + 1:2 * d + 2, :]
        mx = jnp.maximum(a0, a1)
        e0 = jnp.exp(a0 - mx)
        e1 = jnp.exp(a1 - mx)
        lb = e0 / (e0 + e1)
        for c0 in range(0, 1024, 512):
            acc = proj(_RFF0 + d * 1024 + c0, _RFF0 + d * 1024 + c0 + 512)
            lbc = lb[:, c0:c0 + 512]
            f = lbc + (1.0 - lbc) * _sigmoid(acc)
            g_ref[:, d * 1024 + c0:d * 1024 + c0 + 512] = jnp.log(f)

    for c0 in range(0, 1024, 512):
        ri_ref[:, c0:c0 + 512] = proj(_RI0 + c0, _RI0 + c0 + 512).astype(BF16)
    for c0 in range(0, 1024, 512):
        acc = proj(_RG0 + c0, _RG0 + c0 + 512)
        rg_ref[:, c0:c0 + 512] = (acc * _sigmoid(acc)).astype(BF16)
    for c0 in range(0, 2048, 512):
        acc = proj(_ZA0 + c0, _ZA0 + c0 + 512)
        gt_ref[:, c0:c0 + 512] = _sigmoid(acc).astype(BF16)


def _inproj(x2d, mods, mod_row, seq_len, n1, w_in, qg, kg, lbl, cs, sn, tm):
    t, d = x2d.shape
    tiles_per_seq = seq_len // tm
    rope = cs is not None

    def tok(w):
        return pl.BlockSpec((tm, w), lambda i: (i, 0))

    in_specs = [tok(d),
                pl.BlockSpec((None, 6, d), lambda i: (mod_row(i // tiles_per_seq), 0, 0)),
                _resident((1, d)),
                _resident((d, IN_W)),
                _resident((1, 1024)),
                _resident((1, KV_W)),
                _resident((4, 1024))]
    args = [x2d, mods, n1, w_in, qg, kg, lbl]
    if rope:
        in_specs += [pl.BlockSpec((tm, LANES), lambda i: (i % tiles_per_seq, 0))] * 2
        args += [cs, sn]
    out_shape = [jax.ShapeDtypeStruct((t, 1024), BF16),
                 jax.ShapeDtypeStruct((t, KV_W), F32),
                 jax.ShapeDtypeStruct((t, KV_W), F32),
                 jax.ShapeDtypeStruct((t, 1024), BF16),
                 jax.ShapeDtypeStruct((t, 2048), F32),
                 jax.ShapeDtypeStruct((t, 1024), BF16),
                 jax.ShapeDtypeStruct((t, 1024), BF16),
                 jax.ShapeDtypeStruct((t, 2048), BF16)]
    out_specs = [tok(s.shape[1]) for s in out_shape]
    return pl.pallas_call(
        functools.partial(_inproj_kernel, rope=rope),
        name="inproj_rope" if rope else "inproj",
        grid=(t // tm,),
        in_specs=in_specs, out_specs=out_specs, out_shape=out_shape,
        compiler_params=_cparams(("arbitrary",)),
    )(*args)


def _attn_kernel(*refs, n_lat, n_ctx, tq, pps, kchunk):
    if n_ctx:
        q_ref, k_ref, v_ref, ck_ref, cv_ref, o_ref, kpad, vt, p0_scr, p1_scr = refs
    else:
        q_ref, k_ref, v_ref, o_ref, kpad, vt, p0_scr, p1_scr = refs
        ck_ref = cv_ref = None
    lk = n_lat + n_ctx
    first_step = jnp.logical_and(pl.program_id(1) == 0, pl.program_id(2) == 0)

    @pl.when(first_step)
    def _build():
        def place(dst, rows, ks):
            lane = lax.broadcasted_iota(jnp.int32, ks.shape, 1)
            lo = jnp.where(lane < HEAD_DIM, ks, 0.0)
            hi = jnp.where(lane >= HEAD_DIM, ks, 0.0)
            kpad[dst + 0, rows, :] = lo.astype(BF16)
            kpad[dst + 1, rows, :] = pltpu.roll(lo, HEAD_DIM, 1).astype(BF16)
            kpad[dst + 2, rows, :] = pltpu.roll(hi, HEAD_DIM, 1).astype(BF16)
            kpad[dst + 3, rows, :] = hi.astype(BF16)

        def place_v(s, cols_out, vs):
            vtr = vs.T.astype(BF16)
            vt[2 * s, 0:HEAD_DIM, cols_out] = vtr[0:HEAD_DIM, :]
            vt[2 * s + 1, 0:HEAD_DIM, cols_out] = vtr[HEAD_DIM:2 * HEAD_DIM, :]

        ones_rows = jnp.where(lax.broadcasted_iota(jnp.int32, (VT_ROWS - HEAD_DIM, lk), 0) == 0,
                              1.0, 0.0).astype(BF16)
        for kvh in range(N_KV_HEADS):
            vt[kvh, HEAD_DIM:VT_ROWS, :] = ones_rows
        for s in range(2):
            cols = slice(s * LANES, (s + 1) * LANES)
            place(4 * s, slice(0, n_lat), k_ref[:, cols])
            place_v(s, slice(0, n_lat), v_ref[:, cols])
            if n_ctx:
                place(4 * s, slice(n_lat, lk), ck_ref[:, cols])
                place_v(s, slice(n_lat, lk), cv_ref[:, cols])

    n_chunks = lk // kchunk
    p_bufs = (p0_scr, p1_scr)
    pair0 = pl.program_id(2) * pps
    heads = [(j, parity) for j in range(pps) for parity in range(2)]

    def score_chunk(h, c):
        j, parity = heads[h]
        kidx = 2 * ((pair0 + j) // 2) + parity
        s = lax.dot_general(kpad[kidx, c * kchunk:(c + 1) * kchunk, :], q_ref[:, j * LANES:(j + 1) * LANES],
                            (((1,), (1,)), ((), ())), preferred_element_type=F32)
        return s, jnp.max(s.reshape(kchunk // 8, 8, tq), axis=0)

    def softmax_chunk(s, m8, p_scr, c):
        for r0 in range(0, kchunk, SM_ROWS):
            sb = s[r0:r0 + SM_ROWS, :].reshape(SM_ROWS // 8, 8, tq)
            p_scr[c * kchunk + r0:c * kchunk + r0 + SM_ROWS, :] = (
                jnp.exp2(sb - m8).reshape(SM_ROWS, tq).astype(BF16))

    def pv(h):
        j, _ = heads[h]
        ot = jnp.dot(vt[(pair0 + j) // 2], p_bufs[h % 2][...], preferred_element_type=F32)
        return ot[0:HEAD_DIM, :] * (1.0 / ot[HEAD_DIM:HEAD_DIM + 1, :])

    def reduce_max(parts):
        mx = parts[0]
        for part in parts[1:]:
            mx = jnp.maximum(mx, part)
        return jnp.broadcast_to(jnp.max(mx, axis=0, keepdims=True), (8, tq))

    cur = [score_chunk(0, c) for c in range(n_chunks)]
    outs = []
    for h in range(len(heads)):
        m8 = reduce_max([mx for _, mx in cur])
        nxt = []
        for c in range(n_chunks):
            if h + 1 < len(heads):
                nxt.append(score_chunk(h + 1, c))
            softmax_chunk(cur[c][0], m8, p_bufs[h % 2], c)
        outs.append(pv(h))
        cur = nxt
        if h % 2 == 1:
            j = heads[h][0]
            o2t = jnp.concatenate(outs[-2:], axis=0)
            o_ref[:, j * LANES:(j + 1) * LANES] = o2t.T.astype(BF16)


def _attention(q, k, v, ck, cv, batch, seq_len, tq, pps):
    t = q.shape[0]
    n_ctx = 0 if ck is None else ck.shape[1]
    lk = seq_len + n_ctx
    n_qt = seq_len // tq
    n_pp = (N_HEADS // 2) // pps
    kchunk = min(lk, 512)
    in_specs = [pl.BlockSpec((tq, pps * LANES), lambda b, i, p: (b * n_qt + i, p)),
                pl.BlockSpec((seq_len, KV_W), lambda b, i, p: (b, 0)),
                pl.BlockSpec((seq_len, KV_W), lambda b, i, p: (b, 0))]
    args = [q, k, v]
    if n_ctx:
        in_specs += [pl.BlockSpec((None, n_ctx, KV_W), lambda b, i, p: (b, 0, 0))] * 2
        args += [ck, cv]
    return pl.pallas_call(
        functools.partial(_attn_kernel, n_lat=seq_len, n_ctx=n_ctx, tq=tq, pps=pps, kchunk=kchunk),
        name="attn_ctx" if n_ctx else "attn",
        grid=(batch, n_qt, n_pp),
        in_specs=in_specs,
        out_specs=pl.BlockSpec((tq, pps * LANES), lambda b, i, p: (b * n_qt + i, p)),
        out_shape=jax.ShapeDtypeStruct((t, 1024), BF16),
        scratch_shapes=[pltpu.VMEM((2 * N_KV_HEADS, lk, LANES), BF16),
                        pltpu.VMEM((N_KV_HEADS, VT_ROWS, lk), BF16),
                        pltpu.VMEM((lk, tq), BF16), pltpu.VMEM((lk, tq), BF16)],
        compiler_params=_cparams(("arbitrary", "arbitrary", "arbitrary")),
    )(*args)


def _rec_kernel(*refs, seq_len, has_s0, emit_state, hps):
    refs = list(refs)
    rq_ref, gf_ref, gb_ref, ri_ref, rg_ref, gain_ref = refs[:6]
    pos = 6
    s0f_ref = s0b_ref = sf_ref = sb_ref = None
    if has_s0:
        s0f_ref, s0b_ref = refs[pos:pos + 2]
        pos += 2
    rec_ref = refs[pos]
    pos += 1
    if emit_state:
        sf_ref, sb_ref = refs[pos:pos + 2]
        pos += 2
    scratch = refs[pos:]
    for hh in range(hps):
        def cols(r):
            return r.at[:, pl.ds(hh * R_DK, R_DK)]

        def head(r):
            return None if r is None else r.at[hh]

        _rec_head(cols(rq_ref), cols(gf_ref), cols(gb_ref), cols(ri_ref), cols(rg_ref), gain_ref,
                  head(s0f_ref), head(s0b_ref), cols(rec_ref), head(sf_ref), head(sb_ref),
                  scratch, seq_len)


def _rec_head(rq_ref, gf_ref, gb_ref, ri_ref, rg_ref, gain_ref, s0f_ref, s0b_ref, rec_ref, sf_ref, sb_ref,
              scratch, seq_len):
    has_s0 = s0f_ref is not None
    emit_state = sf_ref is not None
    qd_scr, kdf_scr, kdb_scr, ke_scr, d_scr, u_scr, st_scr, a_scr, o_scr = scratch
    L = seq_len
    nc = L // CHUNK
    unroll = min(nc, 8)
    tpos =lax.broadcasted_iota(jnp.int32, (L, R_DK), 0) % CHUNK

    def chunk_cumsum(g, reverse):
        b = g
        for s in (1, 2, 4, 8, 16, 32):
            if reverse:
                sh = pltpu.roll(b, L - s, 0)
                b = b + jnp.where(tpos < CHUNK - s, sh, 0.0)
            else:
                sh = pltpu.roll(b, s, 0)
                b = b + jnp.where(tpos >= s, sh, 0.0)
        return b

    rq = rq_ref[...].astype(F32)
    for d, (g_ref, kd_scr) in enumerate(((gf_ref, kdf_scr), (gb_ref, kdb_scr))):
        g = g_ref[...]
        kk = 1.0 - jnp.exp(g)
        b = chunk_cumsum(g, reverse=(d == 1))
        tot = jnp.sum(g.reshape(nc, CHUNK, R_DK), axis=1, keepdims=True)
        tot = jnp.broadcast_to(tot, (nc, CHUNK, R_DK)).reshape(L, R_DK)
        qd_scr[:, d * R_DK:(d + 1) * R_DK] = (rq * jnp.exp(b)).astype(BF16)
        kd = kk * jnp.exp(-b)
        etot = jnp.exp(tot)
        kd_scr[...] = kd.astype(BF16)
        ke_scr[:, d * R_DK:(d + 1) * R_DK] = (kd * etot).astype(BF16)
        d_scr[:, d * R_DK:(d + 1) * R_DK] = etot

    def rows(c):
        return pl.ds(pl.multiple_of(c * CHUNK, CHUNK), CHUNK)

    ti = lax.broadcasted_iota(jnp.int32, (CHUNK, CHUNK), 0)
    si = lax.broadcasted_iota(jnp.int32, (CHUNK, CHUNK), 1)
    nt = (((1,), (1,)), ((), ()))
    tn = (((0,), (0,)), ((), ()))

    def ua_body(c, carry):
        u_scr[c] = lax.dot_general(ri_ref[rows(c), :], ke_scr[rows(c), :], tn, preferred_element_type=F32)
        qd = qd_scr[rows(c), :]
        af = lax.dot_general(qd[:, 0:R_DK], kdf_scr[rows(c), :], nt, preferred_element_type=F32)
        ab = lax.dot_general(qd[:, R_DK:2 * R_DK], kdb_scr[rows(c), :], nt, preferred_element_type=F32)
        a = jnp.where(si <= ti, af, 0.0) + jnp.where(si >= ti, ab, 0.0)
        a_scr[rows(c), :] = a.astype(BF16)
        return carry

    lax.fori_loop(0, nc, ua_body, 0, unroll=unroll)

    if has_s0:
        sf0 = s0f_ref[...].T
        sb0 = s0b_ref[...].T
    else:
        sf0 = jnp.zeros((R_DK, R_DK), F32)
        sb0 = jnp.zeros((R_DK, R_DK), F32)

    def scan_body(i, carry):
        sf, sb = carry
        cb = nc - 1 - i
        st_scr[i, :, 0:R_DK] = sf.astype(BF16)
        st_scr[cb, :, R_DK:2 * R_DK] = sb.astype(BF16)
        df = d_scr[pl.ds(pl.multiple_of(i * CHUNK, CHUNK), 1), 0:R_DK]
        db = d_scr[pl.ds(pl.multiple_of(cb * CHUNK, CHUNK), 1), R_DK:2 * R_DK]
        sf = df * sf + u_scr[i, :, 0:R_DK]
        sb = db * sb + u_scr[cb, :, R_DK:2 * R_DK]
        return sf, sb

    sf, sb = lax.fori_loop(0, nc, scan_body, (sf0, sb0))
    if emit_state:
        sf_ref[...] = sf.T
        sb_ref[...] = sb.T

    gain = gain_ref[...]

    def o_body(c, carry):
        o = jnp.dot(a_scr[rows(c), :], ri_ref[rows(c), :], preferred_element_type=F32)
        o = o + lax.dot_general(qd_scr[rows(c), :], st_scr[c], nt, preferred_element_type=F32)
        o_scr[rows(c), :] = o
        return carry

    lax.fori_loop(0, nc, o_body, 0, unroll=unroll)

    o = o_scr[...]
    ms = jnp.mean(o * o, axis=-1, keepdims=True)
    y = o * lax.rsqrt(ms + EPS) * gain * rg_ref[...].astype(F32)
    rec_ref[...] = y.astype(BF16)


def _recurrence(rq, g, ri, rg, gain, s0, batch, seq_len, emit_state, hps):
    t = rq.shape[0]
    nc = seq_len // CHUNK
    has_s0 = s0 is not None
    hb = R_HEADS // hps

    def col(off):
        return pl.BlockSpec((seq_len, hps * R_DK), lambda b, h: (b, h + off))

    def state(nblk, off):
        return pl.BlockSpec((hps, R_DK, R_DK), lambda b, h: (b * nblk + off + h, 0, 0))

    in_specs = [col(0), col(0), col(hb), col(0), col(0), _resident((1, R_DK))]
    args = [rq, g, g, ri, rg, gain]
    if has_s0:
        in_specs += [state(2 * hb, 0), state(2 * hb, hb)]
        args += [s0, s0]
    out_shape = [jax.ShapeDtypeStruct((t, 1024), BF16)]
    out_specs = [col(0)]
    if emit_state:
        st_shape = jax.ShapeDtypeStruct((batch * R_HEADS, R_DK, R_DK), F32)
        out_shape += [st_shape, st_shape]
        out_specs += [state(hb, 0)] * 2
    scratch = [pltpu.VMEM((seq_len, 2 * R_DK), BF16),
               pltpu.VMEM((seq_len, R_DK), BF16),
               pltpu.VMEM((seq_len, R_DK), BF16),
               pltpu.VMEM((seq_len, 2 * R_DK), BF16),
               pltpu.VMEM((seq_len, 2 * R_DK), F32),
               pltpu.VMEM((nc, R_DK, 2 * R_DK), F32),
               pltpu.VMEM((nc, R_DK, 2 * R_DK), BF16),
               pltpu.VMEM((seq_len, CHUNK), BF16),
               pltpu.VMEM((seq_len, R_DK), F32)]
    return pl.pallas_call(
        functools.partial(_rec_kernel, seq_len=seq_len, has_s0=has_s0, emit_state=emit_state, hps=hps),
        name="rec_s0" if has_s0 else "rec",
        grid=(batch, hb),
        in_specs=in_specs, out_specs=out_specs, out_shape=out_shape,
        scratch_shapes=scratch,
        compiler_params=_cparams(("arbitrary", "arbitrary")),
    )(*args)


def _merge_kernel(x_ref, attn_ref, rec_ref, gt_ref, wo_ref, mod_ref, n2_ref, x1_ref, h2_ref):
    ga = gt_ref[:, 0:1024].astype(F32)
    gr = gt_ref[:, 1024:2048].astype(F32)
    m = ga * attn_ref[...].astype(F32) + gr * rec_ref[...].astype(F32)
    out = jnp.dot(m.astype(BF16), wo_ref[...], preferred_element_type=F32)
    x1 = x_ref[...] + mod_ref[2:3, :] * out
    x1_ref[...] = x1
    ms = jnp.mean(x1 * x1, axis=-1, keepdims=True)
    h2 = x1 * lax.rsqrt(ms + EPS) * n2_ref[...]
    h2 = h2 * (1.0 + mod_ref[4:5, :]) + mod_ref[3:4, :]
    h2_ref[...] = h2.astype(BF16)


def _merge(x2d, attn, rec, gates, w_o, mods, mod_row, seq_len, n2, tm):
    t, d = x2d.shape
    tiles_per_seq = seq_len // tm

    def tok(w):
        return pl.BlockSpec((tm, w), lambda i: (i, 0))

    return pl.pallas_call(
        _merge_kernel,
        name="merge",
        grid=(t // tm,),
        in_specs=[tok(d), tok(d), tok(d), tok(2 * d), _resident((d, d)),
                  pl.BlockSpec((None, 6, d), lambda i: (mod_row(i // tiles_per_seq), 0, 0)),
                  _resident((1, d))],
        out_specs=[tok(d), tok(d)],
        out_shape=[jax.ShapeDtypeStruct((t, d), F32), jax.ShapeDtypeStruct((t, d), BF16)],
        compiler_params=_cparams(("arbitrary",)),
    )(x2d, attn, rec, gates, w_o, mods, n2)


def _ffn_kernel(hp_ref, h_ref, hn_ref, x1_ref, wup_ref, cw_ref, cb_ref, wdn_ref, mod_ref, fn_ref,
                y_ref, hbuf, act_scr, *, tiles_per_seq, ffc):
    tm = h_ref.shape[0]
    i = pl.program_id(0)
    ts = i % tiles_per_seq
    zero = jnp.zeros((HALO, D_MODEL), BF16)
    hbuf[0:HALO, :] = jnp.where(ts == 0, zero, hp_ref[...])
    hbuf[HALO:HALO + tm, :] = h_ref[...]
    hbuf[HALO + tm:HALO + tm + HALO, :] = jnp.where(ts == tiles_per_seq - 1, zero, hn_ref[...])
    hb = hbuf[...]
    mrows = tm + 2 * HALO

    def conv(col):
        u = jnp.dot(hb, wup_ref[:, col:col + ffc], preferred_element_type=F32)
        up = pltpu.roll(u, 1, 0)[HALO:HALO + tm, :]
        un = pltpu.roll(u, mrows - 1, 0)[HALO:HALO + tm, :]
        uc = u[HALO:HALO + tm, :]
        w = cw_ref[:, col:col + ffc]
        return up * w[0:1, :] + uc * w[1:2, :] + un * w[2:3, :] + cb_ref[:, col:col + ffc]

    for c0 in range(0, D_FF, ffc):
        a = conv(c0)
        b = conv(D_FF + c0)
        act_scr[:, c0:c0 + ffc] = (a * _sigmoid(a) * b).astype(BF16)

    f = jnp.dot(act_scr[...], wdn_ref[...], preferred_element_type=F32)
    x2 = x1_ref[...] + mod_ref[5:6, :] * f
    ms = jnp.mean(x2 * x2, axis=-1, keepdims=True)
    y_ref[...] = x2 * lax.rsqrt(ms + EPS) * fn_ref[...]


def _ffn(h2, x1, w_up, conv_w, conv_b, w_down, mods, mod_row, seq_len, fnorm, tm):
    t, d = x1.shape
    tiles_per_seq = seq_len // tm
    hb = tm // HALO
    n_hb = t // HALO

    return pl.pallas_call(
        functools.partial(_ffn_kernel, tiles_per_seq=tiles_per_seq, ffc=256),
        name="ffn",
        grid=(t // tm,),
        in_specs=[pl.BlockSpec((HALO, d), lambda i: (jnp.maximum(i * hb - 1, 0), 0)),
                  pl.BlockSpec((tm, d), lambda i: (i, 0)),
                  pl.BlockSpec((HALO, d), lambda i: (jnp.minimum((i + 1) * hb, n_hb - 1), 0)),
                  pl.BlockSpec((tm, d), lambda i: (i, 0)),
                  _resident((d, 2 * D_FF)),
                  _resident((3, 2 * D_FF)),
                  _resident((1, 2 * D_FF)),
                  _resident((D_FF, d)),
                  pl.BlockSpec((None, 6, d), lambda i: (mod_row(i // tiles_per_seq), 0, 0)),
                  _resident((1, d))],
        out_specs=pl.BlockSpec((tm, d), lambda i: (i, 0)),
        out_shape=jax.ShapeDtypeStruct((t, d), F32),
        scratch_shapes=[pltpu.VMEM((tm + 2 * HALO, d), BF16),
                        pltpu.VMEM((tm, D_FF), BF16)],
        compiler_params=_cparams(("arbitrary",)),
    )(h2, h2, h2, x1, w_up, conv_w, conv_b, w_down, mods, fnorm)


def _rope_tables(n_tokens):
    rows = n_tokens // GRID_W
    row = jnp.repeat(jnp.arange(rows, dtype=F32), GRID_W)
    colp = jnp.tile(jnp.arange(GRID_W, dtype=F32), rows)
    half = HEAD_DIM // 2
    inv_freq = 1.0 / (ROPE_THETA ** (jnp.arange(0, half, 2, dtype=F32) / half))
    ar = row[:, None] * inv_freq
    ac = colp[:, None] * inv_freq
    ang = jnp.concatenate([ar, ar, ac, ac], axis=-1)
    sign = jnp.asarray(np.tile(np.repeat(np.array([-1.0, 1.0], np.float32), 16), 2))
    cs = jnp.cos(ang)
    sn = jnp.sin(ang) * sign
    return jnp.tile(cs, (1, 2)), jnp.tile(sn, (1, 2))


def _group(x, mods, mod_row, w, ctx, rope_tabs, tq, pps, hps, tm_in, tm_ffn, emit_state):
    batch, seq_len, d = x.shape
    x2d = x.reshape(batch * seq_len, d)
    cs, sn = rope_tabs if rope_tabs is not None else (None, None)
    q, k, v, rq, g, ri, rg, gates = _inproj(
        x2d, mods, mod_row, seq_len, w["n1"], w["w_in"], w["qg"], w["kg"], w["lbl"], cs, sn, tm=tm_in)
    ck, cv, s0 = ctx
    attn = _attention(q, k, v, ck, cv, batch, seq_len, tq, pps)
    rec_out = _recurrence(rq, g, ri, rg, w["rgain"], s0, batch, seq_len, emit_state, hps)
    rec = rec_out[0]
    x1, h2 = _merge(x2d, attn, rec, gates, w["w_o"], mods, mod_row, seq_len, w["n2"], tm=256)
    y = _ffn(h2, x1, w["w_up"], w["conv_w"], w["conv_b"], w["w_down"], mods, mod_row, seq_len,
             w["fnorm"], tm=tm_ffn)
    return y.reshape(batch, seq_len, d), k, v, rec_out[1:]


def kernel(x_prompt, x_sample, c, cache_k, cache_v, state_hgrn, c_ctx, ada_w, ada_b, norm1, norm2,
           w_in, q_norm, k_norm, hgrn_lb_logits, hgrn_norm, w_o, w_up, conv_w, conv_b, w_down, final_norm):
    bp, lp, d = x_prompt.shape
    bs, ls, _ = x_sample.shape
    n_ctx = cache_k.shape[2]

    cond = jnp.zeros((16, d), F32).at[0].set(c_ctx).at[1:1 + bs].set(c)
    mods = _mods(cond, ada_w[0], ada_b[0]).reshape(16, 6, d)

    w = dict(
        n1=norm1[0].reshape(1, d), n2=norm2[0].reshape(1, d), fnorm=final_norm.reshape(1, d),
        w_in=w_in[0].astype(BF16), w_o=w_o[0].astype(BF16),
        w_up=w_up[0].astype(BF16), w_down=w_down[0].astype(BF16),
        conv_w=conv_w[0], conv_b=conv_b[0].reshape(1, 2 * D_FF),
        qg=jnp.tile(q_norm[0], N_HEADS).reshape(1, 1024),
        kg=jnp.tile(k_norm[0], N_KV_HEADS).reshape(1, KV_W),
        lbl=hgrn_lb_logits.reshape(4, 1024),
        rgain=hgrn_norm[0].reshape(1, R_DK),
    )

    y_p, k_p, v_p, st = _group(x_prompt, mods, lambda b: 0, w, (None, None, None), None,
                               tq=lp, pps=8, hps=R_HEADS, tm_in=lp, tm_ffn=lp, emit_state=True)
    ctx = (cache_k[:, 0].reshape(bs, n_ctx, KV_W), cache_v[:, 0].reshape(bs, n_ctx, KV_W),
           state_hgrn[:, 0].reshape(bs * 2 * R_HEADS, R_DK, R_DK))
    y_s, _, _, _ = _group(x_sample, mods, lambda b: b + 1, w, ctx, _rope_tables(ls),
                          tq=512, pps=2, hps=1, tm_in=512, tm_ffn=1024, emit_state=False)

    sf, sb = st
    new_state = jnp.stack([sf.reshape(bp, R_HEADS, R_DK, R_DK),
                           sb.reshape(bp, R_HEADS, R_DK, R_DK)], axis=1)[:, None]
    new_k = k_p.reshape(bp, 1, lp, N_KV_HEADS, HEAD_DIM)
    new_v = v_p.reshape(bp, 1, lp, N_KV_HEADS, HEAD_DIM)
    return (y_p, y_s, new_k, new_v, new_state)
```

```python
import functools

import jax
import jax.numpy as jnp
import numpy as np
from jax import lax
from jax.experimental import pallas as pl
from jax.experimental.pallas import tpu as pltpu

F32 = jnp.float32
BF16 = jnp.bfloat16

D_MODEL = 1024
GRID_W = 64
HEAD_DIM = 64
N_HEADS = 16
N_KV_HEADS = 4
KV_W = N_KV_HEADS * HEAD_DIM
ROPE_THETA = 10000.0
R_DK = 128
R_HEADS = 8
CHUNK = 64
D_FF = 2816
EPS = 1e-6
LOG2E = 1.4426950408889634
IN_W = 8704

LANES = 128
HALO = 16
VT_ROWS = HEAD_DIM + 16
SM_ROWS = 32
SHIFT_SLACK = 100.0
VMEM_LIMIT = 56 * 1024 * 1024

_Q0, _K0, _V0, _RQ0, _RFF0, _RFB0, _RI0, _RG0, _ZA0, _ZR0 = (
    0, 1024, 1280, 1536, 2560, 3584, 4608, 5632, 6656, 7680)


def _sigmoid(x):
    return 1.0 / (1.0 + jnp.exp(-x))


def _cparams(sem):
    return pltpu.CompilerParams(dimension_semantics=sem, vmem_limit_bytes=VMEM_LIMIT)


def _resident(shape):
    nd = len(shape)
    return pl.BlockSpec(shape, lambda *_: (0,) * nd, pipeline_mode=pl.Buffered(1))


def _mod_kernel(c_ref, w_ref, b_ref, o_ref):
    c = c_ref[...]
    s = c * _sigmoid(c)
    o_ref[...] = jnp.dot(s, w_ref[...], preferred_element_type=F32,
                         precision=lax.Precision.HIGHEST) + b_ref[...]


def _mods(cond, ada_w, ada_b):
    rows, d = cond.shape
    n = ada_w.shape[1]
    tn = 1536
    return pl.pallas_call(
        _mod_kernel,
        name="mods",
        grid=(n // tn,),
        in_specs=[pl.BlockSpec((rows, d), lambda j: (0, 0)),
                  pl.BlockSpec((d, tn), lambda j: (0, j)),
                  pl.BlockSpec((1, tn), lambda j: (0, j))],
        out_specs=pl.BlockSpec((rows, tn), lambda j: (0, j)),
        out_shape=jax.ShapeDtypeStruct((rows, n), F32),
        compiler_params=_cparams(("arbitrary",)),
    )(cond, ada_w, ada_b.reshape(1, n))


def _inproj_kernel(*refs, rope):
    if rope:
        (x_ref, mod_ref, n1_ref, w_ref, qg_ref, kg_ref, lbl_ref, cs_ref, sn_ref,
         q_ref, k_ref, v_ref, rq_ref, g_ref, ri_ref, rg_ref, gt_ref) = refs
    else:
        (x_ref, mod_ref, n1_ref, w_ref, qg_ref, kg_ref, lbl_ref,
         q_ref, k_ref, v_ref, rq_ref, g_ref, ri_ref, rg_ref, gt_ref) = refs
        cs_ref = sn_ref = None
    tm = x_ref.shape[0]

    x = x_ref[...]
    ms = jnp.mean(x * x, axis=-1, keepdims=True)
    h = x * lax.rsqrt(ms + EPS) * n1_ref[...]
    h = h * (1.0 + mod_ref[1:2, :]) + mod_ref[0:1, :]
    hb = h.astype(BF16)

    def proj(a, b):
        return jnp.dot(hb, w_ref[:, a:b], preferred_element_type=F32)

    gr = lax.broadcasted_iota(jnp.int32, (LANES, LANES), 0) // HEAD_DIM
    gc = lax.broadcasted_iota(jnp.int32, (LANES, LANES), 1) // HEAD_DIM
    gmat = jnp.where(gr == gc, 1.0 / HEAD_DIM, 0.0).astype(BF16)

    def headnorm(a, gain):
        ss = jnp.dot((a * a).astype(BF16), gmat, preferred_element_type=F32)
        return a * lax.rsqrt(ss + EPS) * gain

    if rope:
        lane = lax.broadcasted_iota(jnp.int32, (tm, LANES), 1)
        first = (lane % 32) < 16
        cs = cs_ref[...]
        sn = sn_ref[...]

        def rot(a):
            sw = jnp.where(first, pltpu.roll(a, LANES - 16, 1), pltpu.roll(a, 16, 1))
            return a * cs + sw * sn
    else:
        def rot(a):
            return a

    scale = HEAD_DIM ** -0.5 * LOG2E
    for c0 in range(0, 1024, 512):
        acc = proj(_Q0 + c0, _Q0 + c0 + 512)
        for s in range(4):
            col = c0 + s * LANES
            a = headnorm(acc[:, s * LANES:(s + 1) * LANES], qg_ref[:, col:col + LANES] * scale)
            q_ref[:, col:col + LANES] = rot(a).astype(BF16)

    acc = proj(_K0, _K0 + 512)
    for s in range(2):
        col = s * LANES
        a = headnorm(acc[:, col:col + LANES], kg_ref[:, col:col + LANES])
        k_ref[:, col:col + LANES] = rot(a)
    v_ref[...] = acc[:, 256:512]

    rscale = R_DK ** -0.5
    for c0 in range(0, 1024, 512):
        acc = proj(_RQ0 + c0, _RQ0 + c0 + 512)
        rq_ref[:, c0:c0 + 512] = (acc * _sigmoid(acc) * rscale).astype(BF16)

    l = lbl_ref[...]
    for d in range(2):
        a0 = l[2 * d:2 * d + 1, :]
        a1 = l[2 * d + 1:2 * d + 2, :]
        mx = jnp.maximum(a0, a1)
        e0 = jnp.exp(a0 - mx)
        e1 = jnp.exp(a1 - mx)
        lb = e0 / (e0 + e1)
        for c0 in range(0, 1024, 512):
            acc = proj(_RFF0 + d * 1024 + c0, _RFF0 + d * 1024 + c0 + 512)
            lbc = lb[:, c0:c0 + 512]
            f = lbc + (1.0 - lbc) * _sigmoid(acc)
            g_ref[:, d * 1024 + c0:d * 1024 + c0 + 512] = jnp.log(f)

    for c0 in range(0, 1024, 512):
        ri_ref[:, c0:c0 + 512] = proj(_RI0 + c0, _RI0 + c0 + 512).astype(BF16)
    for c0 in range(0, 1024, 512):
        acc = proj(_RG0 + c0, _RG0 + c0 + 512)
        rg_ref[:, c0:c0 + 512] = (acc * _sigmoid(acc)).astype(BF16)
    for c0 in range(0, 2048, 512):
        acc = proj(_ZA0 + c0, _ZA0 + c0 + 512)
        gt_ref[:, c0:c0 + 512] = _sigmoid(acc).astype(BF16)


def _inproj(x2d, mods, mod_row, seq_len, n1, w_in, qg, kg, lbl, cs, sn, tm):
    t, d = x2d.shape
    tiles_per_seq = seq_len // tm
    rope = cs is not None

    def tok(w):
        return pl.BlockSpec((tm, w), lambda i: (i, 0))

    in_specs = [tok(d),
                pl.BlockSpec((None, 6, d), lambda i: (mod_row(i // tiles_per_seq), 0, 0)),
                _resident((1, d)),
                _resident((d, IN_W)),
                _resident((1, 1024)),
                _resident((1, KV_W)),
                _resident((4, 1024))]
    args = [x2d, mods, n1, w_in, qg, kg, lbl]
    if rope:
        in_specs += [pl.BlockSpec((tm, LANES), lambda i: (i % tiles_per_seq, 0))] * 2
        args += [cs, sn]
    out_shape = [jax.ShapeDtypeStruct((t, 1024), BF16),
                 jax.ShapeDtypeStruct((t, KV_W), F32),
                 jax.ShapeDtypeStruct((t, KV_W), F32),
                 jax.ShapeDtypeStruct((t, 1024), BF16),
                 jax.ShapeDtypeStruct((t, 2048), F32),
                 jax.ShapeDtypeStruct((t, 1024), BF16),
                 jax.ShapeDtypeStruct((t, 1024), BF16),
                 jax.ShapeDtypeStruct((t, 2048), BF16)]
    out_specs = [tok(s.shape[1]) for s in out_shape]
    return pl.pallas_call(
        functools.partial(_inproj_kernel, rope=rope),
        name="inproj_rope" if rope else "inproj",
        grid=(t // tm,),
        in_specs=in_specs, out_specs=out_specs, out_shape=out_shape,
        compiler_params=_cparams(("arbitrary",)),
    )(*args)


def _attn_kernel(*refs, n_lat, n_ctx, tq, pps, kchunk):
    if n_ctx:
        q_ref, k_ref, v_ref, ck_ref, cv_ref, o_ref, kpad, vt, p0_scr, p1_scr, ref_scr = refs
    else:
        q_ref, k_ref, v_ref, o_ref, kpad, vt, p0_scr, p1_scr, ref_scr = refs
        ck_ref = cv_ref = None
    lk = n_lat + n_ctx
    first_step = jnp.logical_and(pl.program_id(1) == 0, pl.program_id(2) == 0)

    @pl.when(first_step)
    def _build():
        def place(dst, rows, ks):
            lane = lax.broadcasted_iota(jnp.int32, ks.shape, 1)
            lo = jnp.where(lane < HEAD_DIM, ks, 0.0)
            hi = jnp.where(lane >= HEAD_DIM, ks, 0.0)
            kpad[dst + 0, rows, :] = lo.astype(BF16)
            kpad[dst + 1, rows, :] = pltpu.roll(lo, HEAD_DIM, 1).astype(BF16)
            kpad[dst + 2, rows, :] = pltpu.roll(hi, HEAD_DIM, 1).astype(BF16)
            kpad[dst + 3, rows, :] = hi.astype(BF16)

        def place_v(s, cols_out, vs):
            vtr = vs.T.astype(BF16)
            vt[2 * s, 0:HEAD_DIM, cols_out] = vtr[0:HEAD_DIM, :]
            vt[2 * s + 1, 0:HEAD_DIM, cols_out] = vtr[HEAD_DIM:2 * HEAD_DIM, :]

        ones_rows = jnp.where(lax.broadcasted_iota(jnp.int32, (VT_ROWS - HEAD_DIM, lk), 0) == 0,
                              1.0, 0.0).astype(BF16)
        for kvh in range(N_KV_HEADS):
            vt[kvh, HEAD_DIM:VT_ROWS, :] = ones_rows
        for s in range(2):
            cols = slice(s * LANES, (s + 1) * LANES)
            place(4 * s, slice(0, n_lat), k_ref[:, cols])
            place_v(s, slice(0, n_lat), v_ref[:, cols])
            if n_ctx:
                place(4 * s, slice(n_lat, lk), ck_ref[:, cols])
                place_v(s, slice(n_lat, lk), cv_ref[:, cols])

    n_chunks = lk // kchunk
    p_bufs = (p0_scr, p1_scr)
    pair0 = pl.program_id(2) * pps
    nt = (((1,), (1,)), ((), ()))
    QH = tq
    n_qh = tq // QH
    units = [(j, parity, qh) for j in range(pps) for qh in range(n_qh) for parity in range(2)]

    def unit_operands(u):
        j, parity, qh = units[u]
        kv = (pair0 + j) // 2
        q2 = q_ref[qh * QH:(qh + 1) * QH, j * LANES:(j + 1) * LANES]
        return 2 * kv + parity, kv, q2

    def chunk_scores(kidx, q2, c):
        s = lax.dot_general(kpad[kidx, c * kchunk:(c + 1) * kchunk, :], q2, nt, preferred_element_type=F32)
        return s, jnp.max(s.reshape(kchunk // 8, 8, QH), axis=0)

    def bcast_max(x8):
        return jnp.broadcast_to(jnp.max(x8, axis=0, keepdims=True), (8, QH))

    def exact_max():
        for u in range(len(units)):
            kidx, _, q2 = unit_operands(u)
            mx = chunk_scores(kidx, q2, 0)[1]
            for c in range(1, n_chunks):
                mx = jnp.maximum(mx, chunk_scores(kidx, q2, c)[1])
            ref_scr[u] = bcast_max(mx)

    def online_pass():
        worst = jnp.zeros((8, QH), F32)
        outs = []
        for u in range(len(units)):
            j, parity, qh = units[u]
            kidx, kv, q2 = unit_operands(u)
            p_scr = p_bufs[u % 2]
            r8 = ref_scr[u]
            s = lax.dot_general(kpad[kidx], q2, nt, preferred_element_type=F32)
            mtot = jnp.full((8, QH), -jnp.inf, F32)
            for r0 in range(0, lk, SM_ROWS):
                sb = s[r0:r0 + SM_ROWS, :].reshape(SM_ROWS // 8, 8, QH)
                mtot = jnp.maximum(mtot, jnp.max(sb, axis=0))
                p_scr[r0:r0 + SM_ROWS, :] = jnp.exp2(sb - r8).reshape(SM_ROWS, QH).astype(BF16)
            worst = jnp.maximum(worst, jnp.abs(bcast_max(mtot) - r8))
            ot = jnp.dot(vt[kv], p_scr[...], preferred_element_type=F32)
            outs.append(ot[0:HEAD_DIM, :] * (1.0 / ot[HEAD_DIM:HEAD_DIM + 1, :]))
            if parity == 1:
                o2t = jnp.concatenate(outs[-2:], axis=0)
                o_ref[qh * QH:(qh + 1) * QH, j * LANES:(j + 1) * LANES] = o2t.T.astype(BF16)
        return worst

    def attempt(state):
        n, _ = state

        @pl.when(n == 0)
        def _():
            ref_scr[...] = jnp.zeros(ref_scr.shape, F32)

        @pl.when(n == 1)
        def _():
            exact_max()

        worst = online_pass()
        return n + 1, (jnp.max(worst) > SHIFT_SLACK).astype(jnp.int32)

    lax.while_loop(lambda st: jnp.logical_or(st[0] == 0, jnp.logical_and(st[0] == 1, st[1] == 1)),
                   attempt, (jnp.int32(0), jnp.int32(0)))


def _attention(q, k, v, ck, cv, batch, seq_len, tq, pps):
    t = q.shape[0]
    n_ctx = 0 if ck is None else ck.shape[1]
    lk = seq_len + n_ctx
    n_qt = seq_len // tq
    n_pp = (N_HEADS // 2) // pps
    kchunk = min(lk, 512)
    in_specs = [pl.BlockSpec((tq, pps * LANES), lambda b, i, p: (b * n_qt + i, p)),
                pl.BlockSpec((seq_len, KV_W), lambda b, i, p: (b, 0)),
                pl.BlockSpec((seq_len, KV_W), lambda b, i, p: (b, 0))]
    args = [q, k, v]
    if n_ctx:
        in_specs += [pl.BlockSpec((None, n_ctx, KV_W), lambda b, i, p: (b, 0, 0))] * 2
        args += [ck, cv]
    return pl.pallas_call(
        functools.partial(_attn_kernel, n_lat=seq_len, n_ctx=n_ctx, tq=tq, pps=pps, kchunk=kchunk),
        name="attn_ctx" if n_ctx else "attn",
        grid=(batch, n_qt, n_pp),
        in_specs=in_specs,
        out_specs=pl.BlockSpec((tq, pps * LANES), lambda b, i, p: (b * n_qt + i, p)),
        out_shape=jax.ShapeDtypeStruct((t, 1024), BF16),
        scratch_shapes=[pltpu.VMEM((2 * N_KV_HEADS, lk, LANES), BF16),
                        pltpu.VMEM((N_KV_HEADS, VT_ROWS, lk), BF16),
                        pltpu.VMEM((lk, tq), BF16), pltpu.VMEM((lk, tq), BF16),
                        pltpu.VMEM((2 * pps, 8, tq), F32)],
        compiler_params=_cparams(("arbitrary", "arbitrary", "arbitrary")),
    )(*args)


def _rec_kernel(*refs, seq_len, has_s0, emit_state, hps):
    refs = list(refs)
    rq_ref, gf_ref, gb_ref, ri_ref, rg_ref, gain_ref = refs[:6]
    pos = 6
    s0f_ref = s0b_ref = sf_ref = sb_ref = None
    if has_s0:
        s0f_ref, s0b_ref = refs[pos:pos + 2]
        pos += 2
    rec_ref = refs[pos]
    pos += 1
    if emit_state:
        sf_ref, sb_ref = refs[pos:pos + 2]
        pos += 2
    scratch = refs[pos:]
    for hh in range(hps):
        def cols(r):
            return r.at[:, pl.ds(hh * R_DK, R_DK)]

        def head(r):
            return None if r is None else r.at[hh]

        _rec_head(cols(rq_ref), cols(gf_ref), cols(gb_ref), cols(ri_ref), cols(rg_ref), gain_ref,
                  head(s0f_ref), head(s0b_ref), cols(rec_ref), head(sf_ref), head(sb_ref),
                  scratch, seq_len)


def _rec_head(rq_ref, gf_ref, gb_ref, ri_ref, rg_ref, gain_ref, s0f_ref, s0b_ref, rec_ref, sf_ref, sb_ref,
              scratch, seq_len):
    has_s0 = s0f_ref is not None
    emit_state = sf_ref is not None
    qd_scr, kdf_scr, kdb_scr, ke_scr, d_scr, u_scr, st_scr, a_scr, o_scr = scratch
    L = seq_len
    nc = L // CHUNK
    unroll = min(nc, 8)
    tpos =lax.broadcasted_iota(jnp.int32, (L, R_DK), 0) % CHUNK

    def chunk_cumsum(g, reverse):
        b = g
        for s in (1, 2, 4, 8, 16, 32):
            if reverse:
                sh = pltpu.roll(b, L - s, 0)
                b = b + jnp.where(tpos < CHUNK - s, sh, 0.0)
            else:
                sh = pltpu.roll(b, s, 0)
                b = b + jnp.where(tpos >= s, sh, 0.0)
        return b

    rq = rq_ref[...].astype(F32)
    for d, (g_ref, kd_scr) in enumerate(((gf_ref, kdf_scr), (gb_ref, kdb_scr))):
        g = g_ref[...]
        kk = 1.0 - jnp.exp(g)
        b = chunk_cumsum(g, reverse=(d == 1))
        tot = jnp.sum(g.reshape(nc, CHUNK, R_DK), axis=1, keepdims=True)
        tot = jnp.broadcast_to(tot, (nc, CHUNK, R_DK)).reshape(L, R_DK)
        qd_scr[:, d * R_DK:(d + 1) * R_DK] = (rq * jnp.exp(b)).astype(BF16)
        kd = kk * jnp.exp(-b)
        etot = jnp.exp(tot)
        kd_scr[...] = kd.astype(BF16)
        ke_scr[:, d * R_DK:(d + 1) * R_DK] = (kd * etot).astype(BF16)
        d_scr[:, d * R_DK:(d + 1) * R_DK] = etot

    def rows(c):
        return pl.ds(pl.multiple_of(c * CHUNK, CHUNK), CHUNK)

    ti = lax.broadcasted_iota(jnp.int32, (CHUNK, CHUNK), 0)
    si = lax.broadcasted_iota(jnp.int32, (CHUNK, CHUNK), 1)
    nt = (((1,), (1,)), ((), ()))
    tn = (((0,), (0,)), ((), ()))

    def ua_body(c, carry):
        u_scr[c] = lax.dot_general(ri_ref[rows(c), :], ke_scr[rows(c), :], tn, preferred_element_type=F32)
        qd = qd_scr[rows(c), :]
        af = lax.dot_general(qd[:, 0:R_DK], kdf_scr[rows(c), :], nt, preferred_element_type=F32)
        ab = lax.dot_general(qd[:, R_DK:2 * R_DK], kdb_scr[rows(c), :], nt, preferred_element_type=F32)
        a = jnp.where(si <= ti, af, 0.0) + jnp.where(si >= ti, ab, 0.0)
        a_scr[rows(c), :] = a.astype(BF16)
        return carry

    lax.fori_loop(0, nc, ua_body, 0, unroll=unroll)

    if has_s0:
        sf0 = s0f_ref[...].T
        sb0 = s0b_ref[...].T
    else:
        sf0 = jnp.zeros((R_DK, R_DK), F32)
        sb0 = jnp.zeros((R_DK, R_DK), F32)

    def scan_body(i, carry):
        sf, sb = carry
        cb = nc - 1 - i
        st_scr[i, :, 0:R_DK] = sf.astype(BF16)
        st_scr[cb, :, R_DK:2 * R_DK] = sb.astype(BF16)
        df = d_scr[pl.ds(pl.multiple_of(i * CHUNK, CHUNK), 1), 0:R_DK]
        db = d_scr[pl.ds(pl.multiple_of(cb * CHUNK, CHUNK), 1), R_DK:2 * R_DK]
        sf = df * sf + u_scr[i, :, 0:R_DK]
        sb = db * sb + u_scr[cb, :, R_DK:2 * R_DK]
        return sf, sb

    sf, sb = lax.fori_loop(0, nc, scan_body, (sf0, sb0))
    if emit_state:
        sf_ref[...] = sf.T
        sb_ref[...] = sb.T

    gain = gain_ref[...]

    def o_body(c, carry):
        o = jnp.dot(a_scr[rows(c), :], ri_ref[rows(c), :], preferred_element_type=F32)
        o = o + lax.dot_general(qd_scr[rows(c), :], st_scr[c], nt, preferred_element_type=F32)
        o_scr[rows(c), :] = o
        return carry

    lax.fori_loop(0, nc, o_body, 0, unroll=unroll)

    o = o_scr[...]
    ms = jnp.mean(o * o, axis=-1, keepdims=True)
    y = o * lax.rsqrt(ms + EPS) * gain * rg_ref[...].astype(F32)
    rec_ref[...] = y.astype(BF16)


def _recurrence(rq, g, ri, rg, gain, s0, batch, seq_len, emit_state, hps):
    t = rq.shape[0]
    nc = seq_len // CHUNK
    has_s0 = s0 is not None
    hb = R_HEADS // hps

    def col(off):
        return pl.BlockSpec((seq_len, hps * R_DK), lambda b, h: (b, h + off))

    def state(nblk, off):
        return pl.BlockSpec((hps, R_DK, R_DK), lambda b, h: (b * nblk + off + h, 0, 0))

    in_specs = [col(0), col(0), col(hb), col(0), col(0), _resident((1, R_DK))]
    args = [rq, g, g, ri, rg, gain]
    if has_s0:
        in_specs += [state(2 * hb, 0), state(2 * hb, hb)]
        args += [s0, s0]
    out_shape = [jax.ShapeDtypeStruct((t, 1024), BF16)]
    out_specs = [col(0)]
    if emit_state:
        st_shape = jax.ShapeDtypeStruct((batch * R_HEADS, R_DK, R_DK), F32)
        out_shape += [st_shape, st_shape]
        out_specs += [state(hb, 0)] * 2
    scratch = [pltpu.VMEM((seq_len, 2 * R_DK), BF16),
               pltpu.VMEM((seq_len, R_DK), BF16),
               pltpu.VMEM((seq_len, R_DK), BF16),
               pltpu.VMEM((seq_len, 2 * R_DK), BF16),
               pltpu.VMEM((seq_len, 2 * R_DK), F32),
               pltpu.VMEM((nc, R_DK, 2 * R_DK), F32),
               pltpu.VMEM((nc, R_DK, 2 * R_DK), BF16),
               pltpu.VMEM((seq_len, CHUNK), BF16),
               pltpu.VMEM((seq_len, R_DK), F32)]
    return pl.pallas_call(
        functools.partial(_rec_kernel, seq_len=seq_len, has_s0=has_s0, emit_state=emit_state, hps=hps),
        name="rec_s0" if has_s0 else "rec",
        grid=(batch, hb),
        in_specs=in_specs, out_specs=out_specs, out_shape=out_shape,
        scratch_shapes=scratch,
        compiler_params=_cparams(("arbitrary", "arbitrary")),
    )(*args)


def _merge_kernel(x_ref, attn_ref, rec_ref, gt_ref, wo_ref, mod_ref, n2_ref, x1_ref, h2_ref):
    ga = gt_ref[:, 0:1024].astype(F32)
    gr = gt_ref[:, 1024:2048].astype(F32)
    m = ga * attn_ref[...].astype(F32) + gr * rec_ref[...].astype(F32)
    out = jnp.dot(m.astype(BF16), wo_ref[...], preferred_element_type=F32)
    x1 = x_ref[...] + mod_ref[2:3, :] * out
    x1_ref[...] = x1
    ms = jnp.mean(x1 * x1, axis=-1, keepdims=True)
    h2 = x1 * lax.rsqrt(ms + EPS) * n2_ref[...]
    h2 = h2 * (1.0 + mod_ref[4:5, :]) + mod_ref[3:4, :]
    h2_ref[...] = h2.astype(BF16)


def _merge(x2d, attn, rec, gates, w_o, mods, mod_row, seq_len, n2, tm):
    t, d = x2d.shape
    tiles_per_seq = seq_len // tm

    def tok(w):
        return pl.BlockSpec((tm, w), lambda i: (i, 0))

    return pl.pallas_call(
        _merge_kernel,
        name="merge",
        grid=(t // tm,),
        in_specs=[tok(d), tok(d), tok(d), tok(2 * d), _resident((d, d)),
                  pl.BlockSpec((None, 6, d), lambda i: (mod_row(i // tiles_per_seq), 0, 0)),
                  _resident((1, d))],
        out_specs=[tok(d), tok(d)],
        out_shape=[jax.ShapeDtypeStruct((t, d), F32), jax.ShapeDtypeStruct((t, d), BF16)],
        compiler_params=_cparams(("arbitrary",)),
    )(x2d, attn, rec, gates, w_o, mods, n2)


def _ffn_kernel(hp_ref, h_ref, hn_ref, x1_ref, wup_ref, cw_ref, cb_ref, wdn_ref, mod_ref, fn_ref,
                y_ref, hbuf, act_scr, *, tiles_per_seq, ffc):
    tm = h_ref.shape[0]
    i = pl.program_id(0)
    ts = i % tiles_per_seq
    zero = jnp.zeros((HALO, D_MODEL), BF16)
    hbuf[0:HALO, :] = jnp.where(ts == 0, zero, hp_ref[...])
    hbuf[HALO:HALO + tm, :] = h_ref[...]
    hbuf[HALO + tm:HALO + tm + HALO, :] = jnp.where(ts == tiles_per_seq - 1, zero, hn_ref[...])
    hb = hbuf[...]
    mrows = tm + 2 * HALO

    def conv(col):
        u = jnp.dot(hb, wup_ref[:, col:col + ffc], preferred_element_type=F32)
        up = pltpu.roll(u, 1, 0)[HALO:HALO + tm, :]
        un = pltpu.roll(u, mrows - 1, 0)[HALO:HALO + tm, :]
        uc = u[HALO:HALO + tm, :]
        w = cw_ref[:, col:col + ffc]
        return up * w[0:1, :] + uc * w[1:2, :] + un * w[2:3, :] + cb_ref[:, col:col + ffc]

    for c0 in range(0, D_FF, ffc):
        a = conv(c0)
        b = conv(D_FF + c0)
        act_scr[:, c0:c0 + ffc] = (a * _sigmoid(a) * b).astype(BF16)

    f = jnp.dot(act_scr[...], wdn_ref[...], preferred_element_type=F32)
    x2 = x1_ref[...] + mod_ref[5:6, :] * f
    ms = jnp.mean(x2 * x2, axis=-1, keepdims=True)
    y_ref[...] = x2 * lax.rsqrt(ms + EPS) * fn_ref[...]


def _ffn(h2, x1, w_up, conv_w, conv_b, w_down, mods, mod_row, seq_len, fnorm, tm):
    t, d = x1.shape
    tiles_per_seq = seq_len // tm
    hb = tm // HALO
    n_hb = t // HALO

    return pl.pallas_call(
        functools.partial(_ffn_kernel, tiles_per_seq=tiles_per_seq, ffc=256),
        name="ffn",
        grid=(t // tm,),
        in_specs=[pl.BlockSpec((HALO, d), lambda i: (jnp.maximum(i * hb - 1, 0), 0)),
                  pl.BlockSpec((tm, d), lambda i: (i, 0)),
                  pl.BlockSpec((HALO, d), lambda i: (jnp.minimum((i + 1) * hb, n_hb - 1), 0)),
                  pl.BlockSpec((tm, d), lambda i: (i, 0)),
                  _resident((d, 2 * D_FF)),
                  _resident((3, 2 * D_FF)),
                  _resident((1, 2 * D_FF)),
                  _resident((D_FF, d)),
                  pl.BlockSpec((None, 6, d), lambda i: (mod_row(i // tiles_per_seq), 0, 0)),
                  _resident((1, d))],
        out_specs=pl.BlockSpec((tm, d), lambda i: (i, 0)),
        out_shape=jax.ShapeDtypeStruct((t, d), F32),
        scratch_shapes=[pltpu.VMEM((tm + 2 * HALO, d), BF16),
                        pltpu.VMEM((tm, D_FF), BF16)],
        compiler_params=_cparams(("arbitrary",)),
    )(h2, h2, h2, x1, w_up, conv_w, conv_b, w_down, mods, fnorm)


def _rope_tables(n_tokens):
    rows = n_tokens // GRID_W
    row = jnp.repeat(jnp.arange(rows, dtype=F32), GRID_W)
    colp = jnp.tile(jnp.arange(GRID_W, dtype=F32), rows)
    half = HEAD_DIM // 2
    inv_freq = 1.0 / (ROPE_THETA ** (jnp.arange(0, half, 2, dtype=F32) / half))
    ar = row[:, None] * inv_freq
    ac = colp[:, None] * inv_freq
    ang = jnp.concatenate([ar, ar, ac, ac], axis=-1)
    sign = jnp.asarray(np.tile(np.repeat(np.array([-1.0, 1.0], np.float32), 16), 2))
    cs = jnp.cos(ang)
    sn = jnp.sin(ang) * sign
    return jnp.tile(cs, (1, 2)), jnp.tile(sn, (1, 2))


def _group(x, mods, mod_row, w, ctx, rope_tabs, tq, pps, hps, tm_in, tm_ffn, emit_state):
    batch, seq_len, d = x.shape
    x2d = x.reshape(batch * seq_len, d)
    cs, sn = rope_tabs if rope_tabs is not None else (None, None)
    q, k, v, rq, g, ri, rg, gates = _inproj(
        x2d, mods, mod_row, seq_len, w["n1"], w["w_in"], w["qg"], w["kg"], w["lbl"], cs, sn, tm=tm_in)
    ck, cv, s0 = ctx
    attn = _attention(q, k, v, ck, cv, batch, seq_len, tq, pps)
    rec_out = _recurrence(rq, g, ri, rg, w["rgain"], s0, batch, seq_len, emit_state, hps)
    rec = rec_out[0]
    x1, h2 = _merge(x2d, attn, rec, gates, w["w_o"], mods, mod_row, seq_len, w["n2"], tm=256)
    y = _ffn(h2, x1, w["w_up"], w["conv_w"], w["conv_b"], w["w_down"], mods, mod_row, seq_len,
             w["fnorm"], tm=tm_ffn)
    return y.reshape(batch, seq_len, d), k, v, rec_out[1:]


def kernel(x_prompt, x_sample, c, cache_k, cache_v, state_hgrn, c_ctx, ada_w, ada_b, norm1, norm2,
           w_in, q_norm, k_norm, hgrn_lb_logits, hgrn_norm, w_o, w_up, conv_w, conv_b, w_down, final_norm):
    bp, lp, d = x_prompt.shape
    bs, ls, _ = x_sample.shape
    n_ctx = cache_k.shape[2]

    cond = jnp.zeros((16, d), F32).at[0].set(c_ctx).at[1:1 + bs].set(c)
    mods = _mods(cond, ada_w[0], ada_b[0]).reshape(16, 6, d)

    w = dict(
        n1=norm1[0].reshape(1, d), n2=norm2[0].reshape(1, d), fnorm=final_norm.reshape(1, d),
        w_in=w_in[0].astype(BF16), w_o=w_o[0].astype(BF16),
        w_up=w_up[0].astype(BF16), w_down=w_down[0].astype(BF16),
        conv_w=conv_w[0], conv_b=conv_b[0].reshape(1, 2 * D_FF),
        qg=jnp.tile(q_norm[0], N_HEADS).reshape(1, 1024),
        kg=jnp.tile(k_norm[0], N_KV_HEADS).reshape(1, KV_W),
        lbl=hgrn_lb_logits.reshape(4, 1024),
        rgain=hgrn_norm[0].reshape(1, R_DK),
    )

    y_p, k_p, v_p, st = _group(x_prompt, mods, lambda b: 0, w, (None, None, None), None,
                               tq=lp, pps=8, hps=R_HEADS, tm_in=lp, tm_ffn=lp, emit_state=True)
    ctx = (cache_k[:, 0].reshape(bs, n_ctx, KV_W), cache_v[:, 0].reshape(bs, n_ctx, KV_W),
           state_hgrn[:, 0].reshape(bs * 2 * R_HEADS, R_DK, R_DK))
    y_s, _, _, _ = _group(x_sample, mods, lambda b: b + 1, w, ctx, _rope_tables(ls),
                          tq=512, pps=2, hps=1, tm_in=512, tm_ffn=1024, emit_state=False)

    sf, sb = st
    new_state = jnp.stack([sf.reshape(bp, R_HEADS, R_DK, R_DK),
                           sb.reshape(bp, R_HEADS, R_DK, R_DK)], axis=1)[:, None]
    new_k = k_p.reshape(bp, 1, lp, N_KV_HEADS, HEAD_DIM)
    new_v = v_p.reshape(bp, 1, lp, N_KV_HEADS, HEAD_DIM)
    return (y_p, y_s, new_k, new_v, new_state)
```

```python
import functools

import jax
import jax.numpy as jnp
import numpy as np
from jax import lax
from jax.experimental import pallas as pl
from jax.experimental.pallas import tpu as pltpu

F32 = jnp.float32
BF16 = jnp.bfloat16

D_MODEL = 1024
GRID_W = 64
HEAD_DIM = 64
N_HEADS = 16
N_KV_HEADS = 4
KV_W = N_KV_HEADS * HEAD_DIM
ROPE_THETA = 10000.0
R_DK = 128
R_HEADS = 8
CHUNK = 64
D_FF = 2816
EPS = 1e-6
LOG2E = 1.4426950408889634
IN_W = 8704

LANES = 128
HALO = 16
VT_ROWS = HEAD_DIM + 16
SM_ROWS = 32
SHIFT_SLACK = 100.0
VMEM_LIMIT = 56 * 1024 * 1024

_Q0, _K0, _V0, _RQ0, _RFF0, _RFB0, _RI0, _RG0, _ZA0, _ZR0 = (
    0, 1024, 1280, 1536, 2560, 3584, 4608, 5632, 6656, 7680)


def _sigmoid(x):
    return 1.0 / (1.0 + jnp.exp(-x))


def _cparams(sem):
    return pltpu.CompilerParams(dimension_semantics=sem, vmem_limit_bytes=VMEM_LIMIT)


def _resident(shape):
    nd = len(shape)
    return pl.BlockSpec(shape, lambda *_: (0,) * nd, pipeline_mode=pl.Buffered(1))


def _mod_kernel(c_ref, w_ref, b_ref, o_ref):
    c = c_ref[...]
    s = c * _sigmoid(c)
    o_ref[...] = jnp.dot(s, w_ref[...], preferred_element_type=F32,
                         precision=lax.Precision.HIGHEST) + b_ref[...]


def _mods(cond, ada_w, ada_b):
    rows, d = cond.shape
    n = ada_w.shape[1]
    tn = 1536
    return pl.pallas_call(
        _mod_kernel,
        name="mods",
        grid=(n // tn,),
        in_specs=[pl.BlockSpec((rows, d), lambda j: (0, 0)),
                  pl.BlockSpec((d, tn), lambda j: (0, j)),
                  pl.BlockSpec((1, tn), lambda j: (0, j))],
        out_specs=pl.BlockSpec((rows, tn), lambda j: (0, j)),
        out_shape=jax.ShapeDtypeStruct((rows, n), F32),
        compiler_params=_cparams(("arbitrary",)),
    )(cond, ada_w, ada_b.reshape(1, n))


def _inproj_kernel(*refs, rope):
    if rope:
        (x_ref, mod_ref, n1_ref, w_ref, qg_ref, kg_ref, lbl_ref, cs_ref, sn_ref,
         q_ref, k_ref, v_ref, rq_ref, g_ref, ri_ref, rg_ref, gt_ref) = refs
    else:
        (x_ref, mod_ref, n1_ref, w_ref, qg_ref, kg_ref, lbl_ref,
         q_ref, k_ref, v_ref, rq_ref, g_ref, ri_ref, rg_ref, gt_ref) = refs
        cs_ref = sn_ref = None
    tm = x_ref.shape[0]

    x = x_ref[...]
    ms = jnp.mean(x * x, axis=-1, keepdims=True)
    h = x * lax.rsqrt(ms + EPS) * n1_ref[...]
    h = h * (1.0 + mod_ref[1:2, :]) + mod_ref[0:1, :]
    hb = h.astype(BF16)

    def proj(a, b):
        return jnp.dot(hb, w_ref[:, a:b], preferred_element_type=F32)

    gr = lax.broadcasted_iota(jnp.int32, (LANES, LANES), 0) // HEAD_DIM
    gc = lax.broadcasted_iota(jnp.int32, (LANES, LANES), 1) // HEAD_DIM
    gmat = jnp.where(gr == gc, 1.0 / HEAD_DIM, 0.0).astype(BF16)

    def headnorm(a, gain):
        ss = jnp.dot((a * a).astype(BF16), gmat, preferred_element_type=F32)
        return a * lax.rsqrt(ss + EPS) * gain

    if rope:
        lane = lax.broadcasted_iota(jnp.int32, (tm, LANES), 1)
        first = (lane % 32) < 16
        cs = cs_ref[...]
        sn = sn_ref[...]

        def rot(a):
            sw = jnp.where(first, pltpu.roll(a, LANES - 16, 1), pltpu.roll(a, 16, 1))
            return a * cs + sw * sn
    else:
        def rot(a):
            return a

    scale = HEAD_DIM ** -0.5 * LOG2E
    for c0 in range(0, 1024, 512):
        acc = proj(_Q0 + c0, _Q0 + c0 + 512)
        for s in range(4):
            col = c0 + s * LANES
            a = headnorm(acc[:, s * LANES:(s + 1) * LANES], qg_ref[:, col:col + LANES] * scale)
            q_ref[:, col:col + LANES] = rot(a).astype(BF16)

    acc = proj(_K0, _K0 + 512)
    for s in range(2):
        col = s * LANES
        a = headnorm(acc[:, col:col + LANES], kg_ref[:, col:col + LANES])
        k_ref[:, col:col + LANES] = rot(a)
    v_ref[...] = acc[:, 256:512]

    rscale = R_DK ** -0.5
    for c0 in range(0, 1024, 512):
        acc = proj(_RQ0 + c0, _RQ0 + c0 + 512)
        rq_ref[:, c0:c0 + 512] = (acc * _sigmoid(acc) * rscale).astype(BF16)

    l = lbl_ref[...]
    for d in range(2):
        a0 = l[2 * d:2 * d + 1, :]
        a1 = l[2 * d + 1:2 * d + 2, :]
        mx = jnp.maximum(a0, a1)
        e0 = jnp.exp(a0 - mx)
        e1 = jnp.exp(a1 - mx)
        lb = e0 / (e0 + e1)
        for c0 in range(0, 1024, 512):
            acc = proj(_RFF0 + d * 1024 + c0, _RFF0 + d * 1024 + c0 + 512)
            lbc = lb[:, c0:c0 + 512]
            f = lbc + (1.0 - lbc) * _sigmoid(acc)
            g_ref[:, d * 1024 + c0:d * 1024 + c0 + 512] = jnp.log(f)

    for c0 in range(0, 1024, 512):
        ri_ref[:, c0:c0 + 512] = proj(_RI0 + c0, _RI0 + c0 + 512).astype(BF16)
    for c0 in range(0, 1024, 512):
        acc = proj(_RG0 + c0, _RG0 + c0 + 512)
        rg_ref[:, c0:c0 + 512] = (acc * _sigmoid(acc)).astype(BF16)
    for c0 in range(0, 2048, 512):
        acc = proj(_ZA0 + c0, _ZA0 + c0 + 512)
        gt_ref[:, c0:c0 + 512] = _sigmoid(acc).astype(BF16)


def _inproj(x2d, mods, mod_row, seq_len, n1, w_in, qg, kg, lbl, cs, sn, tm):
    t, d = x2d.shape
    tiles_per_seq = seq_len // tm
    rope = cs is not None

    def tok(w):
        return pl.BlockSpec((tm, w), lambda i: (i, 0))

    in_specs = [tok(d),
                pl.BlockSpec((None, 6, d), lambda i: (mod_row(i // tiles_per_seq), 0, 0)),
                _resident((1, d)),
                _resident((d, IN_W)),
                _resident((1, 1024)),
                _resident((1, KV_W)),
                _resident((4, 1024))]
    args = [x2d, mods, n1, w_in, qg, kg, lbl]
    if rope:
        in_specs += [pl.BlockSpec((tm, LANES), lambda i: (i % tiles_per_seq, 0))] * 2
        args += [cs, sn]
    out_shape = [jax.ShapeDtypeStruct((t, 1024), BF16),
                 jax.ShapeDtypeStruct((t, KV_W), F32),
                 jax.ShapeDtypeStruct((t, KV_W), F32),
                 jax.ShapeDtypeStruct((t, 1024), BF16),
                 jax.ShapeDtypeStruct((t, 2048), F32),
                 jax.ShapeDtypeStruct((t, 1024), BF16),
                 jax.ShapeDtypeStruct((t, 1024), BF16),
                 jax.ShapeDtypeStruct((t, 2048), BF16)]
    out_specs = [tok(s.shape[1]) for s in out_shape]
    return pl.pallas_call(
        functools.partial(_inproj_kernel, rope=rope),
        name="inproj_rope" if rope else "inproj",
        grid=(t // tm,),
        in_specs=in_specs, out_specs=out_specs, out_shape=out_shape,
        compiler_params=_cparams(("arbitrary",)),
    )(*args)


def _attn_kernel(*refs, n_lat, n_ctx, tq, pps, kchunk):
    if n_ctx:
        q_ref, k_ref, v_ref, ck_ref, cv_ref, o_ref, kpad, vt, p0_scr, p1_scr, ref_scr = refs
    else:
        q_ref, k_ref, v_ref, o_ref, kpad, vt, p0_scr, p1_scr, ref_scr = refs
        ck_ref = cv_ref = None
    lk = n_lat + n_ctx
    first_step = jnp.logical_and(pl.program_id(1) == 0, pl.program_id(2) == 0)

    @pl.when(first_step)
    def _build():
        def place(dst, rows, ks):
            lane = lax.broadcasted_iota(jnp.int32, ks.shape, 1)
            lo = jnp.where(lane < HEAD_DIM, ks, 0.0)
            hi = jnp.where(lane >= HEAD_DIM, ks, 0.0)
            kpad[dst + 0, rows, :] = lo.astype(BF16)
            kpad[dst + 1, rows, :] = pltpu.roll(lo, HEAD_DIM, 1).astype(BF16)
            kpad[dst + 2, rows, :] = pltpu.roll(hi, HEAD_DIM, 1).astype(BF16)
            kpad[dst + 3, rows, :] = hi.astype(BF16)

        def place_v(s, cols_out, vs):
            vtr = vs.T.astype(BF16)
            vt[2 * s, 0:HEAD_DIM, cols_out] = vtr[0:HEAD_DIM, :]
            vt[2 * s + 1, 0:HEAD_DIM, cols_out] = vtr[HEAD_DIM:2 * HEAD_DIM, :]

        ones_rows = jnp.where(lax.broadcasted_iota(jnp.int32, (VT_ROWS - HEAD_DIM, lk), 0) == 0,
                              1.0, 0.0).astype(BF16)
        for kvh in range(N_KV_HEADS):
            vt[kvh, HEAD_DIM:VT_ROWS, :] = ones_rows
        for s in range(2):
            cols = slice(s * LANES, (s + 1) * LANES)
            place(4 * s, slice(0, n_lat), k_ref[:, cols])
            place_v(s, slice(0, n_lat), v_ref[:, cols])
            if n_ctx:
                place(4 * s, slice(n_lat, lk), ck_ref[:, cols])
                place_v(s, slice(n_lat, lk), cv_ref[:, cols])

    n_chunks = lk // kchunk
    p_bufs = (p0_scr, p1_scr)
    pair0 = pl.program_id(2) * pps
    nt = (((1,), (1,)), ((), ()))
    units = [(j, parity) for j in range(pps) for parity in range(2)]

    def unit_operands(u):
        j, parity = units[u]
        kv = (pair0 + j) // 2
        return 2 * kv + parity, kv, q_ref[:, j * LANES:(j + 1) * LANES]

    def bcast_max(x8):
        return jnp.broadcast_to(jnp.max(x8, axis=0, keepdims=True), (8, tq))

    def exact_max():
        for u in range(len(units)):
            kidx, _, q2 = unit_operands(u)
            mx = jnp.full((8, tq), -jnp.inf, F32)
            for c in range(n_chunks):
                s = lax.dot_general(kpad[kidx, c * kchunk:(c + 1) * kchunk, :], q2, nt,
                                    preferred_element_type=F32)
                mx = jnp.maximum(mx, jnp.max(s.reshape(kchunk // 8, 8, tq), axis=0))
            ref_scr[u] = bcast_max(mx)

    def softmax_pass():
        worst = jnp.zeros((8, tq), F32)
        outs = []
        for u in range(len(units)):
            j, parity = units[u]
            kidx, kv, q2 = unit_operands(u)
            p_scr = p_bufs[u % 2]
            r8 = ref_scr[u]
            s = lax.dot_general(kpad[kidx], q2, nt, preferred_element_type=F32)
            mtot = jnp.full((8, tq), -jnp.inf, F32)
            for r0 in range(0, lk, SM_ROWS):
                sb = s[r0:r0 + SM_ROWS, :].reshape(SM_ROWS // 8, 8, tq)
                mtot = jnp.maximum(mtot, jnp.max(sb, axis=0))
                p_scr[r0:r0 + SM_ROWS, :] = jnp.exp2(sb - r8).reshape(SM_ROWS, tq).astype(BF16)
            worst = jnp.maximum(worst, jnp.abs(bcast_max(mtot) - r8))
            ot = jnp.dot(vt[kv], p_scr[...], preferred_element_type=F32)
            outs.append(ot[0:HEAD_DIM, :] * (1.0 / ot[HEAD_DIM:HEAD_DIM + 1, :]))
            if parity == 1:
                o2t = jnp.concatenate(outs[-2:], axis=0)
                o_ref[:, j * LANES:(j + 1) * LANES] = o2t.T.astype(BF16)
        return worst

    def attempt(state):
        n, _ = state

        @pl.when(n == 0)
        def _():
            ref_scr[...] = jnp.zeros(ref_scr.shape, F32)

        @pl.when(n == 1)
        def _():
            exact_max()

        worst = softmax_pass()
        return n + 1, (jnp.max(worst) > SHIFT_SLACK).astype(jnp.int32)

    lax.while_loop(lambda st: jnp.logical_or(st[0] == 0, jnp.logical_and(st[0] == 1, st[1] == 1)),
                   attempt, (jnp.int32(0), jnp.int32(0)))


def _attention(q, k, v, ck, cv, batch, seq_len, tq, pps):
    t = q.shape[0]
    n_ctx = 0 if ck is None else ck.shape[1]
    lk = seq_len + n_ctx
    n_qt = seq_len // tq
    n_pp = (N_HEADS // 2) // pps
    kchunk = min(lk, 512)
    in_specs = [pl.BlockSpec((tq, pps * LANES), lambda b, i, p: (b * n_qt + i, p)),
                pl.BlockSpec((seq_len, KV_W), lambda b, i, p: (b, 0)),
                pl.BlockSpec((seq_len, KV_W), lambda b, i, p: (b, 0))]
    args = [q, k, v]
    if n_ctx:
        in_specs += [pl.BlockSpec((None, n_ctx, KV_W), lambda b, i, p: (b, 0, 0))] * 2
        args += [ck, cv]
    return pl.pallas_call(
        functools.partial(_attn_kernel, n_lat=seq_len, n_ctx=n_ctx, tq=tq, pps=pps, kchunk=kchunk),
        name="attn_ctx" if n_ctx else "attn",
        grid=(batch, n_qt, n_pp),
        in_specs=in_specs,
        out_specs=pl.BlockSpec((tq, pps * LANES), lambda b, i, p: (b * n_qt + i, p)),
        out_shape=jax.ShapeDtypeStruct((t, 1024), BF16),
        scratch_shapes=[pltpu.VMEM((2 * N_KV_HEADS, lk, LANES), BF16),
                        pltpu.VMEM((N_KV_HEADS, VT_ROWS, lk), BF16),
                        pltpu.VMEM((lk, tq), BF16), pltpu.VMEM((lk, tq), BF16),
                        pltpu.VMEM((2 * pps, 8, tq), F32)],
        compiler_params=_cparams(("arbitrary", "arbitrary", "arbitrary")),
    )(*args)


def _rec_kernel(*refs, seq_len, has_s0, emit_state, hps):
    refs = list(refs)
    rq_ref, gf_ref, gb_ref, ri_ref, rg_ref, gain_ref = refs[:6]
    pos = 6
    s0f_ref = s0b_ref = sf_ref = sb_ref = None
    if has_s0:
        s0f_ref, s0b_ref = refs[pos:pos + 2]
        pos += 2
    rec_ref = refs[pos]
    pos += 1
    if emit_state:
        sf_ref, sb_ref = refs[pos:pos + 2]
        pos += 2
    scratch = refs[pos:]
    for hh in range(hps):
        def cols(r):
            return r.at[:, pl.ds(hh * R_DK, R_DK)]

        def head(r):
            return None if r is None else r.at[hh]

        _rec_head(cols(rq_ref), cols(gf_ref), cols(gb_ref), cols(ri_ref), cols(rg_ref), gain_ref,
                  head(s0f_ref), head(s0b_ref), cols(rec_ref), head(sf_ref), head(sb_ref),
                  scratch, seq_len)


def _rec_head(rq_ref, gf_ref, gb_ref, ri_ref, rg_ref, gain_ref, s0f_ref, s0b_ref, rec_ref, sf_ref, sb_ref,
              scratch, seq_len):
    has_s0 = s0f_ref is not None
    emit_state = sf_ref is not None
    qd_scr, d_scr, u_scr, st_scr, a_scr = scratch
    nc = seq_len // CHUNK
    unroll = min(nc, 8)
    tpos = lax.broadcasted_iota(jnp.int32, (CHUNK, R_DK), 0)
    ti = lax.broadcasted_iota(jnp.int32, (CHUNK, CHUNK), 0)
    si = lax.broadcasted_iota(jnp.int32, (CHUNK, CHUNK), 1)
    nt = (((1,), (1,)), ((), ()))
    tn = (((0,), (0,)), ((), ()))

    def rows(c):
        return pl.ds(pl.multiple_of(c * CHUNK, CHUNK), CHUNK)

    def chunk_cumsum(g, reverse):
        b = g
        for s in (1, 2, 4):
            if reverse:
                sh = pltpu.roll(b, CHUNK - s, 0)
                b = b + jnp.where(tpos < CHUNK - s, sh, 0.0)
            else:
                sh = pltpu.roll(b, s, 0)
                b = b + jnp.where(tpos >= s, sh, 0.0)
        for s in (8, 16, 32):
            if reverse:
                b = jnp.concatenate([b[:CHUNK - s] + b[s:], b[CHUNK - s:]], axis=0)
            else:
                b = jnp.concatenate([b[:s], b[s:] + b[:CHUNK - s]], axis=0)
        return b

    def ua_body(c, carry):
        r = rows(c)
        rq = rq_ref[r, :].astype(F32)
        a = None
        ke = []
        for d, g_ref in enumerate((gf_ref, gb_ref)):
            g = g_ref[r, :]
            kk = 1.0 - jnp.exp(g)
            b = chunk_cumsum(g, reverse=(d == 1))
            etot = jnp.broadcast_to(jnp.exp(jnp.sum(g, axis=0, keepdims=True)), (CHUNK, R_DK))
            eb = jnp.exp(b)
            qd = (rq * eb).astype(BF16)
            kd = kk / eb
            ke.append((kd * etot).astype(BF16))
            qd_scr[r, d * R_DK:(d + 1) * R_DK] = qd
            d_scr[r, d * R_DK:(d + 1) * R_DK] = etot
            sc = lax.dot_general(qd, kd.astype(BF16), nt, preferred_element_type=F32)
            sc = jnp.where(si <= ti, sc, 0.0) if d == 0 else jnp.where(si >= ti, sc, 0.0)
            a = sc if a is None else a + sc
        a_scr[r, :] = a.astype(BF16)
        u_scr[c] = lax.dot_general(ri_ref[r, :], jnp.concatenate(ke, axis=1), tn, preferred_element_type=F32)
        return carry

    lax.fori_loop(0, nc, ua_body, 0, unroll=unroll)

    if has_s0:
        sf0 = s0f_ref[...].T
        sb0 = s0b_ref[...].T
    else:
        sf0 = jnp.zeros((R_DK, R_DK), F32)
        sb0 = jnp.zeros((R_DK, R_DK), F32)

    def scan_body(i, carry):
        sf, sb = carry
        cb = nc - 1 - i
        st_scr[i, :, 0:R_DK] = sf.astype(BF16)
        st_scr[cb, :, R_DK:2 * R_DK] = sb.astype(BF16)
        df = d_scr[pl.ds(pl.multiple_of(i * CHUNK, CHUNK), 1), 0:R_DK]
        db = d_scr[pl.ds(pl.multiple_of(cb * CHUNK, CHUNK), 1), R_DK:2 * R_DK]
        sf = df * sf + u_scr[i, :, 0:R_DK]
        sb = db * sb + u_scr[cb, :, R_DK:2 * R_DK]
        return sf, sb

    sf, sb = lax.fori_loop(0, nc, scan_body, (sf0, sb0))
    if emit_state:
        sf_ref[...] = sf.T
        sb_ref[...] = sb.T

    gain = gain_ref[...]

    def o_body(c, carry):
        r = rows(c)
        o = jnp.dot(a_scr[r, :], ri_ref[r, :], preferred_element_type=F32)
        o = o + lax.dot_general(qd_scr[r, :], st_scr[c], nt, preferred_element_type=F32)
        ms = jnp.mean(o * o, axis=-1, keepdims=True)
        y = o * lax.rsqrt(ms + EPS) * gain * rg_ref[r, :].astype(F32)
        rec_ref[r, :] = y.astype(BF16)
        return carry

    lax.fori_loop(0, nc, o_body, 0, unroll=unroll)


def _recurrence(rq, g, ri, rg, gain, s0, batch, seq_len, emit_state, hps):
    t = rq.shape[0]
    nc = seq_len // CHUNK
    has_s0 = s0 is not None
    hb = R_HEADS // hps

    def col(off):
        return pl.BlockSpec((seq_len, hps * R_DK), lambda b, h: (b, h + off))

    def state(nblk, off):
        return pl.BlockSpec((hps, R_DK, R_DK), lambda b, h: (b * nblk + off + h, 0, 0))

    in_specs = [col(0), col(0), col(hb), col(0), col(0), _resident((1, R_DK))]
    args = [rq, g, g, ri, rg, gain]
    if has_s0:
        in_specs += [state(2 * hb, 0), state(2 * hb, hb)]
        args += [s0, s0]
    out_shape = [jax.ShapeDtypeStruct((t, 1024), BF16)]
    out_specs = [col(0)]
    if emit_state:
        st_shape = jax.ShapeDtypeStruct((batch * R_HEADS, R_DK, R_DK), F32)
        out_shape += [st_shape, st_shape]
        out_specs += [state(hb, 0)] * 2
    scratch = [pltpu.VMEM((seq_len, 2 * R_DK), BF16),
               pltpu.VMEM((seq_len, 2 * R_DK), F32),
               pltpu.VMEM((nc, R_DK, 2 * R_DK), F32),
               pltpu.VMEM((nc, R_DK, 2 * R_DK), BF16),
               pltpu.VMEM((seq_len, CHUNK), BF16)]
    return pl.pallas_call(
        functools.partial(_rec_kernel, seq_len=seq_len, has_s0=has_s0, emit_state=emit_state, hps=hps),
        name="rec_s0" if has_s0 else "rec",
        grid=(batch, hb),
        in_specs=in_specs, out_specs=out_specs, out_shape=out_shape,
        scratch_shapes=scratch,
        compiler_params=_cparams(("arbitrary", "arbitrary")),
    )(*args)


def _mffn_kernel(x_ref, a_ref, r_ref, g_ref, xp_ref, ap_ref, rp_ref, gp_ref, xn_ref, an_ref, rn_ref, gn_ref,
                 wo_ref, n2_ref, wup_ref, cw_ref, cb_ref, wdn_ref, mod_ref, fn_ref,
                 y_ref, hbuf, act_scr, x1_scr, *, tiles_per_seq, ffc):
    tm = x_ref.shape[0]
    ts = pl.program_id(0) % tiles_per_seq

    def merge_rows(x, a, r, g):
        m = g[:, 0:D_MODEL].astype(F32) * a.astype(F32) + g[:, D_MODEL:2 * D_MODEL].astype(F32) * r.astype(F32)
        out = jnp.dot(m.astype(BF16), wo_ref[...], preferred_element_type=F32)
        x1 = x + mod_ref[2:3, :] * out
        ms = jnp.mean(x1 * x1, axis=-1, keepdims=True)
        h2 = x1 * lax.rsqrt(ms + EPS) * n2_ref[...]
        h2 = h2 * (1.0 + mod_ref[4:5, :]) + mod_ref[3:4, :]
        return x1, h2.astype(BF16)

    x1, h2 = merge_rows(x_ref[...], a_ref[...], r_ref[...], g_ref[...])
    x1_scr[...] = x1
    hbuf[HALO:HALO + tm, :] = h2
    _, hh = merge_rows(jnp.concatenate([xp_ref[...], xn_ref[...]], axis=0),
                       jnp.concatenate([ap_ref[...], an_ref[...]], axis=0),
                       jnp.concatenate([rp_ref[...], rn_ref[...]], axis=0),
                       jnp.concatenate([gp_ref[...], gn_ref[...]], axis=0))
    zero = jnp.zeros((HALO, D_MODEL), BF16)
    hbuf[0:HALO, :] = jnp.where(ts == 0, zero, hh[0:HALO, :])
    hbuf[HALO + tm:HALO + tm + HALO, :] = jnp.where(ts == tiles_per_seq - 1, zero, hh[HALO:2 * HALO, :])
    hb = hbuf[...]
    mrows = tm + 2 * HALO

    def conv(col):
        u = jnp.dot(hb, wup_ref[:, col:col + ffc], preferred_element_type=F32)
        up = pltpu.roll(u, 1, 0)[HALO:HALO + tm, :]
        un = pltpu.roll(u, mrows - 1, 0)[HALO:HALO + tm, :]
        uc = u[HALO:HALO + tm, :]
        w = cw_ref[:, col:col + ffc]
        return up * w[0:1, :] + uc * w[1:2, :] + un * w[2:3, :] + cb_ref[:, col:col + ffc]

    for c0 in range(0, D_FF, ffc):
        a = conv(c0)
        b = conv(D_FF + c0)
        act_scr[:, c0:c0 + ffc] = (a * _sigmoid(a) * b).astype(BF16)

    f = jnp.dot(act_scr[...], wdn_ref[...], preferred_element_type=F32)
    x2 = x1_scr[...] + mod_ref[5:6, :] * f
    ms = jnp.mean(x2 * x2, axis=-1, keepdims=True)
    y_ref[...] = x2 * lax.rsqrt(ms + EPS) * fn_ref[...]


def _merge_ffn(x2d, attn, rec, gates, w, mods, mod_row, seq_len, tm):
    t, d = x2d.shape
    tiles_per_seq = seq_len // tm
    hb = tm // HALO
    n_hb = t // HALO

    def main(width):
        return pl.BlockSpec((tm, width), lambda i: (i, 0))

    def prev(width):
        return pl.BlockSpec((HALO, width), lambda i: (jnp.maximum(i * hb - 1, 0), 0))

    def nxt(width):
        return pl.BlockSpec((HALO, width), lambda i: (jnp.minimum((i + 1) * hb, n_hb - 1), 0))

    toks = [x2d, attn, rec, gates]
    widths = [d, d, d, 2 * d]
    return pl.pallas_call(
        functools.partial(_mffn_kernel, tiles_per_seq=tiles_per_seq, ffc=256),
        name="merge_ffn",
        grid=(t // tm,),
        in_specs=([main(wd) for wd in widths] + [prev(wd) for wd in widths] + [nxt(wd) for wd in widths]
                  + [_resident((d, d)), _resident((1, d)),
                     _resident((d, 2 * D_FF)), _resident((3, 2 * D_FF)), _resident((1, 2 * D_FF)),
                     _resident((D_FF, d)),
                     pl.BlockSpec((None, 6, d), lambda i: (mod_row(i // tiles_per_seq), 0, 0)),
                     _resident((1, d))]),
        out_specs=pl.BlockSpec((tm, d), lambda i: (i, 0)),
        out_shape=jax.ShapeDtypeStruct((t, d), F32),
        scratch_shapes=[pltpu.VMEM((tm + 2 * HALO, d), BF16),
                        pltpu.VMEM((tm, D_FF), BF16),
                        pltpu.VMEM((tm, d), F32)],
        compiler_params=_cparams(("arbitrary",)),
    )(*toks, *toks, *toks, w["w_o"], w["n2"], w["w_up"], w["conv_w"], w["conv_b"], w["w_down"], mods, w["fnorm"])


def _rope_tables(n_tokens):
    rows = n_tokens // GRID_W
    row = jnp.repeat(jnp.arange(rows, dtype=F32), GRID_W)
    colp = jnp.tile(jnp.arange(GRID_W, dtype=F32), rows)
    half = HEAD_DIM // 2
    inv_freq = 1.0 / (ROPE_THETA ** (jnp.arange(0, half, 2, dtype=F32) / half))
    ar = row[:, None] * inv_freq
    ac = colp[:, None] * inv_freq
    ang = jnp.concatenate([ar, ar, ac, ac], axis=-1)
    sign = jnp.asarray(np.tile(np.repeat(np.array([-1.0, 1.0], np.float32), 16), 2))
    cs = jnp.cos(ang)
    sn = jnp.sin(ang) * sign
    return jnp.tile(cs, (1, 2)), jnp.tile(sn, (1, 2))


def _group(x, mods, mod_row, w, ctx, rope_tabs, tq, pps, hps, tm_in, tm_ffn, emit_state):
    batch, seq_len, d = x.shape
    x2d = x.reshape(batch * seq_len, d)
    cs, sn = rope_tabs if rope_tabs is not None else (None, None)
    q, k, v, rq, g, ri, rg, gates = _inproj(
        x2d, mods, mod_row, seq_len, w["n1"], w["w_in"], w["qg"], w["kg"], w["lbl"], cs, sn, tm=tm_in)
    ck, cv, s0 = ctx
    attn = _attention(q, k, v, ck, cv, batch, seq_len, tq, pps)
    rec_out = _recurrence(rq, g, ri, rg, w["rgain"], s0, batch, seq_len, emit_state, hps)
    rec = rec_out[0]
    y = _merge_ffn(x2d, attn, rec, gates, w, mods, mod_row, seq_len, tm=tm_ffn)
    return y.reshape(batch, seq_len, d), k, v, rec_out[1:]


def kernel(x_prompt, x_sample, c, cache_k, cache_v, state_hgrn, c_ctx, ada_w, ada_b, norm1, norm2,
           w_in, q_norm, k_norm, hgrn_lb_logits, hgrn_norm, w_o, w_up, conv_w, conv_b, w_down, final_norm):
    bp, lp, d = x_prompt.shape
    bs, ls, _ = x_sample.shape
    n_ctx = cache_k.shape[2]

    cond = jnp.zeros((16, d), F32).at[0].set(c_ctx).at[1:1 + bs].set(c)
    mods = _mods(cond, ada_w[0], ada_b[0]).reshape(16, 6, d)

    w = dict(
        n1=norm1[0].reshape(1, d), n2=norm2[0].reshape(1, d), fnorm=final_norm.reshape(1, d),
        w_in=w_in[0].astype(BF16), w_o=w_o[0].astype(BF16),
        w_up=w_up[0].astype(BF16), w_down=w_down[0].astype(BF16),
        conv_w=conv_w[0], conv_b=conv_b[0].reshape(1, 2 * D_FF),
        qg=jnp.tile(q_norm[0], N_HEADS).reshape(1, 1024),
        kg=jnp.tile(k_norm[0], N_KV_HEADS).reshape(1, KV_W),
        lbl=hgrn_lb_logits.reshape(4, 1024),
        rgain=hgrn_norm[0].reshape(1, R_DK),
    )

    y_p, k_p, v_p, st = _group(x_prompt, mods, lambda b: 0, w, (None, None, None), None,
                               tq=lp, pps=8, hps=R_HEADS, tm_in=lp, tm_ffn=lp, emit_state=True)
    ctx = (cache_k[:, 0].reshape(bs, n_ctx, KV_W), cache_v[:, 0].reshape(bs, n_ctx, KV_W),
           state_hgrn[:, 0].reshape(bs * 2 * R_HEADS, R_DK, R_DK))
    y_s, _, _, _ = _group(x_sample, mods, lambda b: b + 1, w, ctx, _rope_tables(ls),
                          tq=512, pps=2, hps=1, tm_in=512, tm_ffn=512, emit_state=False)

    sf, sb = st
    new_state = jnp.stack([sf.reshape(bp, R_HEADS, R_DK, R_DK),
                           sb.reshape(bp, R_HEADS, R_DK, R_DK)], axis=1)[:, None]
    new_k = k_p.reshape(bp, 1, lp, N_KV_HEADS, HEAD_DIM)
    new_v = v_p.reshape(bp, 1, lp, N_KV_HEADS, HEAD_DIM)
    return (y_p, y_s, new_k, new_v, new_state)
```

```python
import functools

import jax
import jax.numpy as jnp
import numpy as np
from jax import lax
from jax.experimental import pallas as pl
from jax.experimental.pallas import tpu as pltpu

F32 = jnp.float32
BF16 = jnp.bfloat16

D_MODEL = 1024
GRID_W = 64
HEAD_DIM = 64
N_HEADS = 16
N_KV_HEADS = 4
KV_W = N_KV_HEADS * HEAD_DIM
ROPE_THETA = 10000.0
R_DK = 128
R_HEADS = 8
CHUNK = 64
D_FF = 2816
EPS = 1e-6
LOG2E = 1.4426950408889634
IN_W = 8704

LANES = 128
HALO = 16
VT_ROWS = HEAD_DIM + 16
SM_ROWS = 32
SHIFT_SLACK = 100.0
VMEM_LIMIT = 56 * 1024 * 1024

_Q0, _K0, _V0, _RQ0, _RFF0, _RFB0, _RI0, _RG0, _ZA0, _ZR0 = (
    0, 1024, 1280, 1536, 2560, 3584, 4608, 5632, 6656, 7680)


def _sigmoid(x):
    return 1.0 / (1.0 + jnp.exp(-x))


def _cparams(sem):
    return pltpu.CompilerParams(dimension_semantics=sem, vmem_limit_bytes=VMEM_LIMIT)


def _resident(shape):
    nd = len(shape)
    return pl.BlockSpec(shape, lambda *_: (0,) * nd, pipeline_mode=pl.Buffered(1))


def _mod_kernel(c_ref, w_ref, b_ref, o_ref):
    c = c_ref[...]
    s = c * _sigmoid(c)
    o_ref[...] = jnp.dot(s, w_ref[...], preferred_element_type=F32,
                         precision=lax.Precision.HIGHEST) + b_ref[...]


def _mods(cond, ada_w, ada_b):
    rows, d = cond.shape
    n = ada_w.shape[1]
    tn = 1536
    return pl.pallas_call(
        _mod_kernel,
        name="mods",
        grid=(n // tn,),
        in_specs=[pl.BlockSpec((rows, d), lambda j: (0, 0)),
                  pl.BlockSpec((d, tn), lambda j: (0, j)),
                  pl.BlockSpec((1, tn), lambda j: (0, j))],
        out_specs=pl.BlockSpec((rows, tn), lambda j: (0, j)),
        out_shape=jax.ShapeDtypeStruct((rows, n), F32),
        compiler_params=_cparams(("arbitrary",)),
    )(cond, ada_w, ada_b.reshape(1, n))


def _inproj_kernel(*refs, rope):
    if rope:
        (x_ref, mod_ref, n1_ref, w_ref, qg_ref, kg_ref, lbl_ref, cs_ref, sn_ref,
         q_ref, k_ref, v_ref, rq_ref, g_ref, ri_ref, rg_ref, gt_ref) = refs
    else:
        (x_ref, mod_ref, n1_ref, w_ref, qg_ref, kg_ref, lbl_ref,
         q_ref, k_ref, v_ref, rq_ref, g_ref, ri_ref, rg_ref, gt_ref) = refs
        cs_ref = sn_ref = None
    tm = x_ref.shape[0]

    x = x_ref[...]
    ms = jnp.mean(x * x, axis=-1, keepdims=True)
    h = x * lax.rsqrt(ms + EPS) * n1_ref[...]
    h = h * (1.0 + mod_ref[1:2, :]) + mod_ref[0:1, :]
    hb = h.astype(BF16)

    def proj(a, b):
        return jnp.dot(hb, w_ref[:, a:b], preferred_element_type=F32)

    gr = lax.broadcasted_iota(jnp.int32, (LANES, LANES), 0) // HEAD_DIM
    gc = lax.broadcasted_iota(jnp.int32, (LANES, LANES), 1) // HEAD_DIM
    gmat = jnp.where(gr == gc, 1.0 / HEAD_DIM, 0.0).astype(BF16)

    def headnorm(a, gain):
        ss = jnp.dot((a * a).astype(BF16), gmat, preferred_element_type=F32)
        return a * lax.rsqrt(ss + EPS) * gain

    if rope:
        lane = lax.broadcasted_iota(jnp.int32, (tm, LANES), 1)
        first = (lane % 32) < 16
        cs = cs_ref[...]
        sn = sn_ref[...]

        def rot(a):
            sw = jnp.where(first, pltpu.roll(a, LANES - 16, 1), pltpu.roll(a, 16, 1))
            return a * cs + sw * sn
    else:
        def rot(a):
            return a

    scale = HEAD_DIM ** -0.5 * LOG2E
    for c0 in range(0, 1024, 512):
        acc = proj(_Q0 + c0, _Q0 + c0 + 512)
        for s in range(4):
            col = c0 + s * LANES
            a = headnorm(acc[:, s * LANES:(s + 1) * LANES], qg_ref[:, col:col + LANES] * scale)
            q_ref[:, col:col + LANES] = rot(a).astype(BF16)

    acc = proj(_K0, _K0 + 512)
    for s in range(2):
        col = s * LANES
        a = headnorm(acc[:, col:col + LANES], kg_ref[:, col:col + LANES])
        k_ref[:, col:col + LANES] = rot(a)
    v_ref[...] = acc[:, 256:512]

    rscale = R_DK ** -0.5
    for c0 in range(0, 1024, 512):
        acc = proj(_RQ0 + c0, _RQ0 + c0 + 512)
        rq_ref[:, c0:c0 + 512] = (acc * _sigmoid(acc) * rscale).astype(BF16)

    l = lbl_ref[...]
    for d in range(2):
        a0 = l[2 * d:2 * d + 1, :]
        a1 = l[2 * d + 1:2 * d + 2, :]
        mx = jnp.maximum(a0, a1)
        e0 = jnp.exp(a0 - mx)
        e1 = jnp.exp(a1 - mx)
        lb = e0 / (e0 + e1)
        for c0 in range(0, 1024, 512):
            acc = proj(_RFF0 + d * 1024 + c0, _RFF0 + d * 1024 + c0 + 512)
            lbc = lb[:, c0:c0 + 512]
            f = lbc + (1.0 - lbc) * _sigmoid(acc)
            g_ref[:, d * 1024 + c0:d * 1024 + c0 + 512] = jnp.log2(f)

    for c0 in range(0, 1024, 512):
        ri_ref[:, c0:c0 + 512] = proj(_RI0 + c0, _RI0 + c0 + 512).astype(BF16)
    for c0 in range(0, 1024, 512):
        acc = proj(_RG0 + c0, _RG0 + c0 + 512)
        rg_ref[:, c0:c0 + 512] = (acc * _sigmoid(acc)).astype(BF16)
    for c0 in range(0, 2048, 512):
        acc = proj(_ZA0 + c0, _ZA0 + c0 + 512)
        gt_ref[:, c0:c0 + 512] = _sigmoid(acc).astype(BF16)


def _inproj(x2d, mods, mod_row, seq_len, n1, w_in, qg, kg, lbl, cs, sn, tm):
    t, d = x2d.shape
    tiles_per_seq = seq_len // tm
    rope = cs is not None

    def tok(w):
        return pl.BlockSpec((tm, w), lambda i: (i, 0))

    in_specs = [tok(d),
                pl.BlockSpec((None, 6, d), lambda i: (mod_row(i // tiles_per_seq), 0, 0)),
                _resident((1, d)),
                _resident((d, IN_W)),
                _resident((1, 1024)),
                _resident((1, KV_W)),
                _resident((4, 1024))]
    args = [x2d, mods, n1, w_in, qg, kg, lbl]
    if rope:
        in_specs += [pl.BlockSpec((tm, LANES), lambda i: (i % tiles_per_seq, 0))] * 2
        args += [cs, sn]
    out_shape = [jax.ShapeDtypeStruct((t, 1024), BF16),
                 jax.ShapeDtypeStruct((t, KV_W), F32),
                 jax.ShapeDtypeStruct((t, KV_W), F32),
                 jax.ShapeDtypeStruct((t, 1024), BF16),
                 jax.ShapeDtypeStruct((t, 2048), F32),
                 jax.ShapeDtypeStruct((t, 1024), BF16),
                 jax.ShapeDtypeStruct((t, 1024), BF16),
                 jax.ShapeDtypeStruct((t, 2048), BF16)]
    out_specs = [tok(s.shape[1]) for s in out_shape]
    return pl.pallas_call(
        functools.partial(_inproj_kernel, rope=rope),
        name="inproj_rope" if rope else "inproj",
        grid=(t // tm,),
        in_specs=in_specs, out_specs=out_specs, out_shape=out_shape,
        compiler_params=_cparams(("arbitrary",)),
    )(*args)


def _attn_kernel(*refs, n_lat, n_ctx, tq, pps, kchunk, group):
    if n_ctx:
        q_ref, k_ref, v_ref, ck_ref, cv_ref, o_ref, kpad, vt, ref_scr, *p_bufs = refs
    else:
        q_ref, k_ref, v_ref, o_ref, kpad, vt, ref_scr, *p_bufs = refs
        ck_ref = cv_ref = None
    lk = n_lat + n_ctx
    first_step = jnp.logical_and(pl.program_id(1) == 0, pl.program_id(2) == 0)

    @pl.when(first_step)
    def _build():
        def place(dst, rows, ks):
            lane = lax.broadcasted_iota(jnp.int32, ks.shape, 1)
            lo = jnp.where(lane < HEAD_DIM, ks, 0.0)
            hi = jnp.where(lane >= HEAD_DIM, ks, 0.0)
            kpad[dst + 0, rows, :] = lo.astype(BF16)
            kpad[dst + 1, rows, :] = pltpu.roll(lo, HEAD_DIM, 1).astype(BF16)
            kpad[dst + 2, rows, :] = pltpu.roll(hi, HEAD_DIM, 1).astype(BF16)
            kpad[dst + 3, rows, :] = hi.astype(BF16)

        def place_v(s, cols_out, vs):
            vtr = vs.T.astype(BF16)
            vt[2 * s, 0:HEAD_DIM, cols_out] = vtr[0:HEAD_DIM, :]
            vt[2 * s + 1, 0:HEAD_DIM, cols_out] = vtr[HEAD_DIM:2 * HEAD_DIM, :]

        ones_rows = jnp.where(lax.broadcasted_iota(jnp.int32, (VT_ROWS - HEAD_DIM, lk), 0) == 0,
                              1.0, 0.0).astype(BF16)
        for kvh in range(N_KV_HEADS):
            vt[kvh, HEAD_DIM:VT_ROWS, :] = ones_rows
        for s in range(2):
            cols = slice(s * LANES, (s + 1) * LANES)
            place(4 * s, slice(0, n_lat), k_ref[:, cols])
            place_v(s, slice(0, n_lat), v_ref[:, cols])
            if n_ctx:
                place(4 * s, slice(n_lat, lk), ck_ref[:, cols])
                place_v(s, slice(n_lat, lk), cv_ref[:, cols])

    n_chunks = lk // kchunk
    pair0 = pl.program_id(2) * pps
    nt = (((1,), (1,)), ((), ()))
    units = [(j, parity) for j in range(pps) for parity in range(2)]

    def unit_operands(u):
        j, parity = units[u]
        kv = (pair0 + j) // 2
        return 2 * kv + parity, kv, q_ref[:, j * LANES:(j + 1) * LANES]

    def bcast_max(x8):
        return jnp.broadcast_to(jnp.max(x8, axis=0, keepdims=True), (8, tq))

    def exact_max():
        for u in range(len(units)):
            kidx, _, q2 = unit_operands(u)
            mx = jnp.full((8, tq), -jnp.inf, F32)
            for c in range(n_chunks):
                s = lax.dot_general(kpad[kidx, c * kchunk:(c + 1) * kchunk, :], q2, nt,
                                    preferred_element_type=F32)
                mx = jnp.maximum(mx, jnp.max(s.reshape(kchunk // 8, 8, tq), axis=0))
            ref_scr[u] = bcast_max(mx)

    def softmax_pass():
        worst = jnp.zeros((8, tq), F32)
        outs = []
        for u0 in range(0, len(units), group):
            us = range(u0, u0 + group)
            scores = [lax.dot_general(kpad[unit_operands(u)[0]], unit_operands(u)[2], nt,
                                      preferred_element_type=F32) for u in us]
            for u, s in zip(us, scores):
                p_scr = p_bufs[u % len(p_bufs)]
                r8 = ref_scr[u]
                mtot = jnp.full((8, tq), -jnp.inf, F32)
                for r0 in range(0, lk, SM_ROWS):
                    sb = s[r0:r0 + SM_ROWS, :].reshape(SM_ROWS // 8, 8, tq)
                    mtot = jnp.maximum(mtot, jnp.max(sb, axis=0))
                    p_scr[r0:r0 + SM_ROWS, :] = jnp.exp2(sb - r8).reshape(SM_ROWS, tq).astype(BF16)
                worst = jnp.maximum(worst, jnp.abs(bcast_max(mtot) - r8))
            for u in us:
                ot = jnp.dot(vt[unit_operands(u)[1]], p_bufs[u % len(p_bufs)][...],
                             preferred_element_type=F32)
                outs.append(ot[0:HEAD_DIM, :] * (1.0 / ot[HEAD_DIM:HEAD_DIM + 1, :]))
                if u % 2 == 1:
                    j = units[u][0]
                    o2t = jnp.concatenate(outs[-2:], axis=0)
                    o_ref[:, j * LANES:(j + 1) * LANES] = o2t.T.astype(BF16)
        return worst

    def attempt(state):
        n, _ = state

        @pl.when(n == 0)
        def _():
            ref_scr[...] = jnp.zeros(ref_scr.shape, F32)

        @pl.when(n == 1)
        def _():
            exact_max()

        worst = softmax_pass()
        return n + 1, (jnp.max(worst) > SHIFT_SLACK).astype(jnp.int32)

    lax.while_loop(lambda st: jnp.logical_or(st[0] == 0, jnp.logical_and(st[0] == 1, st[1] == 1)),
                   attempt, (jnp.int32(0), jnp.int32(0)))


def _attention(q, k, v, ck, cv, batch, seq_len, tq, pps, group):
    t = q.shape[0]
    n_ctx = 0 if ck is None else ck.shape[1]
    lk = seq_len + n_ctx
    n_qt = seq_len // tq
    n_pp = (N_HEADS // 2) // pps
    kchunk = min(lk, 512)
    in_specs = [pl.BlockSpec((tq, pps * LANES), lambda b, i, p: (b * n_qt + i, p)),
                pl.BlockSpec((seq_len, KV_W), lambda b, i, p: (b, 0)),
                pl.BlockSpec((seq_len, KV_W), lambda b, i, p: (b, 0))]
    args = [q, k, v]
    if n_ctx:
        in_specs += [pl.BlockSpec((None, n_ctx, KV_W), lambda b, i, p: (b, 0, 0))] * 2
        args += [ck, cv]
    return pl.pallas_call(
        functools.partial(_attn_kernel, n_lat=seq_len, n_ctx=n_ctx, tq=tq, pps=pps, kchunk=kchunk, group=group),
        name="attn_ctx" if n_ctx else "attn",
        grid=(batch, n_qt, n_pp),
        in_specs=in_specs,
        out_specs=pl.BlockSpec((tq, pps * LANES), lambda b, i, p: (b * n_qt + i, p)),
        out_shape=jax.ShapeDtypeStruct((t, 1024), BF16),
        scratch_shapes=[pltpu.VMEM((2 * N_KV_HEADS, lk, LANES), BF16),
                        pltpu.VMEM((N_KV_HEADS, VT_ROWS, lk), BF16),
                        pltpu.VMEM((2 * pps, 8, tq), F32)]
                       + [pltpu.VMEM((lk, tq), BF16)] * max(2, group),
        compiler_params=_cparams(("arbitrary", "arbitrary", "arbitrary")),
    )(*args)


def _rec_kernel(*refs, seq_len, has_s0, emit_state, hps):
    refs = list(refs)
    rq_ref, gf_ref, gb_ref, ri_ref, rg_ref, gain_ref = refs[:6]
    pos = 6
    s0f_ref = s0b_ref = sf_ref = sb_ref = None
    if has_s0:
        s0f_ref, s0b_ref = refs[pos:pos + 2]
        pos += 2
    rec_ref = refs[pos]
    pos += 1
    if emit_state:
        st_ref = refs[pos]
        sf_ref, sb_ref = st_ref.at[0], st_ref.at[1]
        pos += 1
    scratch = refs[pos:]
    for hh in range(hps):
        def cols(r):
            return r.at[:, pl.ds(hh * R_DK, R_DK)]

        def head(r):
            return None if r is None else r.at[hh]

        _rec_head(cols(rq_ref), cols(gf_ref), cols(gb_ref), cols(ri_ref), cols(rg_ref), gain_ref,
                  head(s0f_ref), head(s0b_ref), cols(rec_ref), head(sf_ref), head(sb_ref),
                  scratch, seq_len)


def _rec_head(rq_ref, gf_ref, gb_ref, ri_ref, rg_ref, gain_ref, s0f_ref, s0b_ref, rec_ref, sf_ref, sb_ref,
              scratch, seq_len):
    has_s0 = s0f_ref is not None
    emit_state = sf_ref is not None
    qd_scr, d_scr, u_scr, st_scr, a_scr = scratch
    nc = seq_len // CHUNK
    unroll = min(nc, 8)
    tpos = lax.broadcasted_iota(jnp.int32, (CHUNK, R_DK), 0)
    ti = lax.broadcasted_iota(jnp.int32, (CHUNK, CHUNK), 0)
    si = lax.broadcasted_iota(jnp.int32, (CHUNK, CHUNK), 1)
    nt = (((1,), (1,)), ((), ()))
    tn = (((0,), (0,)), ((), ()))

    def rows(c):
        return pl.ds(pl.multiple_of(c * CHUNK, CHUNK), CHUNK)

    def chunk_cumsum(g, reverse):
        b = g
        for s in (1, 2, 4):
            if reverse:
                sh = pltpu.roll(b, CHUNK - s, 0)
                b = b + jnp.where(tpos < CHUNK - s, sh, 0.0)
            else:
                sh = pltpu.roll(b, s, 0)
                b = b + jnp.where(tpos >= s, sh, 0.0)
        for s in (8, 16, 32):
            if reverse:
                b = jnp.concatenate([b[:CHUNK - s] + b[s:], b[CHUNK - s:]], axis=0)
            else:
                b = jnp.concatenate([b[:s], b[s:] + b[:CHUNK - s]], axis=0)
        return b

    def ua_body(c, carry):
        r = rows(c)
        rq = rq_ref[r, :].astype(F32)
        a = None
        ke = []
        for d, g_ref in enumerate((gf_ref, gb_ref)):
            g = g_ref[r, :]
            kk = 1.0 - jnp.exp2(g)
            b = chunk_cumsum(g, reverse=(d == 1))
            etot = jnp.broadcast_to(jnp.exp2(jnp.sum(g, axis=0, keepdims=True)), (CHUNK, R_DK))
            eb = jnp.exp2(b)
            qd = (rq * eb).astype(BF16)
            kd = kk / eb
            ke.append((kd * etot).astype(BF16))
            qd_scr[r, d * R_DK:(d + 1) * R_DK] = qd
            d_scr[r, d * R_DK:(d + 1) * R_DK] = etot
            sc = lax.dot_general(qd, kd.astype(BF16), nt, preferred_element_type=F32)
            sc = jnp.where(si <= ti, sc, 0.0) if d == 0 else jnp.where(si >= ti, sc, 0.0)
            a = sc if a is None else a + sc
        a_scr[r, :] = a.astype(BF16)
        u_scr[c] = lax.dot_general(ri_ref[r, :], jnp.concatenate(ke, axis=1), tn, preferred_element_type=F32)
        return carry

    lax.fori_loop(0, nc, ua_body, 0, unroll=unroll)

    if has_s0:
        sf0 = s0f_ref[...].T
        sb0 = s0b_ref[...].T
    else:
        sf0 = jnp.zeros((R_DK, R_DK), F32)
        sb0 = jnp.zeros((R_DK, R_DK), F32)

    def scan_body(i, carry):
        sf, sb = carry
        cb = nc - 1 - i
        st_scr[i, :, 0:R_DK] = sf.astype(BF16)
        st_scr[cb, :, R_DK:2 * R_DK] = sb.astype(BF16)
        df = d_scr[pl.ds(pl.multiple_of(i * CHUNK, CHUNK), 1), 0:R_DK]
        db = d_scr[pl.ds(pl.multiple_of(cb * CHUNK, CHUNK), 1), R_DK:2 * R_DK]
        sf = df * sf + u_scr[i, :, 0:R_DK]
        sb = db * sb + u_scr[cb, :, R_DK:2 * R_DK]
        return sf, sb

    sf, sb = lax.fori_loop(0, nc, scan_body, (sf0, sb0))
    if emit_state:
        sf_ref[...] = sf.T
        sb_ref[...] = sb.T

    gain = gain_ref[...]

    def o_body(c, carry):
        r = rows(c)
        o = jnp.dot(a_scr[r, :], ri_ref[r, :], preferred_element_type=F32)
        o = o + lax.dot_general(qd_scr[r, :], st_scr[c], nt, preferred_element_type=F32)
        ms = jnp.mean(o * o, axis=-1, keepdims=True)
        y = o * lax.rsqrt(ms + EPS) * gain * rg_ref[r, :].astype(F32)
        rec_ref[r, :] = y.astype(BF16)
        return carry

    lax.fori_loop(0, nc, o_body, 0, unroll=unroll)


def _recurrence(rq, g, ri, rg, gain, s0, batch, seq_len, emit_state, hps):
    t = rq.shape[0]
    nc = seq_len // CHUNK
    has_s0 = s0 is not None
    hb = R_HEADS // hps

    def col(off):
        return pl.BlockSpec((seq_len, hps * R_DK), lambda b, h: (b, h + off))

    def state(nblk, off):
        return pl.BlockSpec((hps, R_DK, R_DK), lambda b, h: (b * nblk + off + h, 0, 0))

    in_specs = [col(0), col(0), col(hb), col(0), col(0), _resident((1, R_DK))]
    args = [rq, g, g, ri, rg, gain]
    if has_s0:
        in_specs += [state(2 * hb, 0), state(2 * hb, hb)]
        args += [s0, s0]
    out_shape = [jax.ShapeDtypeStruct((t, 1024), BF16)]
    out_specs = [col(0)]
    if emit_state:
        assert hps == R_HEADS
        out_shape += [jax.ShapeDtypeStruct((batch, 2, R_HEADS, R_DK, R_DK), F32)]
        out_specs += [pl.BlockSpec((None, 2, R_HEADS, R_DK, R_DK), lambda b, h: (b, 0, 0, 0, 0))]
    scratch = [pltpu.VMEM((seq_len, 2 * R_DK), BF16),
               pltpu.VMEM((seq_len, 2 * R_DK), F32),
               pltpu.VMEM((nc, R_DK, 2 * R_DK), F32),
               pltpu.VMEM((nc, R_DK, 2 * R_DK), BF16),
               pltpu.VMEM((seq_len, CHUNK), BF16)]
    return pl.pallas_call(
        functools.partial(_rec_kernel, seq_len=seq_len, has_s0=has_s0, emit_state=emit_state, hps=hps),
        name="rec_s0" if has_s0 else "rec",
        grid=(batch, hb),
        in_specs=in_specs, out_specs=out_specs, out_shape=out_shape,
        scratch_shapes=scratch,
        compiler_params=_cparams(("arbitrary", "arbitrary")),
    )(*args)


def _mffn_kernel(x_ref, a_ref, r_ref, g_ref, xp_ref, ap_ref, rp_ref, gp_ref, xn_ref, an_ref, rn_ref, gn_ref,
                 wo_ref, n2_ref, wup_ref, cw_ref, cb_ref, wdn_ref, mod_ref, fn_ref,
                 y_ref, hbuf, act_scr, x1_scr, *, tiles_per_seq, ffc):
    tm = x_ref.shape[0]
    ts = pl.program_id(0) % tiles_per_seq

    def merge_rows(x, a, r, g):
        m = g[:, 0:D_MODEL].astype(F32) * a.astype(F32) + g[:, D_MODEL:2 * D_MODEL].astype(F32) * r.astype(F32)
        out = jnp.dot(m.astype(BF16), wo_ref[...], preferred_element_type=F32)
        x1 = x + mod_ref[2:3, :] * out
        ms = jnp.mean(x1 * x1, axis=-1, keepdims=True)
        h2 = x1 * lax.rsqrt(ms + EPS) * n2_ref[...]
        h2 = h2 * (1.0 + mod_ref[4:5, :]) + mod_ref[3:4, :]
        return x1, h2.astype(BF16)

    x1, h2 = merge_rows(x_ref[...], a_ref[...], r_ref[...], g_ref[...])
    x1_scr[...] = x1
    hbuf[HALO:HALO + tm, :] = h2
    _, hh = merge_rows(jnp.concatenate([xp_ref[...], xn_ref[...]], axis=0),
                       jnp.concatenate([ap_ref[...], an_ref[...]], axis=0),
                       jnp.concatenate([rp_ref[...], rn_ref[...]], axis=0),
                       jnp.concatenate([gp_ref[...], gn_ref[...]], axis=0))
    zero = jnp.zeros((HALO, D_MODEL), BF16)
    hbuf[0:HALO, :] = jnp.where(ts == 0, zero, hh[0:HALO, :])
    hbuf[HALO + tm:HALO + tm + HALO, :] = jnp.where(ts == tiles_per_seq - 1, zero, hh[HALO:2 * HALO, :])
    hb = hbuf[...]
    mrows = tm + 2 * HALO

    def conv(col):
        u = jnp.dot(hb, wup_ref[:, col:col + ffc], preferred_element_type=F32)
        up = pltpu.roll(u, 1, 0)[HALO:HALO + tm, :]
        un = pltpu.roll(u, mrows - 1, 0)[HALO:HALO + tm, :]
        uc = u[HALO:HALO + tm, :]
        w = cw_ref[:, col:col + ffc]
        return up * w[0:1, :] + uc * w[1:2, :] + un * w[2:3, :] + cb_ref[:, col:col + ffc]

    for c0 in range(0, D_FF, ffc):
        a = conv(c0)
        b = conv(D_FF + c0)
        act_scr[:, c0:c0 + ffc] = (a * _sigmoid(a) * b).astype(BF16)

    f = jnp.dot(act_scr[...], wdn_ref[...], preferred_element_type=F32)
    x2 = x1_scr[...] + mod_ref[5:6, :] * f
    ms = jnp.mean(x2 * x2, axis=-1, keepdims=True)
    y_ref[...] = x2 * lax.rsqrt(ms + EPS) * fn_ref[...]


def _merge_ffn(x2d, attn, rec, gates, w, mods, mod_row, seq_len, tm):
    t, d = x2d.shape
    tiles_per_seq = seq_len // tm
    hb = tm // HALO
    n_hb = t // HALO

    def main(width):
        return pl.BlockSpec((tm, width), lambda i: (i, 0))

    def prev(width):
        return pl.BlockSpec((HALO, width), lambda i: (jnp.maximum(i * hb - 1, 0), 0))

    def nxt(width):
        return pl.BlockSpec((HALO, width), lambda i: (jnp.minimum((i + 1) * hb, n_hb - 1), 0))

    toks = [x2d, attn, rec, gates]
    widths = [d, d, d, 2 * d]
    return pl.pallas_call(
        functools.partial(_mffn_kernel, tiles_per_seq=tiles_per_seq, ffc=256),
        name="merge_ffn",
        grid=(t // tm,),
        in_specs=([main(wd) for wd in widths] + [prev(wd) for wd in widths] + [nxt(wd) for wd in widths]
                  + [_resident((d, d)), _resident((1, d)),
                     _resident((d, 2 * D_FF)), _resident((3, 2 * D_FF)), _resident((1, 2 * D_FF)),
                     _resident((D_FF, d)),
                     pl.BlockSpec((None, 6, d), lambda i: (mod_row(i // tiles_per_seq), 0, 0)),
                     _resident((1, d))]),
        out_specs=pl.BlockSpec((tm, d), lambda i: (i, 0)),
        out_shape=jax.ShapeDtypeStruct((t, d), F32),
        scratch_shapes=[pltpu.VMEM((tm + 2 * HALO, d), BF16),
                        pltpu.VMEM((tm, D_FF), BF16),
                        pltpu.VMEM((tm, d), F32)],
        compiler_params=_cparams(("arbitrary",)),
    )(*toks, *toks, *toks, w["w_o"], w["n2"], w["w_up"], w["conv_w"], w["conv_b"], w["w_down"], mods, w["fnorm"])


def _rope_tables(n_tokens):
    rows = n_tokens // GRID_W
    row = jnp.repeat(jnp.arange(rows, dtype=F32), GRID_W)
    colp = jnp.tile(jnp.arange(GRID_W, dtype=F32), rows)
    half = HEAD_DIM // 2
    inv_freq = 1.0 / (ROPE_THETA ** (jnp.arange(0, half, 2, dtype=F32) / half))
    ar = row[:, None] * inv_freq
    ac = colp[:, None] * inv_freq
    ang = jnp.concatenate([ar, ar, ac, ac], axis=-1)
    sign = jnp.asarray(np.tile(np.repeat(np.array([-1.0, 1.0], np.float32), 16), 2))
    cs = jnp.cos(ang)
    sn = jnp.sin(ang) * sign
    return jnp.tile(cs, (1, 2)), jnp.tile(sn, (1, 2))


def _group(x, mods, mod_row, w, ctx, rope_tabs, tq, pps, agroup, hps, tm_in, tm_ffn, emit_state):
    batch, seq_len, d = x.shape
    x2d = x.reshape(batch * seq_len, d)
    cs, sn = rope_tabs if rope_tabs is not None else (None, None)
    q, k, v, rq, g, ri, rg, gates = _inproj(
        x2d, mods, mod_row, seq_len, w["n1"], w["w_in"], w["qg"], w["kg"], w["lbl"], cs, sn, tm=tm_in)
    ck, cv, s0 = ctx
    attn = _attention(q, k, v, ck, cv, batch, seq_len, tq, pps, agroup)
    rec_out = _recurrence(rq, g, ri, rg, w["rgain"], s0, batch, seq_len, emit_state, hps)
    rec = rec_out[0]
    y = _merge_ffn(x2d, attn, rec, gates, w, mods, mod_row, seq_len, tm=tm_ffn)
    return y.reshape(batch, seq_len, d), k, v, rec_out[1:]


def kernel(x_prompt, x_sample, c, cache_k, cache_v, state_hgrn, c_ctx, ada_w, ada_b, norm1, norm2,
           w_in, q_norm, k_norm, hgrn_lb_logits, hgrn_norm, w_o, w_up, conv_w, conv_b, w_down, final_norm):
    bp, lp, d = x_prompt.shape
    bs, ls, _ = x_sample.shape
    n_ctx = cache_k.shape[2]

    cond = jnp.zeros((16, d), F32).at[0].set(c_ctx).at[1:1 + bs].set(c)
    mods = _mods(cond, ada_w[0], ada_b[0]).reshape(16, 6, d)

    w = dict(
        n1=norm1[0].reshape(1, d), n2=norm2[0].reshape(1, d), fnorm=final_norm.reshape(1, d),
        w_in=w_in[0].astype(BF16), w_o=w_o[0].astype(BF16),
        w_up=w_up[0].astype(BF16), w_down=w_down[0].astype(BF16),
        conv_w=conv_w[0], conv_b=conv_b[0].reshape(1, 2 * D_FF),
        qg=jnp.tile(q_norm[0], N_HEADS).reshape(1, 1024),
        kg=jnp.tile(k_norm[0], N_KV_HEADS).reshape(1, KV_W),
        lbl=hgrn_lb_logits.reshape(4, 1024),
        rgain=hgrn_norm[0].reshape(1, R_DK),
    )

    y_p, k_p, v_p, st = _group(x_prompt, mods, lambda b: 0, w, (None, None, None), None,
                               tq=lp, pps=8, agroup=8, hps=R_HEADS, tm_in=lp, tm_ffn=lp, emit_state=True)
    ctx = (cache_k[:, 0].reshape(bs, n_ctx, KV_W), cache_v[:, 0].reshape(bs, n_ctx, KV_W),
           state_hgrn[:, 0].reshape(bs * 2 * R_HEADS, R_DK, R_DK))
    y_s, _, _, _ = _group(x_sample, mods, lambda b: b + 1, w, ctx, _rope_tables(ls),
                          tq=512, pps=4, agroup=1, hps=1, tm_in=512, tm_ffn=512, emit_state=False)

    new_state = st[0][:, None]
    new_k = k_p.reshape(bp, 1, lp, N_KV_HEADS, HEAD_DIM)
    new_v = v_p.reshape(bp, 1, lp, N_KV_HEADS, HEAD_DIM)
    return (y_p, y_s, new_k, new_v, new_state)
```

```python
import functools

import jax
import jax.numpy as jnp
import numpy as np
from jax import lax
from jax.experimental import pallas as pl
from jax.experimental.pallas import tpu as pltpu

F32 = jnp.float32
BF16 = jnp.bfloat16

D_MODEL = 1024
GRID_W = 64
HEAD_DIM = 64
N_HEADS = 16
N_KV_HEADS = 4
KV_W = N_KV_HEADS * HEAD_DIM
ROPE_THETA = 10000.0
R_DK = 128
R_HEADS = 8
CHUNK = 64
D_FF = 2816
EPS = 1e-6
LOG2E = 1.4426950408889634
IN_W = 8704

LANES = 128
HALO = 16
VT_ROWS = HEAD_DIM + 16
SM_ROWS = 32
SHIFT_SLACK = 100.0
VMEM_LIMIT = 56 * 1024 * 1024

_Q0, _K0, _V0, _RQ0, _RFF0, _RFB0, _RI0, _RG0, _ZA0, _ZR0 = (
    0, 1024, 1280, 1536, 2560, 3584, 4608, 5632, 6656, 7680)


def _sigmoid(x):
    return 1.0 / (1.0 + jnp.exp(-x))


def _cparams(sem):
    return pltpu.CompilerParams(dimension_semantics=sem, vmem_limit_bytes=VMEM_LIMIT)


def _resident(shape):
    nd = len(shape)
    return pl.BlockSpec(shape, lambda *_: (0,) * nd, pipeline_mode=pl.Buffered(1))


def _mod_kernel(c_ref, w_ref, b_ref, o_ref):
    c = c_ref[...]
    s = c * _sigmoid(c)
    o_ref[...] = jnp.dot(s, w_ref[...], preferred_element_type=F32,
                         precision=lax.Precision.HIGHEST) + b_ref[...]


def _mods(cond, ada_w, ada_b):
    rows, d = cond.shape
    n = ada_w.shape[1]
    tn = 1536
    return pl.pallas_call(
        _mod_kernel,
        name="mods",
        grid=(n // tn,),
        in_specs=[pl.BlockSpec((rows, d), lambda j: (0, 0)),
                  pl.BlockSpec((d, tn), lambda j: (0, j)),
                  pl.BlockSpec((1, tn), lambda j: (0, j))],
        out_specs=pl.BlockSpec((rows, tn), lambda j: (0, j)),
        out_shape=jax.ShapeDtypeStruct((rows, n), F32),
        compiler_params=_cparams(("arbitrary",)),
    )(cond, ada_w, ada_b.reshape(1, n))


def _inproj_kernel(*refs, rope):
    if rope:
        (x_ref, mod_ref, n1_ref, w_ref, qg_ref, kg_ref, lbl_ref, cs_ref, sn_ref,
         q_ref, k_ref, v_ref, rq_ref, g_ref, ri_ref, rg_ref, gt_ref) = refs
    else:
        (x_ref, mod_ref, n1_ref, w_ref, qg_ref, kg_ref, lbl_ref,
         q_ref, k_ref, v_ref, rq_ref, g_ref, ri_ref, rg_ref, gt_ref) = refs
        cs_ref = sn_ref = None
    tm = x_ref.shape[0]

    x = x_ref[...]
    ms = jnp.mean(x * x, axis=-1, keepdims=True)
    h = x * lax.rsqrt(ms + EPS) * n1_ref[...]
    h = h * (1.0 + mod_ref[1:2, :]) + mod_ref[0:1, :]
    hb = h.astype(BF16)

    def proj(a, b):
        return jnp.dot(hb, w_ref[:, a:b], preferred_element_type=F32)

    gr = lax.broadcasted_iota(jnp.int32, (LANES, LANES), 0) // HEAD_DIM
    gc = lax.broadcasted_iota(jnp.int32, (LANES, LANES), 1) // HEAD_DIM
    gmat = jnp.where(gr == gc, 1.0 / HEAD_DIM, 0.0).astype(BF16)

    def headnorm(a, gain):
        ss = jnp.dot((a * a).astype(BF16), gmat, preferred_element_type=F32)
        return a * lax.rsqrt(ss + EPS) * gain

    if rope:
        lane = lax.broadcasted_iota(jnp.int32, (tm, LANES), 1)
        first = (lane % 32) < 16
        cs = cs_ref[...]
        sn = sn_ref[...]

        def rot(a):
            sw = jnp.where(first, pltpu.roll(a, LANES - 16, 1), pltpu.roll(a, 16, 1))
            return a * cs + sw * sn
    else:
        def rot(a):
            return a

    scale = HEAD_DIM ** -0.5 * LOG2E
    for c0 in range(0, 1024, 512):
        acc = proj(_Q0 + c0, _Q0 + c0 + 512)
        for s in range(4):
            col = c0 + s * LANES
            a = headnorm(acc[:, s * LANES:(s + 1) * LANES], qg_ref[:, col:col + LANES] * scale)
            q_ref[:, col:col + LANES] = rot(a).astype(BF16)

    acc = proj(_K0, _K0 + 512)
    for s in range(2):
        col = s * LANES
        a = headnorm(acc[:, col:col + LANES], kg_ref[:, col:col + LANES])
        k_ref[:, col:col + LANES] = rot(a)
    v_ref[...] = acc[:, 256:512]

    rscale = R_DK ** -0.5
    for c0 in range(0, 1024, 512):
        acc = proj(_RQ0 + c0, _RQ0 + c0 + 512)
        rq_ref[:, c0:c0 + 512] = (acc * _sigmoid(acc) * rscale).astype(BF16)

    l = lbl_ref[...]
    for d in range(2):
        a0 = l[2 * d:2 * d + 1, :]
        a1 = l[2 * d + 1:2 * d + 2, :]
        mx = jnp.maximum(a0, a1)
        e0 = jnp.exp(a0 - mx)
        e1 = jnp.exp(a1 - mx)
        lb = e0 / (e0 + e1)
        for c0 in range(0, 1024, 512):
            acc = proj(_RFF0 + d * 1024 + c0, _RFF0 + d * 1024 + c0 + 512)
            lbc = lb[:, c0:c0 + 512]
            f = lbc + (1.0 - lbc) * _sigmoid(acc)
            g_ref[:, d * 1024 + c0:d * 1024 + c0 + 512] = jnp.log2(f)

    for c0 in range(0, 1024, 512):
        ri_ref[:, c0:c0 + 512] = proj(_RI0 + c0, _RI0 + c0 + 512).astype(BF16)
    for c0 in range(0, 1024, 512):
        acc = proj(_RG0 + c0, _RG0 + c0 + 512)
        rg_ref[:, c0:c0 + 512] = (acc * _sigmoid(acc)).astype(BF16)
    for c0 in range(0, 2048, 512):
        acc = proj(_ZA0 + c0, _ZA0 + c0 + 512)
        gt_ref[:, c0:c0 + 512] = _sigmoid(acc).astype(BF16)


def _inproj(x2d, mods, mod_row, seq_len, n1, w_in, qg, kg, lbl, cs, sn, tm):
    t, d = x2d.shape
    tiles_per_seq = seq_len // tm
    rope = cs is not None

    def tok(w):
        return pl.BlockSpec((tm, w), lambda i: (i, 0))

    in_specs = [tok(d),
                pl.BlockSpec((None, 6, d), lambda i: (mod_row(i // tiles_per_seq), 0, 0)),
                _resident((1, d)),
                _resident((d, IN_W)),
                _resident((1, 1024)),
                _resident((1, KV_W)),
                _resident((4, 1024))]
    args = [x2d, mods, n1, w_in, qg, kg, lbl]
    if rope:
        in_specs += [pl.BlockSpec((tm, LANES), lambda i: (i % tiles_per_seq, 0))] * 2
        args += [cs, sn]
    out_shape = [jax.ShapeDtypeStruct((t, 1024), BF16),
                 jax.ShapeDtypeStruct((t, KV_W), F32),
                 jax.ShapeDtypeStruct((t, KV_W), F32),
                 jax.ShapeDtypeStruct((t, 1024), BF16),
                 jax.ShapeDtypeStruct((t, 2048), F32),
                 jax.ShapeDtypeStruct((t, 1024), BF16),
                 jax.ShapeDtypeStruct((t, 1024), BF16),
                 jax.ShapeDtypeStruct((t, 2048), BF16)]
    out_specs = [tok(s.shape[1]) for s in out_shape]
    return pl.pallas_call(
        functools.partial(_inproj_kernel, rope=rope),
        name="inproj_rope" if rope else "inproj",
        grid=(t // tm,),
        in_specs=in_specs, out_specs=out_specs, out_shape=out_shape,
        compiler_params=_cparams(("arbitrary",)),
    )(*args)


def _attn_kernel(*refs, n_lat, n_ctx, tq, pps, kchunk, group):
    if n_ctx:
        q_ref, k_ref, v_ref, ck_ref, cv_ref, o_ref, kpad, vt, ref_scr, *p_bufs = refs
    else:
        q_ref, k_ref, v_ref, o_ref, kpad, vt, ref_scr, *p_bufs = refs
        ck_ref = cv_ref = None
    lk = n_lat + n_ctx
    first_step = jnp.logical_and(pl.program_id(1) == 0, pl.program_id(2) == 0)

    @pl.when(first_step)
    def _build():
        def place(dst, rows, ks):
            lane = lax.broadcasted_iota(jnp.int32, ks.shape, 1)
            lo = jnp.where(lane < HEAD_DIM, ks, 0.0)
            hi = jnp.where(lane >= HEAD_DIM, ks, 0.0)
            kpad[dst + 0, rows, :] = lo.astype(BF16)
            kpad[dst + 1, rows, :] = pltpu.roll(lo, HEAD_DIM, 1).astype(BF16)
            kpad[dst + 2, rows, :] = pltpu.roll(hi, HEAD_DIM, 1).astype(BF16)
            kpad[dst + 3, rows, :] = hi.astype(BF16)

        def place_v(s, cols_out, vs):
            vtr = vs.T.astype(BF16)
            vt[2 * s, 0:HEAD_DIM, cols_out] = vtr[0:HEAD_DIM, :]
            vt[2 * s + 1, 0:HEAD_DIM, cols_out] = vtr[HEAD_DIM:2 * HEAD_DIM, :]

        ones_rows = jnp.where(lax.broadcasted_iota(jnp.int32, (VT_ROWS - HEAD_DIM, lk), 0) == 0,
                              1.0, 0.0).astype(BF16)
        for kvh in range(N_KV_HEADS):
            vt[kvh, HEAD_DIM:VT_ROWS, :] = ones_rows
        for s in range(2):
            cols = slice(s * LANES, (s + 1) * LANES)
            place(4 * s, slice(0, n_lat), k_ref[:, cols])
            place_v(s, slice(0, n_lat), v_ref[:, cols])
            if n_ctx:
                place(4 * s, slice(n_lat, lk), ck_ref[:, cols])
                place_v(s, slice(n_lat, lk), cv_ref[:, cols])

    n_chunks = lk // kchunk
    pair0 = pl.program_id(2) * pps
    nt = (((1,), (1,)), ((), ()))
    units = [(j, parity) for j in range(pps) for parity in range(2)]

    def unit_operands(u):
        j, parity = units[u]
        kv = (pair0 + j) // 2
        return 2 * kv + parity, kv, q_ref[:, j * LANES:(j + 1) * LANES]

    def bcast_max(x8):
        return jnp.broadcast_to(jnp.max(x8, axis=0, keepdims=True), (8, tq))

    def exact_max():
        for u in range(len(units)):
            kidx, _, q2 = unit_operands(u)
            mx = jnp.full((8, tq), -jnp.inf, F32)
            for c in range(n_chunks):
                s = lax.dot_general(kpad[kidx, c * kchunk:(c + 1) * kchunk, :], q2, nt,
                                    preferred_element_type=F32)
                mx = jnp.maximum(mx, jnp.max(s.reshape(kchunk // 8, 8, tq), axis=0))
            ref_scr[u] = bcast_max(mx)

    def softmax_pass():
        worst = jnp.zeros((8, tq), F32)
        outs = []
        for u0 in range(0, len(units), group):
            us = range(u0, u0 + group)
            scores = [lax.dot_general(kpad[unit_operands(u)[0]], unit_operands(u)[2], nt,
                                      preferred_element_type=F32) for u in us]
            for u, s in zip(us, scores):
                p_scr = p_bufs[u % len(p_bufs)]
                r8 = ref_scr[u]
                mtot = jnp.full((8, tq), -jnp.inf, F32)
                for r0 in range(0, lk, SM_ROWS):
                    sb = s[r0:r0 + SM_ROWS, :].reshape(SM_ROWS // 8, 8, tq)
                    mtot = jnp.maximum(mtot, jnp.max(sb, axis=0))
                    p_scr[r0:r0 + SM_ROWS, :] = jnp.exp2(sb - r8).reshape(SM_ROWS, tq).astype(BF16)
                worst = jnp.maximum(worst, jnp.abs(bcast_max(mtot) - r8))
            for u in us:
                ot = jnp.dot(vt[unit_operands(u)[1]], p_bufs[u % len(p_bufs)][...],
                             preferred_element_type=F32)
                outs.append(ot[0:HEAD_DIM, :] * (1.0 / ot[HEAD_DIM:HEAD_DIM + 1, :]))
                if u % 2 == 1:
                    j = units[u][0]
                    o2t = jnp.concatenate(outs[-2:], axis=0)
                    o_ref[:, j * LANES:(j + 1) * LANES] = o2t.T.astype(BF16)
        return worst

    def attempt(state):
        n, _ = state

        @pl.when(n == 0)
        def _():
            ref_scr[...] = jnp.zeros(ref_scr.shape, F32)

        @pl.when(n == 1)
        def _():
            exact_max()

        worst = softmax_pass()
        return n + 1, (jnp.max(worst) > SHIFT_SLACK).astype(jnp.int32)

    lax.while_loop(lambda st: jnp.logical_or(st[0] == 0, jnp.logical_and(st[0] == 1, st[1] == 1)),
                   attempt, (jnp.int32(0), jnp.int32(0)))


def _attention(q, k, v, ck, cv, batch, seq_len, tq, pps, group):
    t = q.shape[0]
    n_ctx = 0 if ck is None else ck.shape[1]
    lk = seq_len + n_ctx
    n_qt = seq_len // tq
    n_pp = (N_HEADS // 2) // pps
    kchunk = min(lk, 512)
    in_specs = [pl.BlockSpec((tq, pps * LANES), lambda b, i, p: (b * n_qt + i, p)),
                pl.BlockSpec((seq_len, KV_W), lambda b, i, p: (b, 0)),
                pl.BlockSpec((seq_len, KV_W), lambda b, i, p: (b, 0))]
    args = [q, k, v]
    if n_ctx:
        in_specs += [pl.BlockSpec((None, n_ctx, KV_W), lambda b, i, p: (b, 0, 0))] * 2
        args += [ck, cv]
    return pl.pallas_call(
        functools.partial(_attn_kernel, n_lat=seq_len, n_ctx=n_ctx, tq=tq, pps=pps, kchunk=kchunk, group=group),
        name="attn_ctx" if n_ctx else "attn",
        grid=(batch, n_qt, n_pp),
        in_specs=in_specs,
        out_specs=pl.BlockSpec((tq, pps * LANES), lambda b, i, p: (b * n_qt + i, p)),
        out_shape=jax.ShapeDtypeStruct((t, 1024), BF16),
        scratch_shapes=[pltpu.VMEM((2 * N_KV_HEADS, lk, LANES), BF16),
                        pltpu.VMEM((N_KV_HEADS, VT_ROWS, lk), BF16),
                        pltpu.VMEM((2 * pps, 8, tq), F32)]
                       + [pltpu.VMEM((lk, tq), BF16)] * max(2, group),
        compiler_params=_cparams(("arbitrary", "arbitrary", "arbitrary")),
    )(*args)


def _rec_kernel(*refs, seq_len, has_s0, emit_state, hps):
    refs = list(refs)
    rq_ref, gf_ref, gb_ref, ri_ref, rg_ref, gain_ref = refs[:6]
    pos = 6
    s0f_ref = s0b_ref = sf_ref = sb_ref = None
    if has_s0:
        s0f_ref, s0b_ref = refs[pos:pos + 2]
        pos += 2
    rec_ref = refs[pos]
    pos += 1
    if emit_state:
        st_ref = refs[pos]
        sf_ref, sb_ref = st_ref.at[0], st_ref.at[1]
        pos += 1
    scratch = refs[pos:]
    for hh in range(hps):
        def cols(r):
            return r.at[:, pl.ds(hh * R_DK, R_DK)]

        def head(r):
            return None if r is None else r.at[hh]

        _rec_head(cols(rq_ref), cols(gf_ref), cols(gb_ref), cols(ri_ref), cols(rg_ref), gain_ref,
                  head(s0f_ref), head(s0b_ref), cols(rec_ref), head(sf_ref), head(sb_ref),
                  scratch, seq_len)


def _rec_head(rq_ref, gf_ref, gb_ref, ri_ref, rg_ref, gain_ref, s0f_ref, s0b_ref, rec_ref, sf_ref, sb_ref,
              scratch, seq_len):
    has_s0 = s0f_ref is not None
    emit_state = sf_ref is not None
    qd_scr, d_scr, u_scr, st_scr, a_scr = scratch
    nc = seq_len // CHUNK
    tpos = lax.broadcasted_iota(jnp.int32, (CHUNK, R_DK), 0)
    ti = lax.broadcasted_iota(jnp.int32, (CHUNK, CHUNK), 0)
    si = lax.broadcasted_iota(jnp.int32, (CHUNK, CHUNK), 1)
    nt = (((1,), (1,)), ((), ()))
    tn = (((0,), (0,)), ((), ()))

    def rows(c):
        return pl.ds(pl.multiple_of(c * CHUNK, CHUNK), CHUNK)

    def chunk_cumsum(g, reverse):
        b = g
        for s in (1, 2, 4):
            if reverse:
                sh = pltpu.roll(b, CHUNK - s, 0)
                b = b + jnp.where(tpos < CHUNK - s, sh, 0.0)
            else:
                sh = pltpu.roll(b, s, 0)
                b = b + jnp.where(tpos >= s, sh, 0.0)
        for s in (8, 16, 32):
            if reverse:
                b = jnp.concatenate([b[:CHUNK - s] + b[s:], b[CHUNK - s:]], axis=0)
            else:
                b = jnp.concatenate([b[:s], b[s:] + b[:CHUNK - s]], axis=0)
        return b

    def ua_body(c, carry):
        r = rows(c)
        rq = rq_ref[r, :].astype(F32)
        a = None
        ke = []
        for d, g_ref in enumerate((gf_ref, gb_ref)):
            g = g_ref[r, :]
            kk = 1.0 - jnp.exp2(g)
            b = chunk_cumsum(g, reverse=(d == 1))
            etot = jnp.broadcast_to(jnp.exp2(jnp.sum(g, axis=0, keepdims=True)), (CHUNK, R_DK))
            eb = jnp.exp2(b)
            qd = (rq * eb).astype(BF16)
            kd = kk / eb
            ke.append((kd * etot).astype(BF16))
            qd_scr[r, d * R_DK:(d + 1) * R_DK] = qd
            d_scr[r, d * R_DK:(d + 1) * R_DK] = etot
            sc = lax.dot_general(qd, kd.astype(BF16), nt, preferred_element_type=F32)
            sc = jnp.where(si <= ti, sc, 0.0) if d == 0 else jnp.where(si >= ti, sc, 0.0)
            a = sc if a is None else a + sc
        a_scr[r, :] = a.astype(BF16)
        u_scr[c] = lax.dot_general(ri_ref[r, :], jnp.concatenate(ke, axis=1), tn, preferred_element_type=F32)
        return carry

    lax.fori_loop(0, nc, ua_body, 0, unroll=min(nc, 16))

    if has_s0:
        sf0 = s0f_ref[...].T
        sb0 = s0b_ref[...].T
    else:
        sf0 = jnp.zeros((R_DK, R_DK), F32)
        sb0 = jnp.zeros((R_DK, R_DK), F32)

    def scan_body(i, carry):
        sf, sb = carry
        cb = nc - 1 - i
        st_scr[i, :, 0:R_DK] = sf.astype(BF16)
        st_scr[cb, :, R_DK:2 * R_DK] = sb.astype(BF16)
        df = d_scr[pl.ds(pl.multiple_of(i * CHUNK, CHUNK), 1), 0:R_DK]
        db = d_scr[pl.ds(pl.multiple_of(cb * CHUNK, CHUNK), 1), R_DK:2 * R_DK]
        sf = df * sf + u_scr[i, :, 0:R_DK]
        sb = db * sb + u_scr[cb, :, R_DK:2 * R_DK]
        return sf, sb

    sf, sb = lax.fori_loop(0, nc, scan_body, (sf0, sb0))
    if emit_state:
        sf_ref[...] = sf.T
        sb_ref[...] = sb.T

    gain = gain_ref[...]

    def o_body(c, carry):
        r = rows(c)
        o = jnp.dot(a_scr[r, :], ri_ref[r, :], preferred_element_type=F32)
        o = o + lax.dot_general(qd_scr[r, :], st_scr[c], nt, preferred_element_type=F32)
        ms = jnp.mean(o * o, axis=-1, keepdims=True)
        y = o * lax.rsqrt(ms + EPS) * gain * rg_ref[r, :].astype(F32)
        rec_ref[r, :] = y.astype(BF16)
        return carry

    lax.fori_loop(0, nc, o_body, 0, unroll=min(nc, 32))


def _recurrence(rq, g, ri, rg, gain, s0, batch, seq_len, emit_state, hps):
    t = rq.shape[0]
    nc = seq_len // CHUNK
    has_s0 = s0 is not None
    hb = R_HEADS // hps

    def col(off):
        return pl.BlockSpec((seq_len, hps * R_DK), lambda b, h: (b, h + off))

    def state(nblk, off):
        return pl.BlockSpec((hps, R_DK, R_DK), lambda b, h: (b * nblk + off + h, 0, 0))

    in_specs = [col(0), col(0), col(hb), col(0), col(0), _resident((1, R_DK))]
    args = [rq, g, g, ri, rg, gain]
    if has_s0:
        in_specs += [state(2 * hb, 0), state(2 * hb, hb)]
        args += [s0, s0]
    out_shape = [jax.ShapeDtypeStruct((t, 1024), BF16)]
    out_specs = [col(0)]
    if emit_state:
        assert hps == R_HEADS
        out_shape += [jax.ShapeDtypeStruct((batch, 2, R_HEADS, R_DK, R_DK), F32)]
        out_specs += [pl.BlockSpec((None, 2, R_HEADS, R_DK, R_DK), lambda b, h: (b, 0, 0, 0, 0))]
    scratch = [pltpu.VMEM((seq_len, 2 * R_DK), BF16),
               pltpu.VMEM((seq_len, 2 * R_DK), F32),
               pltpu.VMEM((nc, R_DK, 2 * R_DK), F32),
               pltpu.VMEM((nc, R_DK, 2 * R_DK), BF16),
               pltpu.VMEM((seq_len, CHUNK), BF16)]
    return pl.pallas_call(
        functools.partial(_rec_kernel, seq_len=seq_len, has_s0=has_s0, emit_state=emit_state, hps=hps),
        name="rec_s0" if has_s0 else "rec",
        grid=(batch, hb),
        in_specs=in_specs, out_specs=out_specs, out_shape=out_shape,
        scratch_shapes=scratch,
        compiler_params=_cparams(("arbitrary", "arbitrary")),
    )(*args)


def _mffn_kernel(x_ref, a_ref, r_ref, g_ref, xp_ref, ap_ref, rp_ref, gp_ref, xn_ref, an_ref, rn_ref, gn_ref,
                 wo_ref, n2_ref, wup_ref, cw_ref, cb_ref, wdn_ref, mod_ref, fn_ref,
                 y_ref, hbuf, act_scr, x1_scr, *, tiles_per_seq, ffc):
    tm = x_ref.shape[0]
    ts = pl.program_id(0) % tiles_per_seq

    def merge_rows(x, a, r, g):
        m = g[:, 0:D_MODEL].astype(F32) * a.astype(F32) + g[:, D_MODEL:2 * D_MODEL].astype(F32) * r.astype(F32)
        out = jnp.dot(m.astype(BF16), wo_ref[...], preferred_element_type=F32)
        x1 = x + mod_ref[2:3, :] * out
        ms = jnp.mean(x1 * x1, axis=-1, keepdims=True)
        h2 = x1 * lax.rsqrt(ms + EPS) * n2_ref[...]
        h2 = h2 * (1.0 + mod_ref[4:5, :]) + mod_ref[3:4, :]
        return x1, h2.astype(BF16)

    x1, h2 = merge_rows(x_ref[...], a_ref[...], r_ref[...], g_ref[...])
    x1_scr[...] = x1
    hbuf[HALO:HALO + tm, :] = h2
    _, hh = merge_rows(jnp.concatenate([xp_ref[...], xn_ref[...]], axis=0),
                       jnp.concatenate([ap_ref[...], an_ref[...]], axis=0),
                       jnp.concatenate([rp_ref[...], rn_ref[...]], axis=0),
                       jnp.concatenate([gp_ref[...], gn_ref[...]], axis=0))
    zero = jnp.zeros((HALO, D_MODEL), BF16)
    hbuf[0:HALO, :] = jnp.where(ts == 0, zero, hh[0:HALO, :])
    hbuf[HALO + tm:HALO + tm + HALO, :] = jnp.where(ts == tiles_per_seq - 1, zero, hh[HALO:2 * HALO, :])
    hb = hbuf[...]
    mrows = tm + 2 * HALO

    def conv(col):
        u = jnp.dot(hb, wup_ref[:, col:col + ffc], preferred_element_type=F32)
        up = pltpu.roll(u, 1, 0)[HALO:HALO + tm, :]
        un = pltpu.roll(u, mrows - 1, 0)[HALO:HALO + tm, :]
        uc = u[HALO:HALO + tm, :]
        w = cw_ref[:, col:col + ffc]
        return up * w[0:1, :] + uc * w[1:2, :] + un * w[2:3, :] + cb_ref[:, col:col + ffc]

    for c0 in range(0, D_FF, ffc):
        a = conv(c0)
        b = conv(D_FF + c0)
        act_scr[:, c0:c0 + ffc] = (a * _sigmoid(a) * b).astype(BF16)

    f = jnp.dot(act_scr[...], wdn_ref[...], preferred_element_type=F32)
    x2 = x1_scr[...] + mod_ref[5:6, :] * f
    ms = jnp.mean(x2 * x2, axis=-1, keepdims=True)
    y_ref[...] = x2 * lax.rsqrt(ms + EPS) * fn_ref[...]


def _merge_ffn(x2d, attn, rec, gates, w, mods, mod_row, seq_len, tm):
    t, d = x2d.shape
    tiles_per_seq = seq_len // tm
    hb = tm // HALO
    n_hb = t // HALO

    def main(width):
        return pl.BlockSpec((tm, width), lambda i: (i, 0))

    def prev(width):
        return pl.BlockSpec((HALO, width), lambda i: (jnp.maximum(i * hb - 1, 0), 0))

    def nxt(width):
        return pl.BlockSpec((HALO, width), lambda i: (jnp.minimum((i + 1) * hb, n_hb - 1), 0))

    toks = [x2d, attn, rec, gates]
    widths = [d, d, d, 2 * d]
    return pl.pallas_call(
        functools.partial(_mffn_kernel, tiles_per_seq=tiles_per_seq, ffc=256),
        name="merge_ffn",
        grid=(t // tm,),
        in_specs=([main(wd) for wd in widths] + [prev(wd) for wd in widths] + [nxt(wd) for wd in widths]
                  + [_resident((d, d)), _resident((1, d)),
                     _resident((d, 2 * D_FF)), _resident((3, 2 * D_FF)), _resident((1, 2 * D_FF)),
                     _resident((D_FF, d)),
                     pl.BlockSpec((None, 6, d), lambda i: (mod_row(i // tiles_per_seq), 0, 0)),
                     _resident((1, d))]),
        out_specs=pl.BlockSpec((tm, d), lambda i: (i, 0)),
        out_shape=jax.ShapeDtypeStruct((t, d), F32),
        scratch_shapes=[pltpu.VMEM((tm + 2 * HALO, d), BF16),
                        pltpu.VMEM((tm, D_FF), BF16),
                        pltpu.VMEM((tm, d), F32)],
        compiler_params=_cparams(("arbitrary",)),
    )(*toks, *toks, *toks, w["w_o"], w["n2"], w["w_up"], w["conv_w"], w["conv_b"], w["w_down"], mods, w["fnorm"])


def _rope_tables(n_tokens):
    rows = n_tokens // GRID_W
    row = jnp.repeat(jnp.arange(rows, dtype=F32), GRID_W)
    colp = jnp.tile(jnp.arange(GRID_W, dtype=F32), rows)
    half = HEAD_DIM // 2
    inv_freq = 1.0 / (ROPE_THETA ** (jnp.arange(0, half, 2, dtype=F32) / half))
    ar = row[:, None] * inv_freq
    ac = colp[:, None] * inv_freq
    ang = jnp.concatenate([ar, ar, ac, ac], axis=-1)
    sign = jnp.asarray(np.tile(np.repeat(np.array([-1.0, 1.0], np.float32), 16), 2))
    cs = jnp.cos(ang)
    sn = jnp.sin(ang) * sign
    return jnp.tile(cs, (1, 2)), jnp.tile(sn, (1, 2))


def _group(x, mods, mod_row, w, ctx, rope_tabs, tq, pps, agroup, hps, tm_in, tm_ffn, emit_state):
    batch, seq_len, d = x.shape
    x2d = x.reshape(batch * seq_len, d)
    cs, sn = rope_tabs if rope_tabs is not None else (None, None)
    q, k, v, rq, g, ri, rg, gates = _inproj(
        x2d, mods, mod_row, seq_len, w["n1"], w["w_in"], w["qg"], w["kg"], w["lbl"], cs, sn, tm=tm_in)
    ck, cv, s0 = ctx
    attn = _attention(q, k, v, ck, cv, batch, seq_len, tq, pps, agroup)
    rec_out = _recurrence(rq, g, ri, rg, w["rgain"], s0, batch, seq_len, emit_state, hps)
    rec = rec_out[0]
    y = _merge_ffn(x2d, attn, rec, gates, w, mods, mod_row, seq_len, tm=tm_ffn)
    return y.reshape(batch, seq_len, d), k, v, rec_out[1:]


def kernel(x_prompt, x_sample, c, cache_k, cache_v, state_hgrn, c_ctx, ada_w, ada_b, norm1, norm2,
           w_in, q_norm, k_norm, hgrn_lb_logits, hgrn_norm, w_o, w_up, conv_w, conv_b, w_down, final_norm):
    bp, lp, d = x_prompt.shape
    bs, ls, _ = x_sample.shape
    n_ctx = cache_k.shape[2]

    cond = jnp.zeros((16, d), F32).at[0].set(c_ctx).at[1:1 + bs].set(c)
    mods = _mods(cond, ada_w[0], ada_b[0]).reshape(16, 6, d)

    w = dict(
        n1=norm1[0].reshape(1, d), n2=norm2[0].reshape(1, d), fnorm=final_norm.reshape(1, d),
        w_in=w_in[0].astype(BF16), w_o=w_o[0].astype(BF16),
        w_up=w_up[0].astype(BF16), w_down=w_down[0].astype(BF16),
        conv_w=conv_w[0], conv_b=conv_b[0].reshape(1, 2 * D_FF),
        qg=jnp.tile(q_norm[0], N_HEADS).reshape(1, 1024),
        kg=jnp.tile(k_norm[0], N_KV_HEADS).reshape(1, KV_W),
        lbl=hgrn_lb_logits.reshape(4, 1024),
        rgain=hgrn_norm[0].reshape(1, R_DK),
    )

    y_p, k_p, v_p, st = _group(x_prompt, mods, lambda b: 0, w, (None, None, None), None,
                               tq=lp, pps=8, agroup=8, hps=R_HEADS, tm_in=lp, tm_ffn=lp, emit_state=True)
    ctx = (cache_k[:, 0].reshape(bs, n_ctx, KV_W), cache_v[:, 0].reshape(bs, n_ctx, KV_W),
           state_hgrn[:, 0].reshape(bs * 2 * R_HEADS, R_DK, R_DK))
    y_s, _, _, _ = _group(x_sample, mods, lambda b: b + 1, w, ctx, _rope_tables(ls),
                          tq=512, pps=4, agroup=1, hps=1, tm_in=512, tm_ffn=512, emit_state=False)

    new_state = st[0][:, None]
    new_k = k_p.reshape(bp, 1, lp, N_KV_HEADS, HEAD_DIM)
    new_v = v_p.reshape(bp, 1, lp, N_KV_HEADS, HEAD_DIM)
    return (y_p, y_s, new_k, new_v, new_state)
```

```python
import functools

import jax
import jax.numpy as jnp
import numpy as np
from jax import lax
from jax.experimental import pallas as pl
from jax.experimental.pallas import tpu as pltpu

F32 = jnp.float32
BF16 = jnp.bfloat16

D_MODEL = 1024
GRID_W = 64
HEAD_DIM = 64
N_HEADS = 16
N_KV_HEADS = 4
KV_W = N_KV_HEADS * HEAD_DIM
ROPE_THETA = 10000.0
R_DK = 128
R_HEADS = 8
CHUNK = 64
D_FF = 2816
EPS = 1e-6
LOG2E = 1.4426950408889634
IN_W = 8704

LANES = 128
HALO = 16
VT_ROWS = HEAD_DIM + 16
SM_ROWS = 32
SHIFT_SLACK = 100.0
VMEM_LIMIT = 56 * 1024 * 1024

_Q0, _K0, _V0, _RQ0, _RFF0, _RFB0, _RI0, _RG0, _ZA0, _ZR0 = (
    0, 1024, 1280, 1536, 2560, 3584, 4608, 5632, 6656, 7680)


def _sigmoid(x):
    return 1.0 / (1.0 + jnp.exp(-x))


def _cparams(sem):
    return pltpu.CompilerParams(dimension_semantics=sem, vmem_limit_bytes=VMEM_LIMIT)


def _resident(shape):
    nd = len(shape)
    return pl.BlockSpec(shape, lambda *_: (0,) * nd, pipeline_mode=pl.Buffered(1))


def _mod_kernel(c_ref, w_ref, b_ref, o_ref):
    c = c_ref[...]
    s = c * _sigmoid(c)
    o_ref[...] = jnp.dot(s, w_ref[...], preferred_element_type=F32,
                         precision=lax.Precision.HIGHEST) + b_ref[...]


def _mods(cond, ada_w, ada_b):
    rows, d = cond.shape
    n = ada_w.shape[1]
    tn = 1536
    return pl.pallas_call(
        _mod_kernel,
        name="mods",
        grid=(n // tn,),
        in_specs=[pl.BlockSpec((rows, d), lambda j: (0, 0)),
                  pl.BlockSpec((d, tn), lambda j: (0, j)),
                  pl.BlockSpec((1, tn), lambda j: (0, j))],
        out_specs=pl.BlockSpec((rows, tn), lambda j: (0, j)),
        out_shape=jax.ShapeDtypeStruct((rows, n), F32),
        compiler_params=_cparams(("arbitrary",)),
    )(cond, ada_w, ada_b.reshape(1, n))


def _inproj_kernel(*refs, rope, kv_t):
    if rope:
        (x_ref, mod_ref, n1_ref, w_ref, qg_ref, kg_ref, lbl_ref, cs_ref, sn_ref,
         q_ref, k_ref, v_ref, rq_ref, g_ref, ri_ref, rg_ref, gt_ref) = refs
    else:
        (x_ref, mod_ref, n1_ref, w_ref, qg_ref, kg_ref, lbl_ref,
         q_ref, k_ref, v_ref, rq_ref, g_ref, ri_ref, rg_ref, gt_ref) = refs
        cs_ref = sn_ref = None
    tm = x_ref.shape[0]

    x = x_ref[...]
    ms = jnp.mean(x * x, axis=-1, keepdims=True)
    h = x * lax.rsqrt(ms + EPS) * n1_ref[...]
    h = h * (1.0 + mod_ref[1:2, :]) + mod_ref[0:1, :]
    hb = h.astype(BF16)

    def proj(a, b):
        return jnp.dot(hb, w_ref[:, a:b], preferred_element_type=F32)

    gr = lax.broadcasted_iota(jnp.int32, (LANES, LANES), 0) // HEAD_DIM
    gc = lax.broadcasted_iota(jnp.int32, (LANES, LANES), 1) // HEAD_DIM
    gmat = jnp.where(gr == gc, 1.0 / HEAD_DIM, 0.0).astype(BF16)

    def headnorm(a, gain):
        ss = jnp.dot((a * a).astype(BF16), gmat, preferred_element_type=F32)
        return a * lax.rsqrt(ss + EPS) * gain

    if rope:
        lane = lax.broadcasted_iota(jnp.int32, (tm, LANES), 1)
        first = (lane % 32) < 16
        cs = cs_ref[...]
        sn = sn_ref[...]

        def rot(a):
            sw = jnp.where(first, pltpu.roll(a, LANES - 16, 1), pltpu.roll(a, 16, 1))
            return a * cs + sw * sn
    else:
        def rot(a):
            return a

    scale = HEAD_DIM ** -0.5 * LOG2E
    for c0 in range(0, 1024, 512):
        acc = proj(_Q0 + c0, _Q0 + c0 + 512)
        for s in range(4):
            col = c0 + s * LANES
            a = headnorm(acc[:, s * LANES:(s + 1) * LANES], qg_ref[:, col:col + LANES] * scale)
            q_ref[:, col:col + LANES] = rot(a).astype(BF16)

    acc = proj(_K0, _K0 + 512)
    for s in range(2):
        col = s * LANES
        a = rot(headnorm(acc[:, col:col + LANES], kg_ref[:, col:col + LANES]))
        vv = acc[:, KV_W + col:KV_W + col + LANES]
        if kv_t:
            k_ref[col:col + LANES, :] = a.T
            v_ref[col:col + LANES, :] = vv.T
        else:
            k_ref[:, col:col + LANES] = a
            v_ref[:, col:col + LANES] = vv

    rscale = R_DK ** -0.5
    for c0 in range(0, 1024, 512):
        acc = proj(_RQ0 + c0, _RQ0 + c0 + 512)
        rq_ref[:, c0:c0 + 512] = (acc * _sigmoid(acc) * rscale).astype(BF16)

    l = lbl_ref[...]
    for d in range(2):
        a0 = l[2 * d:2 * d + 1, :]
        a1 = l[2 * d + 1:2 * d + 2, :]
        mx = jnp.maximum(a0, a1)
        e0 = jnp.exp(a0 - mx)
        e1 = jnp.exp(a1 - mx)
        lb = e0 / (e0 + e1)
        for c0 in range(0, 1024, 512):
            acc = proj(_RFF0 + d * 1024 + c0, _RFF0 + d * 1024 + c0 + 512)
            lbc = lb[:, c0:c0 + 512]
            f = lbc + (1.0 - lbc) * _sigmoid(acc)
            g_ref[:, d * 1024 + c0:d * 1024 + c0 + 512] = jnp.log2(f)

    for c0 in range(0, 1024, 512):
        ri_ref[:, c0:c0 + 512] = proj(_RI0 + c0, _RI0 + c0 + 512).astype(BF16)
    for c0 in range(0, 1024, 512):
        acc = proj(_RG0 + c0, _RG0 + c0 + 512)
        rg_ref[:, c0:c0 + 512] = (acc * _sigmoid(acc)).astype(BF16)
    for c0 in range(0, 2048, 512):
        acc = proj(_ZA0 + c0, _ZA0 + c0 + 512)
        gt_ref[:, c0:c0 + 512] = _sigmoid(acc).astype(BF16)


def _inproj(x2d, mods, mod_row, seq_len, n1, w_in, qg, kg, lbl, cs, sn, tm, kv_t):
    t, d = x2d.shape
    tiles_per_seq = seq_len // tm
    rope = cs is not None

    def tok(w):
        return pl.BlockSpec((tm, w), lambda i: (i, 0))

    in_specs = [tok(d),
                pl.BlockSpec((None, 6, d), lambda i: (mod_row(i // tiles_per_seq), 0, 0)),
                _resident((1, d)),
                _resident((d, IN_W)),
                _resident((1, 1024)),
                _resident((1, KV_W)),
                _resident((4, 1024))]
    args = [x2d, mods, n1, w_in, qg, kg, lbl]
    if rope:
        in_specs += [pl.BlockSpec((tm, LANES), lambda i: (i % tiles_per_seq, 0))] * 2
        args += [cs, sn]
    out_shape = [jax.ShapeDtypeStruct((t, 1024), BF16),
                 jax.ShapeDtypeStruct((t, KV_W), F32),
                 jax.ShapeDtypeStruct((t, KV_W), F32),
                 jax.ShapeDtypeStruct((t, 1024), BF16),
                 jax.ShapeDtypeStruct((t, 2048), F32),
                 jax.ShapeDtypeStruct((t, 1024), BF16),
                 jax.ShapeDtypeStruct((t, 1024), BF16),
                 jax.ShapeDtypeStruct((t, 2048), BF16)]
    out_specs = [tok(s.shape[1]) for s in out_shape]
    if kv_t:
        kvt_shape = jax.ShapeDtypeStruct((t // seq_len, KV_W, seq_len), F32)
        kvt_spec = pl.BlockSpec((None, KV_W, tm), lambda i: (i // tiles_per_seq, 0, i % tiles_per_seq))
        out_shape[1:3] = [kvt_shape, kvt_shape]
        out_specs[1:3] = [kvt_spec, kvt_spec]
    return pl.pallas_call(
        functools.partial(_inproj_kernel, rope=rope, kv_t=kv_t),
        name="inproj_rope" if rope else "inproj",
        grid=(t // tm,),
        in_specs=in_specs, out_specs=out_specs, out_shape=out_shape,
        compiler_params=_cparams(("arbitrary",)),
    )(*args)


def _attn_kernel(*refs, n_lat, n_ctx, tq, pps, kchunk, group, kv_t):
    if n_ctx:
        q_ref, k_ref, v_ref, ck_ref, cv_ref, o_ref, kpad, vt, ref_scr, *p_bufs = refs
    else:
        q_ref, k_ref, v_ref, o_ref, kpad, vt, ref_scr, *p_bufs = refs
        ck_ref = cv_ref = None
    lk = n_lat + n_ctx
    first_step = jnp.logical_and(pl.program_id(1) == 0, pl.program_id(2) == 0)

    @pl.when(first_step)
    def _build():
        def place(dst, rows, ks):
            lane = lax.broadcasted_iota(jnp.int32, ks.shape, 1)
            lo = jnp.where(lane < HEAD_DIM, ks, 0.0)
            hi = jnp.where(lane >= HEAD_DIM, ks, 0.0)
            kpad[dst + 0, rows, :] = lo.astype(BF16)
            kpad[dst + 1, rows, :] = pltpu.roll(lo, HEAD_DIM, 1).astype(BF16)
            kpad[dst + 2, rows, :] = pltpu.roll(hi, HEAD_DIM, 1).astype(BF16)
            kpad[dst + 3, rows, :] = hi.astype(BF16)

        def place_v(s, cols_out, vtr):
            vt[2 * s, 0:HEAD_DIM, cols_out] = vtr[0:HEAD_DIM, :].astype(BF16)
            vt[2 * s + 1, 0:HEAD_DIM, cols_out] = vtr[HEAD_DIM:2 * HEAD_DIM, :].astype(BF16)

        ones_rows = jnp.where(lax.broadcasted_iota(jnp.int32, (VT_ROWS - HEAD_DIM, lk), 0) == 0,
                              1.0, 0.0).astype(BF16)
        for kvh in range(N_KV_HEADS):
            vt[kvh, HEAD_DIM:VT_ROWS, :] = ones_rows
        for s in range(2):
            cols = slice(s * LANES, (s + 1) * LANES)
            if kv_t:
                place(4 * s, slice(0, n_lat), k_ref[cols, :].T)
                place_v(s, slice(0, n_lat), v_ref[cols, :])
            else:
                place(4 * s, slice(0, n_lat), k_ref[:, cols])
                place_v(s, slice(0, n_lat), v_ref[:, cols].T)
            if n_ctx:
                place(4 * s, slice(n_lat, lk), ck_ref[cols, :].T)
                place_v(s, slice(n_lat, lk), cv_ref[cols, :])

    n_chunks = lk // kchunk
    pair0 = pl.program_id(2) * pps
    nt = (((1,), (1,)), ((), ()))
    units = [(j, parity) for j in range(pps) for parity in range(2)]

    def unit_operands(u):
        j, parity = units[u]
        kv = (pair0 + j) // 2
        return 2 * kv + parity, kv, q_ref[:, j * LANES:(j + 1) * LANES]

    def bcast_max(x8):
        return jnp.broadcast_to(jnp.max(x8, axis=0, keepdims=True), (8, tq))

    def exact_max():
        for u in range(len(units)):
            kidx, _, q2 = unit_operands(u)
            mx = jnp.full((8, tq), -jnp.inf, F32)
            for c in range(n_chunks):
                s = lax.dot_general(kpad[kidx, c * kchunk:(c + 1) * kchunk, :], q2, nt,
                                    preferred_element_type=F32)
                mx = jnp.maximum(mx, jnp.max(s.reshape(kchunk // 8, 8, tq), axis=0))
            ref_scr[u] = bcast_max(mx)

    def softmax_pass():
        worst = jnp.zeros((8, tq), F32)
        outs = []
        for u0 in range(0, len(units), group):
            us = range(u0, u0 + group)
            scores = [lax.dot_general(kpad[unit_operands(u)[0]], unit_operands(u)[2], nt,
                                      preferred_element_type=F32) for u in us]
            for u, s in zip(us, scores):
                p_scr = p_bufs[u % len(p_bufs)]
                r8 = ref_scr[u]
                mtot = jnp.full((8, tq), -jnp.inf, F32)
                for r0 in range(0, lk, SM_ROWS):
                    sb = s[r0:r0 + SM_ROWS, :].reshape(SM_ROWS // 8, 8, tq)
                    mtot = jnp.maximum(mtot, jnp.max(sb, axis=0))
                    p_scr[r0:r0 + SM_ROWS, :] = jnp.exp2(sb - r8).reshape(SM_ROWS, tq).astype(BF16)
                worst = jnp.maximum(worst, jnp.abs(bcast_max(mtot) - r8))
            for u in us:
                ot = jnp.dot(vt[unit_operands(u)[1]], p_bufs[u % len(p_bufs)][...],
                             preferred_element_type=F32)
                outs.append(ot[0:HEAD_DIM, :] * (1.0 / ot[HEAD_DIM:HEAD_DIM + 1, :]))
                if u % 2 == 1:
                    j = units[u][0]
                    o2t = jnp.concatenate(outs[-2:], axis=0)
                    o_ref[:, j * LANES:(j + 1) * LANES] = o2t.T.astype(BF16)
        return worst

    def attempt(state):
        n, _ = state

        @pl.when(n == 0)
        def _():
            ref_scr[...] = jnp.zeros(ref_scr.shape, F32)

        @pl.when(n == 1)
        def _():
            exact_max()

        worst = softmax_pass()
        return n + 1, (jnp.max(worst) > SHIFT_SLACK).astype(jnp.int32)

    lax.while_loop(lambda st: jnp.logical_or(st[0] == 0, jnp.logical_and(st[0] == 1, st[1] == 1)),
                   attempt, (jnp.int32(0), jnp.int32(0)))


def _attention(q, k, v, ck, cv, batch, seq_len, tq, pps, group, kv_t):
    t = q.shape[0]
    n_ctx = 0 if ck is None else ck.shape[2]
    lk = seq_len + n_ctx
    n_qt = seq_len // tq
    n_pp = (N_HEADS // 2) // pps
    kchunk = min(lk, 512)
    if kv_t:
        kv_spec = pl.BlockSpec((None, KV_W, seq_len), lambda b, i, p: (b, 0, 0))
    else:
        kv_spec = pl.BlockSpec((seq_len, KV_W), lambda b, i, p: (b, 0))
    in_specs = [pl.BlockSpec((tq, pps * LANES), lambda b, i, p: (b * n_qt + i, p)), kv_spec, kv_spec]
    args = [q, k, v]
    if n_ctx:
        in_specs += [pl.BlockSpec((None, KV_W, n_ctx), lambda b, i, p: (b, 0, 0))] * 2
        args += [ck, cv]
    return pl.pallas_call(
        functools.partial(_attn_kernel, n_lat=seq_len, n_ctx=n_ctx, tq=tq, pps=pps, kchunk=kchunk, group=group, kv_t=kv_t),
        name="attn_ctx" if n_ctx else "attn",
        grid=(batch, n_qt, n_pp),
        in_specs=in_specs,
        out_specs=pl.BlockSpec((tq, pps * LANES), lambda b, i, p: (b * n_qt + i, p)),
        out_shape=jax.ShapeDtypeStruct((t, 1024), BF16),
        scratch_shapes=[pltpu.VMEM((2 * N_KV_HEADS, lk, LANES), BF16),
                        pltpu.VMEM((N_KV_HEADS, VT_ROWS, lk), BF16),
                        pltpu.VMEM((2 * pps, 8, tq), F32)]
                       + [pltpu.VMEM((lk, tq), BF16)] * max(2, group),
        compiler_params=_cparams(("arbitrary", "arbitrary", "arbitrary")),
    )(*args)


def _rec_kernel(*refs, seq_len, has_s0, emit_state, hps):
    refs = list(refs)
    rq_ref, gf_ref, gb_ref, ri_ref, rg_ref, gain_ref = refs[:6]
    pos = 6
    s0f_ref = s0b_ref = sf_ref = sb_ref = None
    if has_s0:
        s0f_ref, s0b_ref = refs[pos:pos + 2]
        pos += 2
    rec_ref = refs[pos]
    pos += 1
    if emit_state:
        st_ref = refs[pos]
        sf_ref, sb_ref = st_ref.at[0], st_ref.at[1]
        pos += 1
    scratch = refs[pos:]
    for hh in range(hps):
        def cols(r):
            return r.at[:, pl.ds(hh * R_DK, R_DK)]

        def head(r):
            return None if r is None else r.at[hh]

        _rec_head(cols(rq_ref), cols(gf_ref), cols(gb_ref), cols(ri_ref), cols(rg_ref), gain_ref,
                  head(s0f_ref), head(s0b_ref), cols(rec_ref), head(sf_ref), head(sb_ref),
                  scratch, seq_len)


def _rec_head(rq_ref, gf_ref, gb_ref, ri_ref, rg_ref, gain_ref, s0f_ref, s0b_ref, rec_ref, sf_ref, sb_ref,
              scratch, seq_len):
    has_s0 = s0f_ref is not None
    emit_state = sf_ref is not None
    qd_scr, d_scr, u_scr, st_scr, a_scr = scratch
    nc = seq_len // CHUNK
    tpos = lax.broadcasted_iota(jnp.int32, (CHUNK, R_DK), 0)
    ti = lax.broadcasted_iota(jnp.int32, (CHUNK, CHUNK), 0)
    si = lax.broadcasted_iota(jnp.int32, (CHUNK, CHUNK), 1)
    nt = (((1,), (1,)), ((), ()))
    tn = (((0,), (0,)), ((), ()))

    def rows(c):
        return pl.ds(pl.multiple_of(c * CHUNK, CHUNK), CHUNK)

    def chunk_cumsum(g, reverse):
        b = g
        for s in (1, 2, 4):
            if reverse:
                sh = pltpu.roll(b, CHUNK - s, 0)
                b = b + jnp.where(tpos < CHUNK - s, sh, 0.0)
            else:
                sh = pltpu.roll(b, s, 0)
                b = b + jnp.where(tpos >= s, sh, 0.0)
        for s in (8, 16, 32):
            if reverse:
                b = jnp.concatenate([b[:CHUNK - s] + b[s:], b[CHUNK - s:]], axis=0)
            else:
                b = jnp.concatenate([b[:s], b[s:] + b[:CHUNK - s]], axis=0)
        return b

    def ua_body(c, carry):
        r = rows(c)
        rq = rq_ref[r, :].astype(F32)
        a = None
        ke = []
        for d, g_ref in enumerate((gf_ref, gb_ref)):
            g = g_ref[r, :]
            kk = 1.0 - jnp.exp2(g)
            b = chunk_cumsum(g, reverse=(d == 1))
            etot = jnp.broadcast_to(jnp.exp2(jnp.sum(g, axis=0, keepdims=True)), (CHUNK, R_DK))
            eb = jnp.exp2(b)
            qd = (rq * eb).astype(BF16)
            kd = kk / eb
            ke.append((kd * etot).astype(BF16))
            qd_scr[r, d * R_DK:(d + 1) * R_DK] = qd
            d_scr[r, d * R_DK:(d + 1) * R_DK] = etot
            sc = lax.dot_general(qd, kd.astype(BF16), nt, preferred_element_type=F32)
            sc = jnp.where(si <= ti, sc, 0.0) if d == 0 else jnp.where(si >= ti, sc, 0.0)
            a = sc if a is None else a + sc
        a_scr[r, :] = a.astype(BF16)
        u_scr[c] = lax.dot_general(ri_ref[r, :], jnp.concatenate(ke, axis=1), tn, preferred_element_type=F32)
        return carry

    lax.fori_loop(0, nc, ua_body, 0, unroll=min(nc, 16))

    if has_s0:
        sf0 = s0f_ref[...].T
        sb0 = s0b_ref[...].T
    else:
        sf0 = jnp.zeros((R_DK, R_DK), F32)
        sb0 = jnp.zeros((R_DK, R_DK), F32)

    def scan_body(i, carry):
        sf, sb = carry
        cb = nc - 1 - i
        st_scr[i, :, 0:R_DK] = sf.astype(BF16)
        st_scr[cb, :, R_DK:2 * R_DK] = sb.astype(BF16)
        df = d_scr[pl.ds(pl.multiple_of(i * CHUNK, CHUNK), 1), 0:R_DK]
        db = d_scr[pl.ds(pl.multiple_of(cb * CHUNK, CHUNK), 1), R_DK:2 * R_DK]
        sf = df * sf + u_scr[i, :, 0:R_DK]
        sb = db * sb + u_scr[cb, :, R_DK:2 * R_DK]
        return sf, sb

    sf, sb = lax.fori_loop(0, nc, scan_body, (sf0, sb0))
    if emit_state:
        sf_ref[...] = sf.T
        sb_ref[...] = sb.T

    gain = gain_ref[...]

    def o_body(c, carry):
        r = rows(c)
        o = jnp.dot(a_scr[r, :], ri_ref[r, :], preferred_element_type=F32)
        o = o + lax.dot_general(qd_scr[r, :], st_scr[c], nt, preferred_element_type=F32)
        ms = jnp.mean(o * o, axis=-1, keepdims=True)
        y = o * lax.rsqrt(ms + EPS) * gain * rg_ref[r, :].astype(F32)
        rec_ref[r, :] = y.astype(BF16)
        return carry

    lax.fori_loop(0, nc, o_body, 0, unroll=min(nc, 32))


def _recurrence(rq, g, ri, rg, gain, s0, batch, seq_len, emit_state, hps):
    t = rq.shape[0]
    nc = seq_len // CHUNK
    has_s0 = s0 is not None
    hb = R_HEADS // hps

    def col(off):
        return pl.BlockSpec((seq_len, hps * R_DK), lambda b, h: (b, h + off))

    def state(nblk, off):
        return pl.BlockSpec((hps, R_DK, R_DK), lambda b, h: (b * nblk + off + h, 0, 0))

    in_specs = [col(0), col(0), col(hb), col(0), col(0), _resident((1, R_DK))]
    args = [rq, g, g, ri, rg, gain]
    if has_s0:
        in_specs += [state(2 * hb, 0), state(2 * hb, hb)]
        args += [s0, s0]
    out_shape = [jax.ShapeDtypeStruct((t, 1024), BF16)]
    out_specs = [col(0)]
    if emit_state:
        assert hps == R_HEADS
        out_shape += [jax.ShapeDtypeStruct((batch, 2, R_HEADS, R_DK, R_DK), F32)]
        out_specs += [pl.BlockSpec((None, 2, R_HEADS, R_DK, R_DK), lambda b, h: (b, 0, 0, 0, 0))]
    scratch = [pltpu.VMEM((seq_len, 2 * R_DK), BF16),
               pltpu.VMEM((seq_len, 2 * R_DK), F32),
               pltpu.VMEM((nc, R_DK, 2 * R_DK), F32),
               pltpu.VMEM((nc, R_DK, 2 * R_DK), BF16),
               pltpu.VMEM((seq_len, CHUNK), BF16)]
    return pl.pallas_call(
        functools.partial(_rec_kernel, seq_len=seq_len, has_s0=has_s0, emit_state=emit_state, hps=hps),
        name="rec_s0" if has_s0 else "rec",
        grid=(batch, hb),
        in_specs=in_specs, out_specs=out_specs, out_shape=out_shape,
        scratch_shapes=scratch,
        compiler_params=_cparams(("arbitrary", "arbitrary")),
    )(*args)


def _mffn_kernel(x_ref, a_ref, r_ref, g_ref, xp_ref, ap_ref, rp_ref, gp_ref, xn_ref, an_ref, rn_ref, gn_ref,
                 wo_ref, n2_ref, wup_ref, cw_ref, cb_ref, wdn_ref, mod_ref, fn_ref,
                 y_ref, hbuf, act_scr, x1_scr, *, tiles_per_seq, ffc):
    tm = x_ref.shape[0]
    ts = pl.program_id(0) % tiles_per_seq

    def merge_rows(x, a, r, g):
        m = g[:, 0:D_MODEL].astype(F32) * a.astype(F32) + g[:, D_MODEL:2 * D_MODEL].astype(F32) * r.astype(F32)
        out = jnp.dot(m.astype(BF16), wo_ref[...], preferred_element_type=F32)
        x1 = x + mod_ref[2:3, :] * out
        ms = jnp.mean(x1 * x1, axis=-1, keepdims=True)
        h2 = x1 * lax.rsqrt(ms + EPS) * n2_ref[...]
        h2 = h2 * (1.0 + mod_ref[4:5, :]) + mod_ref[3:4, :]
        return x1, h2.astype(BF16)

    x1, h2 = merge_rows(x_ref[...], a_ref[...], r_ref[...], g_ref[...])
    x1_scr[...] = x1
    hbuf[HALO:HALO + tm, :] = h2
    _, hh = merge_rows(jnp.concatenate([xp_ref[...], xn_ref[...]], axis=0),
                       jnp.concatenate([ap_ref[...], an_ref[...]], axis=0),
                       jnp.concatenate([rp_ref[...], rn_ref[...]], axis=0),
                       jnp.concatenate([gp_ref[...], gn_ref[...]], axis=0))
    zero = jnp.zeros((HALO, D_MODEL), BF16)
    hbuf[0:HALO, :] = jnp.where(ts == 0, zero, hh[0:HALO, :])
    hbuf[HALO + tm:HALO + tm + HALO, :] = jnp.where(ts == tiles_per_seq - 1, zero, hh[HALO:2 * HALO, :])
    hb = hbuf[...]
    mrows = tm + 2 * HALO

    def conv(col):
        u = jnp.dot(hb, wup_ref[:, col:col + ffc], preferred_element_type=F32)
        up = pltpu.roll(u, 1, 0)[HALO:HALO + tm, :]
        un = pltpu.roll(u, mrows - 1, 0)[HALO:HALO + tm, :]
        uc = u[HALO:HALO + tm, :]
        w = cw_ref[:, col:col + ffc]
        return up * w[0:1, :] + uc * w[1:2, :] + un * w[2:3, :] + cb_ref[:, col:col + ffc]

    for c0 in range(0, D_FF, ffc):
        a = conv(c0)
        b = conv(D_FF + c0)
        act_scr[:, c0:c0 + ffc] = (a * _sigmoid(a) * b).astype(BF16)

    f = jnp.dot(act_scr[...], wdn_ref[...], preferred_element_type=F32)
    x2 = x1_scr[...] + mod_ref[5:6, :] * f
    ms = jnp.mean(x2 * x2, axis=-1, keepdims=True)
    y_ref[...] = x2 * lax.rsqrt(ms + EPS) * fn_ref[...]


def _merge_ffn(x2d, attn, rec, gates, w, mods, mod_row, seq_len, tm):
    t, d = x2d.shape
    tiles_per_seq = seq_len // tm
    hb = tm // HALO
    n_hb = t // HALO

    def main(width):
        return pl.BlockSpec((tm, width), lambda i: (i, 0))

    def prev(width):
        return pl.BlockSpec((HALO, width), lambda i: (jnp.maximum(i * hb - 1, 0), 0))

    def nxt(width):
        return pl.BlockSpec((HALO, width), lambda i: (jnp.minimum((i + 1) * hb, n_hb - 1), 0))

    toks = [x2d, attn, rec, gates]
    widths = [d, d, d, 2 * d]
    return pl.pallas_call(
        functools.partial(_mffn_kernel, tiles_per_seq=tiles_per_seq, ffc=256),
        name="merge_ffn",
        grid=(t // tm,),
        in_specs=([main(wd) for wd in widths] + [prev(wd) for wd in widths] + [nxt(wd) for wd in widths]
                  + [_resident((d, d)), _resident((1, d)),
                     _resident((d, 2 * D_FF)), _resident((3, 2 * D_FF)), _resident((1, 2 * D_FF)),
                     _resident((D_FF, d)),
                     pl.BlockSpec((None, 6, d), lambda i: (mod_row(i // tiles_per_seq), 0, 0)),
                     _resident((1, d))]),
        out_specs=pl.BlockSpec((tm, d), lambda i: (i, 0)),
        out_shape=jax.ShapeDtypeStruct((t, d), F32),
        scratch_shapes=[pltpu.VMEM((tm + 2 * HALO, d), BF16),
                        pltpu.VMEM((tm, D_FF), BF16),
                        pltpu.VMEM((tm, d), F32)],
        compiler_params=_cparams(("arbitrary",)),
    )(*toks, *toks, *toks, w["w_o"], w["n2"], w["w_up"], w["conv_w"], w["conv_b"], w["w_down"], mods, w["fnorm"])


def _rope_tables(n_tokens):
    rows = n_tokens // GRID_W
    row = jnp.repeat(jnp.arange(rows, dtype=F32), GRID_W)
    colp = jnp.tile(jnp.arange(GRID_W, dtype=F32), rows)
    half = HEAD_DIM // 2
    inv_freq = 1.0 / (ROPE_THETA ** (jnp.arange(0, half, 2, dtype=F32) / half))
    ar = row[:, None] * inv_freq
    ac = colp[:, None] * inv_freq
    ang = jnp.concatenate([ar, ar, ac, ac], axis=-1)
    sign = jnp.asarray(np.tile(np.repeat(np.array([-1.0, 1.0], np.float32), 16), 2))
    cs = jnp.cos(ang)
    sn = jnp.sin(ang) * sign
    return jnp.tile(cs, (1, 2)), jnp.tile(sn, (1, 2))


def _group(x, mods, mod_row, w, ctx, rope_tabs, tq, pps, agroup, hps, tm_in, tm_ffn, emit_state):
    batch, seq_len, d = x.shape
    x2d = x.reshape(batch * seq_len, d)
    cs, sn = rope_tabs if rope_tabs is not None else (None, None)
    kv_t = emit_state
    q, k, v, rq, g, ri, rg, gates = _inproj(
        x2d, mods, mod_row, seq_len, w["n1"], w["w_in"], w["qg"], w["kg"], w["lbl"], cs, sn, tm=tm_in,
        kv_t=kv_t)
    ck, cv, s0 = ctx
    attn = _attention(q, k, v, ck, cv, batch, seq_len, tq, pps, agroup, kv_t)
    rec_out = _recurrence(rq, g, ri, rg, w["rgain"], s0, batch, seq_len, emit_state, hps)
    rec = rec_out[0]
    y = _merge_ffn(x2d, attn, rec, gates, w, mods, mod_row, seq_len, tm=tm_ffn)
    return y.reshape(batch, seq_len, d), k, v, rec_out[1:]


def kernel(x_prompt, x_sample, c, cache_k, cache_v, state_hgrn, c_ctx, ada_w, ada_b, norm1, norm2,
           w_in, q_norm, k_norm, hgrn_lb_logits, hgrn_norm, w_o, w_up, conv_w, conv_b, w_down, final_norm):
    bp, lp, d = x_prompt.shape
    bs, ls, _ = x_sample.shape
    n_ctx = cache_k.shape[2]

    cond = jnp.zeros((16, d), F32).at[0].set(c_ctx).at[1:1 + bs].set(c)
    mods = _mods(cond, ada_w[0], ada_b[0]).reshape(16, 6, d)

    w = dict(
        n1=norm1[0].reshape(1, d), n2=norm2[0].reshape(1, d), fnorm=final_norm.reshape(1, d),
        w_in=w_in[0].astype(BF16), w_o=w_o[0].astype(BF16),
        w_up=w_up[0].astype(BF16), w_down=w_down[0].astype(BF16),
        conv_w=conv_w[0], conv_b=conv_b[0].reshape(1, 2 * D_FF),
        qg=jnp.tile(q_norm[0], N_HEADS).reshape(1, 1024),
        kg=jnp.tile(k_norm[0], N_KV_HEADS).reshape(1, KV_W),
        lbl=hgrn_lb_logits.reshape(4, 1024),
        rgain=hgrn_norm[0].reshape(1, R_DK),
    )

    y_p, k_p, v_p, st = _group(x_prompt, mods, lambda b: 0, w, (None, None, None), None,
                               tq=lp, pps=8, agroup=8, hps=R_HEADS, tm_in=lp, tm_ffn=lp, emit_state=True)
    def feature_major(cache):
        return jnp.transpose(cache[:, 0], (0, 2, 3, 1)).reshape(bs, KV_W, n_ctx)

    ctx = (feature_major(cache_k), feature_major(cache_v),
           state_hgrn[:, 0].reshape(bs * 2 * R_HEADS, R_DK, R_DK))
    y_s, _, _, _ = _group(x_sample, mods, lambda b: b + 1, w, ctx, _rope_tables(ls),
                          tq=512, pps=4, agroup=1, hps=1, tm_in=512, tm_ffn=512, emit_state=False)

    new_state = st[0][:, None]
    def position_major(kvt):
        return jnp.transpose(kvt.reshape(bp, N_KV_HEADS, HEAD_DIM, lp), (0, 3, 1, 2))[:, None]

    new_k = position_major(k_p)
    new_v = position_major(v_p)
    return (y_p, y_s, new_k, new_v, new_state)
```

```python
import functools

import jax
import jax.numpy as jnp
import numpy as np
from jax import lax
from jax.experimental import pallas as pl
from jax.experimental.pallas import tpu as pltpu

F32 = jnp.float32
BF16 = jnp.bfloat16

D_MODEL = 1024
GRID_W = 64
HEAD_DIM = 64
N_HEADS = 16
N_KV_HEADS = 4
KV_W = N_KV_HEADS * HEAD_DIM
ROPE_THETA = 10000.0
R_DK = 128
R_HEADS = 8
CHUNK = 64
D_FF = 2816
EPS = 1e-6
LOG2E = 1.4426950408889634
IN_W = 8704

LANES = 128
HALO = 16
VT_ROWS = HEAD_DIM + 16
SM_ROWS = 32
SHIFT_SLACK = 100.0
VMEM_LIMIT = 56 * 1024 * 1024

_Q0, _K0, _V0, _RQ0, _RFF0, _RFB0, _RI0, _RG0, _ZA0, _ZR0 = (
    0, 1024, 1280, 1536, 2560, 3584, 4608, 5632, 6656, 7680)


def _sigmoid(x):
    return 1.0 / (1.0 + jnp.exp(-x))


def _cparams(sem):
    return pltpu.CompilerParams(dimension_semantics=sem, vmem_limit_bytes=VMEM_LIMIT)


def _resident(shape):
    nd = len(shape)
    return pl.BlockSpec(shape, lambda *_: (0,) * nd, pipeline_mode=pl.Buffered(1))


def _mod_kernel(c_ref, w_ref, b_ref, o_ref):
    c = c_ref[...]
    s = c * _sigmoid(c)
    o_ref[...] = jnp.dot(s.astype(BF16), w_ref[...].astype(BF16), preferred_element_type=F32) + b_ref[...]


def _mods(cond, ada_w, ada_b):
    rows, d = cond.shape
    n = ada_w.shape[1]
    tn = 1536
    return pl.pallas_call(
        _mod_kernel,
        name="mods",
        grid=(n // tn,),
        in_specs=[pl.BlockSpec((rows, d), lambda j: (0, 0)),
                  pl.BlockSpec((d, tn), lambda j: (0, j)),
                  pl.BlockSpec((1, tn), lambda j: (0, j))],
        out_specs=pl.BlockSpec((rows, tn), lambda j: (0, j)),
        out_shape=jax.ShapeDtypeStruct((rows, n), F32),
        compiler_params=_cparams(("arbitrary",)),
    )(cond, ada_w, ada_b.reshape(1, n))


def _inproj_kernel(*refs, rope, kv_t):
    if rope:
        (x_ref, mod_ref, n1_ref, w_ref, qg_ref, kg_ref, lbl_ref, cs_ref, sn_ref,
         q_ref, k_ref, v_ref, rq_ref, g_ref, ri_ref, rg_ref, gt_ref) = refs
    else:
        (x_ref, mod_ref, n1_ref, w_ref, qg_ref, kg_ref, lbl_ref,
         q_ref, k_ref, v_ref, rq_ref, g_ref, ri_ref, rg_ref, gt_ref) = refs
        cs_ref = sn_ref = None
    tm = x_ref.shape[0]

    x = x_ref[...]
    ms = jnp.mean(x * x, axis=-1, keepdims=True)
    h = x * lax.rsqrt(ms + EPS) * n1_ref[...]
    h = h * (1.0 + mod_ref[1:2, :]) + mod_ref[0:1, :]
    hb = h.astype(BF16)

    def proj(a, b):
        return jnp.dot(hb, w_ref[:, a:b], preferred_element_type=F32)

    gr = lax.broadcasted_iota(jnp.int32, (LANES, LANES), 0) // HEAD_DIM
    gc = lax.broadcasted_iota(jnp.int32, (LANES, LANES), 1) // HEAD_DIM
    gmat = jnp.where(gr == gc, 1.0 / HEAD_DIM, 0.0).astype(BF16)

    def headnorm(a, gain):
        ss = jnp.dot((a * a).astype(BF16), gmat, preferred_element_type=F32)
        return a * lax.rsqrt(ss + EPS) * gain

    if rope:
        lane = lax.broadcasted_iota(jnp.int32, (tm, LANES), 1)
        first = (lane % 32) < 16
        cs = cs_ref[...]
        sn = sn_ref[...]

        def rot(a):
            sw = jnp.where(first, pltpu.roll(a, LANES - 16, 1), pltpu.roll(a, 16, 1))
            return a * cs + sw * sn
    else:
        def rot(a):
            return a

    scale = HEAD_DIM ** -0.5 * LOG2E
    for c0 in range(0, 1024, 512):
        acc = proj(_Q0 + c0, _Q0 + c0 + 512)
        for s in range(4):
            col = c0 + s * LANES
            a = headnorm(acc[:, s * LANES:(s + 1) * LANES], qg_ref[:, col:col + LANES] * scale)
            q_ref[:, col:col + LANES] = rot(a).astype(BF16)

    acc = proj(_K0, _K0 + 512)
    for s in range(2):
        col = s * LANES
        a = rot(headnorm(acc[:, col:col + LANES], kg_ref[:, col:col + LANES]))
        vv = acc[:, KV_W + col:KV_W + col + LANES]
        if kv_t:
            k_ref[col:col + LANES, :] = a.T
            v_ref[col:col + LANES, :] = vv.T
        else:
            k_ref[:, col:col + LANES] = a
            v_ref[:, col:col + LANES] = vv

    rscale = R_DK ** -0.5
    for c0 in range(0, 1024, 512):
        acc = proj(_RQ0 + c0, _RQ0 + c0 + 512)
        rq_ref[:, c0:c0 + 512] = (acc * _sigmoid(acc) * rscale).astype(BF16)

    l = lbl_ref[...]
    for d in range(2):
        a0 = l[2 * d:2 * d + 1, :]
        a1 = l[2 * d + 1:2 * d + 2, :]
        mx = jnp.maximum(a0, a1)
        e0 = jnp.exp(a0 - mx)
        e1 = jnp.exp(a1 - mx)
        lb = e0 / (e0 + e1)
        for c0 in range(0, 1024, 512):
            acc = proj(_RFF0 + d * 1024 + c0, _RFF0 + d * 1024 + c0 + 512)
            lbc = lb[:, c0:c0 + 512]
            f = lbc + (1.0 - lbc) * _sigmoid(acc)
            g_ref[:, d * 1024 + c0:d * 1024 + c0 + 512] = jnp.log2(f)

    for c0 in range(0, 1024, 512):
        ri_ref[:, c0:c0 + 512] = proj(_RI0 + c0, _RI0 + c0 + 512).astype(BF16)
    for c0 in range(0, 1024, 512):
        acc = proj(_RG0 + c0, _RG0 + c0 + 512)
        rg_ref[:, c0:c0 + 512] = (acc * _sigmoid(acc)).astype(BF16)
    for c0 in range(0, 2048, 512):
        acc = proj(_ZA0 + c0, _ZA0 + c0 + 512)
        gt_ref[:, c0:c0 + 512] = _sigmoid(acc).astype(BF16)


def _inproj(x2d, mods, mod_row, seq_len, n1, w_in, qg, kg, lbl, cs, sn, tm, kv_t):
    t, d = x2d.shape
    tiles_per_seq = seq_len // tm
    rope = cs is not None

    def tok(w):
        return pl.BlockSpec((tm, w), lambda i: (i, 0))

    in_specs = [tok(d),
                pl.BlockSpec((None, 6, d), lambda i: (mod_row(i // tiles_per_seq), 0, 0)),
                _resident((1, d)),
                _resident((d, IN_W)),
                _resident((1, 1024)),
                _resident((1, KV_W)),
                _resident((4, 1024))]
    args = [x2d, mods, n1, w_in, qg, kg, lbl]
    if rope:
        in_specs += [pl.BlockSpec((tm, LANES), lambda i: (i % tiles_per_seq, 0))] * 2
        args += [cs, sn]
    out_shape = [jax.ShapeDtypeStruct((t, 1024), BF16),
                 jax.ShapeDtypeStruct((t, KV_W), F32),
                 jax.ShapeDtypeStruct((t, KV_W), F32),
                 jax.ShapeDtypeStruct((t, 1024), BF16),
                 jax.ShapeDtypeStruct((t, 2048), F32),
                 jax.ShapeDtypeStruct((t, 1024), BF16),
                 jax.ShapeDtypeStruct((t, 1024), BF16),
                 jax.ShapeDtypeStruct((t, 2048), BF16)]
    out_specs = [tok(s.shape[1]) for s in out_shape]
    if kv_t:
        kvt_shape = jax.ShapeDtypeStruct((t // seq_len, KV_W, seq_len), F32)
        kvt_spec = pl.BlockSpec((None, KV_W, tm), lambda i: (i // tiles_per_seq, 0, i % tiles_per_seq))
        out_shape[1:3] = [kvt_shape, kvt_shape]
        out_specs[1:3] = [kvt_spec, kvt_spec]
    return pl.pallas_call(
        functools.partial(_inproj_kernel, rope=rope, kv_t=kv_t),
        name="inproj_rope" if rope else "inproj",
        grid=(t // tm,),
        in_specs=in_specs, out_specs=out_specs, out_shape=out_shape,
        compiler_params=_cparams(("arbitrary",)),
    )(*args)


def _attn_kernel(*refs, n_lat, n_ctx, tq, pps, kchunk, group, kv_t):
    if n_ctx:
        q_ref, k_ref, v_ref, ck_ref, cv_ref, o_ref, kpad, vt, ref_scr, *p_bufs = refs
    else:
        q_ref, k_ref, v_ref, o_ref, kpad, vt, ref_scr, *p_bufs = refs
        ck_ref = cv_ref = None
    lk = n_lat + n_ctx
    first_step = jnp.logical_and(pl.program_id(1) == 0, pl.program_id(2) == 0)

    @pl.when(first_step)
    def _build():
        def place(dst, rows, ks):
            lane = lax.broadcasted_iota(jnp.int32, ks.shape, 1)
            lo = jnp.where(lane < HEAD_DIM, ks, 0.0)
            hi = jnp.where(lane >= HEAD_DIM, ks, 0.0)
            kpad[dst + 0, rows, :] = lo.astype(BF16)
            kpad[dst + 1, rows, :] = pltpu.roll(lo, HEAD_DIM, 1).astype(BF16)
            kpad[dst + 2, rows, :] = pltpu.roll(hi, HEAD_DIM, 1).astype(BF16)
            kpad[dst + 3, rows, :] = hi.astype(BF16)

        def place_v(s, cols_out, vtr):
            vt[2 * s, 0:HEAD_DIM, cols_out] = vtr[0:HEAD_DIM, :].astype(BF16)
            vt[2 * s + 1, 0:HEAD_DIM, cols_out] = vtr[HEAD_DIM:2 * HEAD_DIM, :].astype(BF16)

        ones_rows = jnp.where(lax.broadcasted_iota(jnp.int32, (VT_ROWS - HEAD_DIM, lk), 0) == 0,
                              1.0, 0.0).astype(BF16)
        for kvh in range(N_KV_HEADS):
            vt[kvh, HEAD_DIM:VT_ROWS, :] = ones_rows
        for s in range(2):
            cols = slice(s * LANES, (s + 1) * LANES)
            if kv_t:
                place(4 * s, slice(0, n_lat), k_ref[cols, :].T)
                place_v(s, slice(0, n_lat), v_ref[cols, :])
            else:
                place(4 * s, slice(0, n_lat), k_ref[:, cols])
                place_v(s, slice(0, n_lat), v_ref[:, cols].T)
            if n_ctx:
                place(4 * s, slice(n_lat, lk), ck_ref[cols, :].T)
                place_v(s, slice(n_lat, lk), cv_ref[cols, :])

    n_chunks = lk // kchunk
    pair0 = pl.program_id(2) * pps
    nt = (((1,), (1,)), ((), ()))
    units = [(j, parity) for j in range(pps) for parity in range(2)]

    def unit_operands(u):
        j, parity = units[u]
        kv = (pair0 + j) // 2
        return 2 * kv + parity, kv, q_ref[:, j * LANES:(j + 1) * LANES]

    def bcast_max(x8):
        return jnp.broadcast_to(jnp.max(x8, axis=0, keepdims=True), (8, tq))

    def exact_max():
        for u in range(len(units)):
            kidx, _, q2 = unit_operands(u)
            mx = jnp.full((8, tq), -jnp.inf, F32)
            for c in range(n_chunks):
                s = lax.dot_general(kpad[kidx, c * kchunk:(c + 1) * kchunk, :], q2, nt,
                                    preferred_element_type=F32)
                mx = jnp.maximum(mx, jnp.max(s.reshape(kchunk // 8, 8, tq), axis=0))
            ref_scr[u] = bcast_max(mx)

    def softmax_pass():
        worst = jnp.zeros((8, tq), F32)
        outs = []
        for u0 in range(0, len(units), group):
            us = range(u0, u0 + group)
            scores = [lax.dot_general(kpad[unit_operands(u)[0]], unit_operands(u)[2], nt,
                                      preferred_element_type=F32) for u in us]
            for u, s in zip(us, scores):
                p_scr = p_bufs[u % len(p_bufs)]
                r8 = ref_scr[u]
                mtot = jnp.full((8, tq), -jnp.inf, F32)
                for r0 in range(0, lk, SM_ROWS):
                    sb = s[r0:r0 + SM_ROWS, :].reshape(SM_ROWS // 8, 8, tq)
                    mtot = jnp.maximum(mtot, jnp.max(sb, axis=0))
                    p_scr[r0:r0 + SM_ROWS, :] = jnp.exp2(sb - r8).reshape(SM_ROWS, tq).astype(BF16)
                worst = jnp.maximum(worst, jnp.abs(bcast_max(mtot) - r8))
            for u in us:
                ot = jnp.dot(vt[unit_operands(u)[1]], p_bufs[u % len(p_bufs)][...],
                             preferred_element_type=F32)
                outs.append(ot[0:HEAD_DIM, :] * (1.0 / ot[HEAD_DIM:HEAD_DIM + 1, :]))
                if u % 2 == 1:
                    j = units[u][0]
                    o2t = jnp.concatenate(outs[-2:], axis=0)
                    o_ref[:, j * LANES:(j + 1) * LANES] = o2t.T.astype(BF16)
        return worst

    def attempt(state):
        n, _ = state

        @pl.when(n == 0)
        def _():
            ref_scr[...] = jnp.zeros(ref_scr.shape, F32)

        @pl.when(n == 1)
        def _():
            exact_max()

        worst = softmax_pass()
        return n + 1, (jnp.max(worst) > SHIFT_SLACK).astype(jnp.int32)

    lax.while_loop(lambda st: jnp.logical_or(st[0] == 0, jnp.logical_and(st[0] == 1, st[1] == 1)),
                   attempt, (jnp.int32(0), jnp.int32(0)))


def _attention(q, k, v, ck, cv, batch, seq_len, tq, pps, group, kv_t):
    t = q.shape[0]
    n_ctx = 0 if ck is None else ck.shape[2]
    lk = seq_len + n_ctx
    n_qt = seq_len // tq
    n_pp = (N_HEADS // 2) // pps
    kchunk = min(lk, 512)
    if kv_t:
        kv_spec = pl.BlockSpec((None, KV_W, seq_len), lambda b, i, p: (b, 0, 0))
    else:
        kv_spec = pl.BlockSpec((seq_len, KV_W), lambda b, i, p: (b, 0))
    in_specs = [pl.BlockSpec((tq, pps * LANES), lambda b, i, p: (b * n_qt + i, p)), kv_spec, kv_spec]
    args = [q, k, v]
    if n_ctx:
        in_specs += [pl.BlockSpec((None, KV_W, n_ctx), lambda b, i, p: (b, 0, 0))] * 2
        args += [ck, cv]
    return pl.pallas_call(
        functools.partial(_attn_kernel, n_lat=seq_len, n_ctx=n_ctx, tq=tq, pps=pps, kchunk=kchunk, group=group, kv_t=kv_t),
        name="attn_ctx" if n_ctx else "attn",
        grid=(batch, n_qt, n_pp),
        in_specs=in_specs,
        out_specs=pl.BlockSpec((tq, pps * LANES), lambda b, i, p: (b * n_qt + i, p)),
        out_shape=jax.ShapeDtypeStruct((t, 1024), BF16),
        scratch_shapes=[pltpu.VMEM((2 * N_KV_HEADS, lk, LANES), BF16),
                        pltpu.VMEM((N_KV_HEADS, VT_ROWS, lk), BF16),
                        pltpu.VMEM((2 * pps, 8, tq), F32)]
                       + [pltpu.VMEM((lk, tq), BF16)] * max(2, group),
        compiler_params=_cparams(("arbitrary", "arbitrary", "arbitrary")),
    )(*args)


def _rec_kernel(*refs, seq_len, has_s0, emit_state, hps):
    refs = list(refs)
    rq_ref, gf_ref, gb_ref, ri_ref, rg_ref, gain_ref = refs[:6]
    pos = 6
    s0f_ref = s0b_ref = sf_ref = sb_ref = None
    if has_s0:
        s0f_ref, s0b_ref = refs[pos:pos + 2]
        pos += 2
    rec_ref = refs[pos]
    pos += 1
    if emit_state:
        st_ref = refs[pos]
        sf_ref, sb_ref = st_ref.at[0], st_ref.at[1]
        pos += 1
    qd_scr, d_scr, u_scr, st_scr, a_scr = refs[pos:]

    nc = seq_len // CHUNK
    tpos = lax.broadcasted_iota(jnp.int32, (CHUNK, R_DK), 0)
    ti = lax.broadcasted_iota(jnp.int32, (CHUNK, CHUNK), 0)
    si = lax.broadcasted_iota(jnp.int32, (CHUNK, CHUNK), 1)
    nt = (((1,), (1,)), ((), ()))
    tn = (((0,), (0,)), ((), ()))
    gain = gain_ref[...]

    def cols(r, hh):
        return r.at[:, pl.ds(hh * R_DK, R_DK)]

    def rows(c):
        return pl.ds(pl.multiple_of(c * CHUNK, CHUNK), CHUNK)

    def chunk_cumsum(g, reverse):
        b = g
        for s in (1, 2, 4):
            if reverse:
                sh = pltpu.roll(b, CHUNK - s, 0)
                b = b + jnp.where(tpos < CHUNK - s, sh, 0.0)
            else:
                sh = pltpu.roll(b, s, 0)
                b = b + jnp.where(tpos >= s, sh, 0.0)
        for s in (8, 16, 32):
            if reverse:
                b = jnp.concatenate([b[:CHUNK - s] + b[s:], b[CHUNK - s:]], axis=0)
            else:
                b = jnp.concatenate([b[:s], b[s:] + b[:CHUNK - s]], axis=0)
        return b

    def ua_chunk(c, hh):
        r = rows(c)
        rq = cols(rq_ref, hh)[r, :].astype(F32)
        a = None
        ke = []
        for d, g_ref in enumerate((gf_ref, gb_ref)):
            g = cols(g_ref, hh)[r, :]
            kk = 1.0 - jnp.exp2(g)
            b = chunk_cumsum(g, reverse=(d == 1))
            etot = jnp.broadcast_to(jnp.exp2(jnp.sum(g, axis=0, keepdims=True)), (CHUNK, R_DK))
            eb = jnp.exp2(b)
            qd = (rq * eb).astype(BF16)
            kd = kk / eb
            ke.append((kd * etot).astype(BF16))
            qd_scr[hh, r, d * R_DK:(d + 1) * R_DK] = qd
            d_scr[hh, r, d * R_DK:(d + 1) * R_DK] = etot
            sc = lax.dot_general(qd, kd.astype(BF16), nt, preferred_element_type=F32)
            sc = jnp.where(si <= ti, sc, 0.0) if d == 0 else jnp.where(si >= ti, sc, 0.0)
            a = sc if a is None else a + sc
        a_scr[hh, r, :] = a.astype(BF16)
        u_scr[hh, c] = lax.dot_general(cols(ri_ref, hh)[r, :], jnp.concatenate(ke, axis=1), tn,
                                       preferred_element_type=F32)

    def ua_body(c, carry):
        for hh in range(hps):
            ua_chunk(c, hh)
        return carry

    lax.fori_loop(0, nc, ua_body, 0, unroll=min(nc, 16))

    for hh in range(hps):
        if has_s0:
            sf0 = s0f_ref[hh].T
            sb0 = s0b_ref[hh].T
        else:
            sf0 = jnp.zeros((R_DK, R_DK), F32)
            sb0 = jnp.zeros((R_DK, R_DK), F32)

        def scan_body(i, carry, hh=hh):
            sf, sb = carry
            cb = nc - 1 - i
            st_scr[hh, i, :, 0:R_DK] = sf.astype(BF16)
            st_scr[hh, cb, :, R_DK:2 * R_DK] = sb.astype(BF16)
            df = d_scr[hh, pl.ds(pl.multiple_of(i * CHUNK, CHUNK), 1), 0:R_DK]
            db = d_scr[hh, pl.ds(pl.multiple_of(cb * CHUNK, CHUNK), 1), R_DK:2 * R_DK]
            sf = df * sf + u_scr[hh, i, :, 0:R_DK]
            sb = db * sb + u_scr[hh, cb, :, R_DK:2 * R_DK]
            return sf, sb

        sf, sb = lax.fori_loop(0, nc, scan_body, (sf0, sb0))
        if emit_state:
            sf_ref[hh] = sf.T
            sb_ref[hh] = sb.T

    def o_body(c, carry):
        r = rows(c)
        for hh in range(hps):
            o = jnp.dot(a_scr[hh, r, :], cols(ri_ref, hh)[r, :], preferred_element_type=F32)
            o = o + lax.dot_general(qd_scr[hh, r, :], st_scr[hh, c], nt, preferred_element_type=F32)
            ms = jnp.mean(o * o, axis=-1, keepdims=True)
            y = o * lax.rsqrt(ms + EPS) * gain * cols(rg_ref, hh)[r, :].astype(F32)
            cols(rec_ref, hh)[r, :] = y.astype(BF16)
        return carry

    lax.fori_loop(0, nc, o_body, 0, unroll=min(nc, 32))


def _recurrence(rq, g, ri, rg, gain, s0, batch, seq_len, emit_state, hps):
    t = rq.shape[0]
    nc = seq_len // CHUNK
    has_s0 = s0 is not None
    hb = R_HEADS // hps

    def col(off):
        return pl.BlockSpec((seq_len, hps * R_DK), lambda b, h: (b, h + off))

    def state(nblk, off):
        return pl.BlockSpec((hps, R_DK, R_DK), lambda b, h: (b * nblk + off + h, 0, 0))

    in_specs = [col(0), col(0), col(hb), col(0), col(0), _resident((1, R_DK))]
    args = [rq, g, g, ri, rg, gain]
    if has_s0:
        in_specs += [state(2 * hb, 0), state(2 * hb, hb)]
        args += [s0, s0]
    out_shape = [jax.ShapeDtypeStruct((t, 1024), BF16)]
    out_specs = [col(0)]
    if emit_state:
        assert hps == R_HEADS
        out_shape += [jax.ShapeDtypeStruct((batch, 2, R_HEADS, R_DK, R_DK), F32)]
        out_specs += [pl.BlockSpec((None, 2, R_HEADS, R_DK, R_DK), lambda b, h: (b, 0, 0, 0, 0))]
    scratch = [pltpu.VMEM((hps, seq_len, 2 * R_DK), BF16),
               pltpu.VMEM((hps, seq_len, 2 * R_DK), F32),
               pltpu.VMEM((hps, nc, R_DK, 2 * R_DK), F32),
               pltpu.VMEM((hps, nc, R_DK, 2 * R_DK), BF16),
               pltpu.VMEM((hps, seq_len, CHUNK), BF16)]
    return pl.pallas_call(
        functools.partial(_rec_kernel, seq_len=seq_len, has_s0=has_s0, emit_state=emit_state, hps=hps),
        name="rec_s0" if has_s0 else "rec",
        grid=(batch, hb),
        in_specs=in_specs, out_specs=out_specs, out_shape=out_shape,
        scratch_shapes=scratch,
        compiler_params=_cparams(("arbitrary", "arbitrary")),
    )(*args)


def _mffn_kernel(x_ref, a_ref, r_ref, g_ref, xp_ref, ap_ref, rp_ref, gp_ref, xn_ref, an_ref, rn_ref, gn_ref,
                 wo_ref, n2_ref, wup_ref, cw_ref, cb_ref, wdn_ref, mod_ref, fn_ref,
                 y_ref, hbuf, act_scr, x1_scr, *, tiles_per_seq, ffc):
    tm = x_ref.shape[0]
    ts = pl.program_id(0) % tiles_per_seq

    def merge_rows(x, a, r, g):
        m = g[:, 0:D_MODEL].astype(F32) * a.astype(F32) + g[:, D_MODEL:2 * D_MODEL].astype(F32) * r.astype(F32)
        out = jnp.dot(m.astype(BF16), wo_ref[...], preferred_element_type=F32)
        x1 = x + mod_ref[2:3, :] * out
        ms = jnp.mean(x1 * x1, axis=-1, keepdims=True)
        h2 = x1 * lax.rsqrt(ms + EPS) * n2_ref[...]
        h2 = h2 * (1.0 + mod_ref[4:5, :]) + mod_ref[3:4, :]
        return x1, h2.astype(BF16)

    x1, h2 = merge_rows(x_ref[...], a_ref[...], r_ref[...], g_ref[...])
    x1_scr[...] = x1
    hbuf[HALO:HALO + tm, :] = h2
    _, hh = merge_rows(jnp.concatenate([xp_ref[...], xn_ref[...]], axis=0),
                       jnp.concatenate([ap_ref[...], an_ref[...]], axis=0),
                       jnp.concatenate([rp_ref[...], rn_ref[...]], axis=0),
                       jnp.concatenate([gp_ref[...], gn_ref[...]], axis=0))
    zero = jnp.zeros((HALO, D_MODEL), BF16)
    hbuf[0:HALO, :] = jnp.where(ts == 0, zero, hh[0:HALO, :])
    hbuf[HALO + tm:HALO + tm + HALO, :] = jnp.where(ts == tiles_per_seq - 1, zero, hh[HALO:2 * HALO, :])
    hb = hbuf[...]
    mrows = tm + 2 * HALO

    def conv(col):
        u = jnp.dot(hb, wup_ref[:, col:col + ffc], preferred_element_type=F32)
        up = pltpu.roll(u, 1, 0)[HALO:HALO + tm, :]
        un = pltpu.roll(u, mrows - 1, 0)[HALO:HALO + tm, :]
        uc = u[HALO:HALO + tm, :]
        w = cw_ref[:, col:col + ffc]
        return up * w[0:1, :] + uc * w[1:2, :] + un * w[2:3, :] + cb_ref[:, col:col + ffc]

    for c0 in range(0, D_FF, ffc):
        a = conv(c0)
        b = conv(D_FF + c0)
        act_scr[:, c0:c0 + ffc] = (a * _sigmoid(a) * b).astype(BF16)

    f = jnp.dot(act_scr[...], wdn_ref[...], preferred_element_type=F32)
    x2 = x1_scr[...] + mod_ref[5:6, :] * f
    ms = jnp.mean(x2 * x2, axis=-1, keepdims=True)
    y_ref[...] = x2 * lax.rsqrt(ms + EPS) * fn_ref[...]


def _merge_ffn(x2d, attn, rec, gates, w, mods, mod_row, seq_len, tm):
    t, d = x2d.shape
    tiles_per_seq = seq_len // tm
    hb = tm // HALO
    n_hb = t // HALO

    def main(width):
        return pl.BlockSpec((tm, width), lambda i: (i, 0))

    def prev(width):
        return pl.BlockSpec((HALO, width), lambda i: (jnp.maximum(i * hb - 1, 0), 0))

    def nxt(width):
        return pl.BlockSpec((HALO, width), lambda i: (jnp.minimum((i + 1) * hb, n_hb - 1), 0))

    toks = [x2d, attn, rec, gates]
    widths = [d, d, d, 2 * d]
    return pl.pallas_call(
        functools.partial(_mffn_kernel, tiles_per_seq=tiles_per_seq, ffc=256),
        name="merge_ffn",
        grid=(t // tm,),
        in_specs=([main(wd) for wd in widths] + [prev(wd) for wd in widths] + [nxt(wd) for wd in widths]
                  + [_resident((d, d)), _resident((1, d)),
                     _resident((d, 2 * D_FF)), _resident((3, 2 * D_FF)), _resident((1, 2 * D_FF)),
                     _resident((D_FF, d)),
                     pl.BlockSpec((None, 6, d), lambda i: (mod_row(i // tiles_per_seq), 0, 0)),
                     _resident((1, d))]),
        out_specs=pl.BlockSpec((tm, d), lambda i: (i, 0)),
        out_shape=jax.ShapeDtypeStruct((t, d), F32),
        scratch_shapes=[pltpu.VMEM((tm + 2 * HALO, d), BF16),
                        pltpu.VMEM((tm, D_FF), BF16),
                        pltpu.VMEM((tm, d), F32)],
        compiler_params=_cparams(("arbitrary",)),
    )(*toks, *toks, *toks, w["w_o"], w["n2"], w["w_up"], w["conv_w"], w["conv_b"], w["w_down"], mods, w["fnorm"])


def _rope_tables(n_tokens):
    rows = n_tokens // GRID_W
    row = jnp.repeat(jnp.arange(rows, dtype=F32), GRID_W)
    colp = jnp.tile(jnp.arange(GRID_W, dtype=F32), rows)
    half = HEAD_DIM // 2
    inv_freq = 1.0 / (ROPE_THETA ** (jnp.arange(0, half, 2, dtype=F32) / half))
    ar = row[:, None] * inv_freq
    ac = colp[:, None] * inv_freq
    ang = jnp.concatenate([ar, ar, ac, ac], axis=-1)
    sign = jnp.asarray(np.tile(np.repeat(np.array([-1.0, 1.0], np.float32), 16), 2))
    cs = jnp.cos(ang)
    sn = jnp.sin(ang) * sign
    return jnp.tile(cs, (1, 2)), jnp.tile(sn, (1, 2))


def _group(x, mods, mod_row, w, ctx, rope_tabs, tq, pps, agroup, hps, tm_in, tm_ffn, emit_state):
    batch, seq_len, d = x.shape
    x2d = x.reshape(batch * seq_len, d)
    cs, sn = rope_tabs if rope_tabs is not None else (None, None)
    kv_t = emit_state
    q, k, v, rq, g, ri, rg, gates = _inproj(
        x2d, mods, mod_row, seq_len, w["n1"], w["w_in"], w["qg"], w["kg"], w["lbl"], cs, sn, tm=tm_in,
        kv_t=kv_t)
    ck, cv, s0 = ctx
    attn = _attention(q, k, v, ck, cv, batch, seq_len, tq, pps, agroup, kv_t)
    rec_out = _recurrence(rq, g, ri, rg, w["rgain"], s0, batch, seq_len, emit_state, hps)
    rec = rec_out[0]
    y = _merge_ffn(x2d, attn, rec, gates, w, mods, mod_row, seq_len, tm=tm_ffn)
    return y.reshape(batch, seq_len, d), k, v, rec_out[1:]


def kernel(x_prompt, x_sample, c, cache_k, cache_v, state_hgrn, c_ctx, ada_w, ada_b, norm1, norm2,
           w_in, q_norm, k_norm, hgrn_lb_logits, hgrn_norm, w_o, w_up, conv_w, conv_b, w_down, final_norm):
    bp, lp, d = x_prompt.shape
    bs, ls, _ = x_sample.shape
    n_ctx = cache_k.shape[2]

    cond = jnp.zeros((16, d), F32).at[0].set(c_ctx).at[1:1 + bs].set(c)
    mods = _mods(cond, ada_w[0], ada_b[0]).reshape(16, 6, d)

    w = dict(
        n1=norm1[0].reshape(1, d), n2=norm2[0].reshape(1, d), fnorm=final_norm.reshape(1, d),
        w_in=w_in[0].astype(BF16), w_o=w_o[0].astype(BF16),
        w_up=w_up[0].astype(BF16), w_down=w_down[0].astype(BF16),
        conv_w=conv_w[0], conv_b=conv_b[0].reshape(1, 2 * D_FF),
        qg=jnp.tile(q_norm[0], N_HEADS).reshape(1, 1024),
        kg=jnp.tile(k_norm[0], N_KV_HEADS).reshape(1, KV_W),
        lbl=hgrn_lb_logits.reshape(4, 1024),
        rgain=hgrn_norm[0].reshape(1, R_DK),
    )

    y_p, k_p, v_p, st = _group(x_prompt, mods, lambda b: 0, w, (None, None, None), None,
                               tq=lp, pps=8, agroup=8, hps=R_HEADS, tm_in=lp, tm_ffn=lp, emit_state=True)
    def feature_major(cache):
        return jnp.transpose(cache[:, 0], (0, 2, 3, 1)).reshape(bs, KV_W, n_ctx)

    ctx = (feature_major(cache_k), feature_major(cache_v),
           state_hgrn[:, 0].reshape(bs * 2 * R_HEADS, R_DK, R_DK))
    y_s, _, _, _ = _group(x_sample, mods, lambda b: b + 1, w, ctx, _rope_tables(ls),
                          tq=512, pps=4, agroup=1, hps=1, tm_in=512, tm_ffn=512, emit_state=False)

    new_state = st[0][:, None]
    def position_major(kvt):
        return jnp.transpose(kvt.reshape(bp, N_KV_HEADS, HEAD_DIM, lp), (0, 3, 1, 2))[:, None]

    new_k = position_major(k_p)
    new_v = position_major(v_p)
    return (y_p, y_s, new_k, new_v, new_state)
```

```python
import functools

import jax
import jax.numpy as jnp
import numpy as np
from jax import lax
from jax.experimental import pallas as pl
from jax.experimental.pallas import tpu as pltpu

F32 = jnp.float32
BF16 = jnp.bfloat16

D_MODEL = 1024
GRID_W = 64
HEAD_DIM = 64
N_HEADS = 16
N_KV_HEADS = 4
KV_W = N_KV_HEADS * HEAD_DIM
ROPE_THETA = 10000.0
R_DK = 128
R_HEADS = 8
CHUNK = 64
D_FF = 2816
EPS = 1e-6
LOG2E = 1.4426950408889634
IN_W = 8704

LANES = 128
HALO = 16
VT_ROWS = HEAD_DIM + 16
SM_ROWS = 32
PROJ_W = 512
SHIFT_SLACK = 100.0
VMEM_LIMIT = 56 * 1024 * 1024

_Q0, _K0, _V0, _RQ0, _RFF0, _RFB0, _RI0, _RG0, _ZA0, _ZR0 = (
    0, 1024, 1280, 1536, 2560, 3584, 4608, 5632, 6656, 7680)


def _sigmoid(x):
    return 0.5 * jnp.tanh(0.5 * x) + 0.5


def _cparams(sem):
    return pltpu.CompilerParams(dimension_semantics=sem, vmem_limit_bytes=VMEM_LIMIT)


def _resident(shape):
    nd = len(shape)
    return pl.BlockSpec(shape, lambda *_: (0,) * nd, pipeline_mode=pl.Buffered(1))


def _mod_kernel(c_ref, w_ref, b_ref, o_ref):
    c = c_ref[...]
    s = c * _sigmoid(c)
    o_ref[...] = jnp.dot(s.astype(BF16), w_ref[...].astype(BF16), preferred_element_type=F32) + b_ref[...]


def _mods(cond, ada_w, ada_b):
    rows, d = cond.shape
    n = ada_w.shape[1]
    tn = 1536
    return pl.pallas_call(
        _mod_kernel,
        name="mods",
        grid=(n // tn,),
        in_specs=[pl.BlockSpec((rows, d), lambda j: (0, 0)),
                  pl.BlockSpec((d, tn), lambda j: (0, j)),
                  pl.BlockSpec((1, tn), lambda j: (0, j))],
        out_specs=pl.BlockSpec((rows, tn), lambda j: (0, j)),
        out_shape=jax.ShapeDtypeStruct((rows, n), F32),
        compiler_params=_cparams(("arbitrary",)),
    )(cond, ada_w, ada_b.reshape(1, n))


def _inproj_kernel(*refs, rope, kv_t):
    if rope:
        (x_ref, mod_ref, n1_ref, w_ref, qg_ref, kg_ref, lbl_ref, cs_ref, sn_ref,
         q_ref, k_ref, v_ref, rq_ref, g_ref, ri_ref, rg_ref, gt_ref) = refs
    else:
        (x_ref, mod_ref, n1_ref, w_ref, qg_ref, kg_ref, lbl_ref,
         q_ref, k_ref, v_ref, rq_ref, g_ref, ri_ref, rg_ref, gt_ref) = refs
        cs_ref = sn_ref = None
    tm = x_ref.shape[0]

    x = x_ref[...]
    ms = jnp.mean(x * x, axis=-1, keepdims=True)
    h = x * lax.rsqrt(ms + EPS) * n1_ref[...]
    h = h * (1.0 + mod_ref[1:2, :]) + mod_ref[0:1, :]
    hb = h.astype(BF16)

    def proj(a, b):
        return jnp.dot(hb, w_ref[:, a:b], preferred_element_type=F32)

    low_head = lax.broadcasted_iota(jnp.int32, (tm, LANES), 1) < HEAD_DIM

    def headnorm(a, gain):
        sq = a * a
        s_lo = jnp.sum(jnp.where(low_head, sq, 0.0), axis=-1, keepdims=True)
        s_hi = jnp.sum(jnp.where(low_head, 0.0, sq), axis=-1, keepdims=True)
        ms = jnp.where(low_head, s_lo, s_hi) * (1.0 / HEAD_DIM)
        return a * lax.rsqrt(ms + EPS) * gain

    if rope:
        lane = lax.broadcasted_iota(jnp.int32, (tm, LANES), 1)
        first = (lane % 32) < 16
        cs = cs_ref[...]
        sn = sn_ref[...]

        def rot(a):
            sw = jnp.where(first, pltpu.roll(a, LANES - 16, 1), pltpu.roll(a, 16, 1))
            return a * cs + sw * sn
    else:
        def rot(a):
            return a

    scale = HEAD_DIM ** -0.5 * LOG2E
    for c0 in range(0, 1024, 512):
        acc = proj(_Q0 + c0, _Q0 + c0 + 512)
        for s in range(4):
            col = c0 + s * LANES
            a = headnorm(acc[:, s * LANES:(s + 1) * LANES], qg_ref[:, col:col + LANES] * scale)
            q_ref[:, col:col + LANES] = rot(a).astype(BF16)

    acc = proj(_K0, _K0 + 512)
    for s in range(2):
        col = s * LANES
        a = rot(headnorm(acc[:, col:col + LANES], kg_ref[:, col:col + LANES]))
        vv = acc[:, KV_W + col:KV_W + col + LANES]
        if kv_t:
            k_ref[col:col + LANES, :] = a.T
            v_ref[col:col + LANES, :] = vv.T
        else:
            k_ref[:, col:col + LANES] = a
            v_ref[:, col:col + LANES] = vv

    rscale = R_DK ** -0.5
    for c0 in range(0, 1024, PROJ_W):
        acc = proj(_RQ0 + c0, _RQ0 + c0 + PROJ_W)
        rq_ref[:, c0:c0 + PROJ_W] = (acc * _sigmoid(acc) * rscale).astype(BF16)

    l = lbl_ref[...]
    for d in range(2):
        a0 = l[2 * d:2 * d + 1, :]
        a1 = l[2 * d + 1:2 * d + 2, :]
        mx = jnp.maximum(a0, a1)
        e0 = jnp.exp(a0 - mx)
        e1 = jnp.exp(a1 - mx)
        lb = e0 / (e0 + e1)
        for c0 in range(0, 1024, PROJ_W):
            acc = proj(_RFF0 + d * 1024 + c0, _RFF0 + d * 1024 + c0 + PROJ_W)
            lbc = lb[:, c0:c0 + PROJ_W]
            g_ref[:, d * 1024 + c0:d * 1024 + c0 + PROJ_W] = lbc + (1.0 - lbc) * _sigmoid(acc)

    for c0 in range(0, 1024, PROJ_W):
        ri_ref[:, c0:c0 + PROJ_W] = proj(_RI0 + c0, _RI0 + c0 + PROJ_W).astype(BF16)
    for c0 in range(0, 1024, PROJ_W):
        acc = proj(_RG0 + c0, _RG0 + c0 + PROJ_W)
        rg_ref[:, c0:c0 + PROJ_W] = (acc * _sigmoid(acc)).astype(BF16)
    for c0 in range(0, 2048, PROJ_W):
        acc = proj(_ZA0 + c0, _ZA0 + c0 + PROJ_W)
        gt_ref[:, c0:c0 + PROJ_W] = _sigmoid(acc).astype(BF16)


def _inproj(x2d, mods, mod_row, seq_len, n1, w_in, qg, kg, lbl, cs, sn, tm, kv_t):
    t, d = x2d.shape
    tiles_per_seq = seq_len // tm
    rope = cs is not None

    def tok(w):
        return pl.BlockSpec((tm, w), lambda i: (i, 0))

    in_specs = [tok(d),
                pl.BlockSpec((None, 6, d), lambda i: (mod_row(i // tiles_per_seq), 0, 0)),
                _resident((1, d)),
                _resident((d, IN_W)),
                _resident((1, 1024)),
                _resident((1, KV_W)),
                _resident((4, 1024))]
    args = [x2d, mods, n1, w_in, qg, kg, lbl]
    if rope:
        in_specs += [pl.BlockSpec((tm, LANES), lambda i: (i % tiles_per_seq, 0))] * 2
        args += [cs, sn]
    out_shape = [jax.ShapeDtypeStruct((t, 1024), BF16),
                 jax.ShapeDtypeStruct((t, KV_W), F32),
                 jax.ShapeDtypeStruct((t, KV_W), F32),
                 jax.ShapeDtypeStruct((t, 1024), BF16),
                 jax.ShapeDtypeStruct((t, 2048), F32),
                 jax.ShapeDtypeStruct((t, 1024), BF16),
                 jax.ShapeDtypeStruct((t, 1024), BF16),
                 jax.ShapeDtypeStruct((t, 2048), BF16)]
    out_specs = [tok(s.shape[1]) for s in out_shape]
    if kv_t:
        kvt_shape = jax.ShapeDtypeStruct((t // seq_len, KV_W, seq_len), F32)
        kvt_spec = pl.BlockSpec((None, KV_W, tm), lambda i: (i // tiles_per_seq, 0, i % tiles_per_seq))
        out_shape[1:3] = [kvt_shape, kvt_shape]
        out_specs[1:3] = [kvt_spec, kvt_spec]
    return pl.pallas_call(
        functools.partial(_inproj_kernel, rope=rope, kv_t=kv_t),
        name="inproj_rope" if rope else "inproj",
        grid=(t // tm,),
        in_specs=in_specs, out_specs=out_specs, out_shape=out_shape,
        compiler_params=_cparams(("arbitrary",)),
    )(*args)


def _attn_kernel(*refs, n_lat, n_ctx, tq, pps, kchunk, group, kv_t):
    if n_ctx:
        q_ref, k_ref, v_ref, ck_ref, cv_ref, o_ref, kpad, vt, ref_scr, *p_bufs = refs
    else:
        q_ref, k_ref, v_ref, o_ref, kpad, vt, ref_scr, *p_bufs = refs
        ck_ref = cv_ref = None
    lk = n_lat + n_ctx
    first_step = jnp.logical_and(pl.program_id(1) == 0, pl.program_id(2) == 0)

    @pl.when(first_step)
    def _build():
        def place(dst, rows, ks):
            lane = lax.broadcasted_iota(jnp.int32, ks.shape, 1)
            lo = jnp.where(lane < HEAD_DIM, ks, 0.0)
            hi = jnp.where(lane >= HEAD_DIM, ks, 0.0)
            kpad[dst + 0, rows, :] = lo.astype(BF16)
            kpad[dst + 1, rows, :] = pltpu.roll(lo, HEAD_DIM, 1).astype(BF16)
            kpad[dst + 2, rows, :] = pltpu.roll(hi, HEAD_DIM, 1).astype(BF16)
            kpad[dst + 3, rows, :] = hi.astype(BF16)

        def place_v(s, cols_out, vtr):
            vt[2 * s, 0:HEAD_DIM, cols_out] = vtr[0:HEAD_DIM, :].astype(BF16)
            vt[2 * s + 1, 0:HEAD_DIM, cols_out] = vtr[HEAD_DIM:2 * HEAD_DIM, :].astype(BF16)

        ones_rows = jnp.where(lax.broadcasted_iota(jnp.int32, (VT_ROWS - HEAD_DIM, lk), 0) == 0,
                              1.0, 0.0).astype(BF16)
        for kvh in range(N_KV_HEADS):
            vt[kvh, HEAD_DIM:VT_ROWS, :] = ones_rows
        for s in range(2):
            cols = slice(s * LANES, (s + 1) * LANES)
            if kv_t:
                place(4 * s, slice(0, n_lat), k_ref[cols, :].T)
                place_v(s, slice(0, n_lat), v_ref[cols, :])
            else:
                place(4 * s, slice(0, n_lat), k_ref[:, cols])
                place_v(s, slice(0, n_lat), v_ref[:, cols].T)
            if n_ctx:
                place(4 * s, slice(n_lat, lk), ck_ref[cols, :].T)
                place_v(s, slice(n_lat, lk), cv_ref[cols, :])

    n_chunks = lk // kchunk
    pair0 = pl.program_id(2) * pps
    nt = (((1,), (1,)), ((), ()))
    units = [(j, parity) for j in range(pps) for parity in range(2)]

    def unit_operands(u):
        j, parity = units[u]
        kv = (pair0 + j) // 2
        return 2 * kv + parity, kv, q_ref[:, j * LANES:(j + 1) * LANES]

    def bcast_max(x8):
        return jnp.broadcast_to(jnp.max(x8, axis=0, keepdims=True), (8, tq))

    def exact_max():
        for u in range(len(units)):
            kidx, _, q2 = unit_operands(u)
            mx = jnp.full((8, tq), -jnp.inf, F32)
            for c in range(n_chunks):
                s = lax.dot_general(kpad[kidx, c * kchunk:(c + 1) * kchunk, :], q2, nt,
                                    preferred_element_type=F32)
                mx = jnp.maximum(mx, jnp.max(s.reshape(kchunk // 8, 8, tq), axis=0))
            ref_scr[u] = bcast_max(mx)

    def softmax_pass():
        worst = jnp.zeros((8, tq), F32)
        outs = []
        for u0 in range(0, len(units), group):
            us = range(u0, u0 + group)
            scores = [lax.dot_general(kpad[unit_operands(u)[0]], unit_operands(u)[2], nt,
                                      preferred_element_type=F32) for u in us]
            for u, s in zip(us, scores):
                p_scr = p_bufs[u % len(p_bufs)]
                r8 = ref_scr[u]
                mtot = jnp.full((8, tq), -jnp.inf, F32)
                for r0 in range(0, lk, SM_ROWS):
                    sb = s[r0:r0 + SM_ROWS, :].reshape(SM_ROWS // 8, 8, tq)
                    mtot = jnp.maximum(mtot, jnp.max(sb, axis=0))
                    p_scr[r0:r0 + SM_ROWS, :] = jnp.exp2(sb - r8).reshape(SM_ROWS, tq).astype(BF16)
                worst = jnp.maximum(worst, jnp.abs(bcast_max(mtot) - r8))
            for u in us:
                ot = jnp.dot(vt[unit_operands(u)[1]], p_bufs[u % len(p_bufs)][...],
                             preferred_element_type=F32)
                outs.append(ot[0:HEAD_DIM, :] * (1.0 / ot[HEAD_DIM:HEAD_DIM + 1, :]))
                if u % 2 == 1:
                    j = units[u][0]
                    o2t = jnp.concatenate(outs[-2:], axis=0)
                    o_ref[:, j * LANES:(j + 1) * LANES] = o2t.T.astype(BF16)
        return worst

    def attempt(state):
        n, _ = state

        @pl.when(n == 0)
        def _():
            ref_scr[...] = jnp.zeros(ref_scr.shape, F32)

        @pl.when(n == 1)
        def _():
            exact_max()

        worst = softmax_pass()
        return n + 1, (jnp.max(worst) > SHIFT_SLACK).astype(jnp.int32)

    lax.while_loop(lambda st: jnp.logical_or(st[0] == 0, jnp.logical_and(st[0] == 1, st[1] == 1)),
                   attempt, (jnp.int32(0), jnp.int32(0)))


def _attention(q, k, v, ck, cv, batch, seq_len, tq, pps, group, kv_t):
    t = q.shape[0]
    n_ctx = 0 if ck is None else ck.shape[2]
    lk = seq_len + n_ctx
    n_qt = seq_len // tq
    n_pp = (N_HEADS // 2) // pps
    kchunk = min(lk, 512)
    if kv_t:
        kv_spec = pl.BlockSpec((None, KV_W, seq_len), lambda b, i, p: (b, 0, 0))
    else:
        kv_spec = pl.BlockSpec((seq_len, KV_W), lambda b, i, p: (b, 0))
    in_specs = [pl.BlockSpec((tq, pps * LANES), lambda b, i, p: (b * n_qt + i, p)), kv_spec, kv_spec]
    args = [q, k, v]
    if n_ctx:
        in_specs += [pl.BlockSpec((None, KV_W, n_ctx), lambda b, i, p: (b, 0, 0))] * 2
        args += [ck, cv]
    return pl.pallas_call(
        functools.partial(_attn_kernel, n_lat=seq_len, n_ctx=n_ctx, tq=tq, pps=pps, kchunk=kchunk, group=group, kv_t=kv_t),
        name="attn_ctx" if n_ctx else "attn",
        grid=(batch, n_qt, n_pp),
        in_specs=in_specs,
        out_specs=pl.BlockSpec((tq, pps * LANES), lambda b, i, p: (b * n_qt + i, p)),
        out_shape=jax.ShapeDtypeStruct((t, 1024), BF16),
        scratch_shapes=[pltpu.VMEM((2 * N_KV_HEADS, lk, LANES), BF16),
                        pltpu.VMEM((N_KV_HEADS, VT_ROWS, lk), BF16),
                        pltpu.VMEM((2 * pps, 8, tq), F32)]
                       + [pltpu.VMEM((lk, tq), BF16)] * max(2, group),
        compiler_params=_cparams(("arbitrary", "arbitrary", "arbitrary")),
    )(*args)


def _rec_kernel(*refs, seq_len, has_s0, emit_state, hps):
    refs = list(refs)
    rq_ref, gf_ref, gb_ref, ri_ref, rg_ref, gain_ref = refs[:6]
    pos = 6
    s0f_ref = s0b_ref = sf_ref = sb_ref = None
    if has_s0:
        s0f_ref, s0b_ref = refs[pos:pos + 2]
        pos += 2
    rec_ref = refs[pos]
    pos += 1
    if emit_state:
        st_ref = refs[pos]
        sf_ref, sb_ref = st_ref.at[0], st_ref.at[1]
        pos += 1
    qd_scr, d_scr, u_scr, st_scr, a_scr = refs[pos:]

    nc = seq_len // CHUNK
    tpos = lax.broadcasted_iota(jnp.int32, (CHUNK, R_DK), 0)
    ti = lax.broadcasted_iota(jnp.int32, (CHUNK, CHUNK), 0)
    si = lax.broadcasted_iota(jnp.int32, (CHUNK, CHUNK), 1)
    nt = (((1,), (1,)), ((), ()))
    tn = (((0,), (0,)), ((), ()))
    gain = gain_ref[...]

    def cols(r, hh):
        return r.at[:, pl.ds(hh * R_DK, R_DK)]

    def rows(c):
        return pl.ds(pl.multiple_of(c * CHUNK, CHUNK), CHUNK)

    def chunk_cumsum(g, reverse):
        b = g
        for s in (1, 2, 4):
            if reverse:
                sh = pltpu.roll(b, CHUNK - s, 0)
                b = b + jnp.where(tpos < CHUNK - s, sh, 0.0)
            else:
                sh = pltpu.roll(b, s, 0)
                b = b + jnp.where(tpos >= s, sh, 0.0)
        for s in (8, 16, 32):
            if reverse:
                b = jnp.concatenate([b[:CHUNK - s] + b[s:], b[CHUNK - s:]], axis=0)
            else:
                b = jnp.concatenate([b[:s], b[s:] + b[:CHUNK - s]], axis=0)
        return b

    def ua_chunk(c, hh):
        r = rows(c)
        rq = cols(rq_ref, hh)[r, :].astype(F32)
        a = None
        ke = []
        for d, g_ref in enumerate((gf_ref, gb_ref)):
            f = cols(g_ref, hh)[r, :]
            kk = 1.0 - f
            g = jnp.log2(f)
            b = chunk_cumsum(g, reverse=(d == 1))
            etot = jnp.broadcast_to(jnp.exp2(jnp.sum(g, axis=0, keepdims=True)), (CHUNK, R_DK))
            eb = jnp.exp2(b)
            qd = (rq * eb).astype(BF16)
            kd = kk / eb
            ke.append((kd * etot).astype(BF16))
            qd_scr[hh, r, d * R_DK:(d + 1) * R_DK] = qd
            d_scr[hh, r, d * R_DK:(d + 1) * R_DK] = etot
            sc = lax.dot_general(qd, kd.astype(BF16), nt, preferred_element_type=F32)
            sc = jnp.where(si <= ti, sc, 0.0) if d == 0 else jnp.where(si >= ti, sc, 0.0)
            a = sc if a is None else a + sc
        a_scr[hh, r, :] = a.astype(BF16)
        u_scr[hh, c] = lax.dot_general(cols(ri_ref, hh)[r, :], jnp.concatenate(ke, axis=1), tn,
                                       preferred_element_type=F32)

    def ua_body(c, carry):
        for hh in range(hps):
            ua_chunk(c, hh)
        return carry

    lax.fori_loop(0, nc, ua_body, 0, unroll=min(nc, 16))

    for hh in range(hps):
        if has_s0:
            sf0 = s0f_ref[hh].T
            sb0 = s0b_ref[hh].T
        else:
            sf0 = jnp.zeros((R_DK, R_DK), F32)
            sb0 = jnp.zeros((R_DK, R_DK), F32)

        def scan_body(i, carry, hh=hh):
            sf, sb = carry
            cb = nc - 1 - i
            st_scr[hh, i, :, 0:R_DK] = sf.astype(BF16)
            st_scr[hh, cb, :, R_DK:2 * R_DK] = sb.astype(BF16)
            df = d_scr[hh, pl.ds(pl.multiple_of(i * CHUNK, CHUNK), 1), 0:R_DK]
            db = d_scr[hh, pl.ds(pl.multiple_of(cb * CHUNK, CHUNK), 1), R_DK:2 * R_DK]
            sf = df * sf + u_scr[hh, i, :, 0:R_DK]
            sb = db * sb + u_scr[hh, cb, :, R_DK:2 * R_DK]
            return sf, sb

        sf, sb = lax.fori_loop(0, nc, scan_body, (sf0, sb0))
        if emit_state:
            sf_ref[hh] = sf.T
            sb_ref[hh] = sb.T

    def o_body(c, carry):
        r = rows(c)
        for hh in range(hps):
            o = jnp.dot(a_scr[hh, r, :], cols(ri_ref, hh)[r, :], preferred_element_type=F32)
            o = o + lax.dot_general(qd_scr[hh, r, :], st_scr[hh, c], nt, preferred_element_type=F32)
            ms = jnp.mean(o * o, axis=-1, keepdims=True)
            y = o * lax.rsqrt(ms + EPS) * gain * cols(rg_ref, hh)[r, :].astype(F32)
            cols(rec_ref, hh)[r, :] = y.astype(BF16)
        return carry

    lax.fori_loop(0, nc, o_body, 0, unroll=min(nc, 32))


def _recurrence(rq, g, ri, rg, gain, s0, batch, seq_len, emit_state, hps):
    t = rq.shape[0]
    nc = seq_len // CHUNK
    has_s0 = s0 is not None
    hb = R_HEADS // hps

    def col(off):
        return pl.BlockSpec((seq_len, hps * R_DK), lambda b, h: (b, h + off))

    def state(nblk, off):
        return pl.BlockSpec((hps, R_DK, R_DK), lambda b, h: (b * nblk + off + h, 0, 0))

    in_specs = [col(0), col(0), col(hb), col(0), col(0), _resident((1, R_DK))]
    args = [rq, g, g, ri, rg, gain]
    if has_s0:
        in_specs += [state(2 * hb, 0), state(2 * hb, hb)]
        args += [s0, s0]
    out_shape = [jax.ShapeDtypeStruct((t, 1024), BF16)]
    out_specs = [col(0)]
    if emit_state:
        assert hps == R_HEADS
        out_shape += [jax.ShapeDtypeStruct((batch, 2, R_HEADS, R_DK, R_DK), F32)]
        out_specs += [pl.BlockSpec((None, 2, R_HEADS, R_DK, R_DK), lambda b, h: (b, 0, 0, 0, 0))]
    scratch = [pltpu.VMEM((hps, seq_len, 2 * R_DK), BF16),
               pltpu.VMEM((hps, seq_len, 2 * R_DK), F32),
               pltpu.VMEM((hps, nc, R_DK, 2 * R_DK), F32),
               pltpu.VMEM((hps, nc, R_DK, 2 * R_DK), BF16),
               pltpu.VMEM((hps, seq_len, CHUNK), BF16)]
    return pl.pallas_call(
        functools.partial(_rec_kernel, seq_len=seq_len, has_s0=has_s0, emit_state=emit_state, hps=hps),
        name="rec_s0" if has_s0 else "rec",
        grid=(batch, hb),
        in_specs=in_specs, out_specs=out_specs, out_shape=out_shape,
        scratch_shapes=scratch,
        compiler_params=_cparams(("arbitrary", "arbitrary")),
    )(*args)


def _mffn_kernel(x_ref, a_ref, r_ref, g_ref, xp_ref, ap_ref, rp_ref, gp_ref, xn_ref, an_ref, rn_ref, gn_ref,
                 wo_ref, n2_ref, wup_ref, cw_ref, cb_ref, wdn_ref, mod_ref, fn_ref,
                 y_ref, hbuf, act_scr, x1_scr, *, tiles_per_seq, ffc):
    tm = x_ref.shape[0]
    ts = pl.program_id(0) % tiles_per_seq

    def merge_rows(x, a, r, g):
        m = g[:, 0:D_MODEL].astype(F32) * a.astype(F32) + g[:, D_MODEL:2 * D_MODEL].astype(F32) * r.astype(F32)
        out = jnp.dot(m.astype(BF16), wo_ref[...], preferred_element_type=F32)
        x1 = x + mod_ref[2:3, :] * out
        ms = jnp.mean(x1 * x1, axis=-1, keepdims=True)
        h2 = x1 * lax.rsqrt(ms + EPS) * n2_ref[...]
        h2 = h2 * (1.0 + mod_ref[4:5, :]) + mod_ref[3:4, :]
        return x1, h2.astype(BF16)

    x1, h2 = merge_rows(x_ref[...], a_ref[...], r_ref[...], g_ref[...])
    x1_scr[...] = x1
    hbuf[HALO:HALO + tm, :] = h2
    _, hh = merge_rows(jnp.concatenate([xp_ref[...], xn_ref[...]], axis=0),
                       jnp.concatenate([ap_ref[...], an_ref[...]], axis=0),
                       jnp.concatenate([rp_ref[...], rn_ref[...]], axis=0),
                       jnp.concatenate([gp_ref[...], gn_ref[...]], axis=0))
    zero = jnp.zeros((HALO, D_MODEL), BF16)
    hbuf[0:HALO, :] = jnp.where(ts == 0, zero, hh[0:HALO, :])
    hbuf[HALO + tm:HALO + tm + HALO, :] = jnp.where(ts == tiles_per_seq - 1, zero, hh[HALO:2 * HALO, :])
    hb = hbuf[...]
    mrows = tm + 2 * HALO

    def conv(col):
        u = jnp.dot(hb, wup_ref[:, col:col + ffc], preferred_element_type=F32)
        up = pltpu.roll(u, 1, 0)[HALO:HALO + tm, :]
        un = pltpu.roll(u, mrows - 1, 0)[HALO:HALO + tm, :]
        uc = u[HALO:HALO + tm, :]
        w = cw_ref[:, col:col + ffc]
        return up * w[0:1, :] + uc * w[1:2, :] + un * w[2:3, :] + cb_ref[:, col:col + ffc]

    for c0 in range(0, D_FF, ffc):
        a = conv(c0)
        b = conv(D_FF + c0)
        act_scr[:, c0:c0 + ffc] = (a * _sigmoid(a) * b).astype(BF16)

    f = jnp.dot(act_scr[...], wdn_ref[...], preferred_element_type=F32)
    x2 = x1_scr[...] + mod_ref[5:6, :] * f
    ms = jnp.mean(x2 * x2, axis=-1, keepdims=True)
    y_ref[...] = x2 * lax.rsqrt(ms + EPS) * fn_ref[...]


def _merge_ffn(x2d, attn, rec, gates, w, mods, mod_row, seq_len, tm):
    t, d = x2d.shape
    tiles_per_seq = seq_len // tm
    hb = tm // HALO
    n_hb = t // HALO

    def main(width):
        return pl.BlockSpec((tm, width), lambda i: (i, 0))

    def prev(width):
        return pl.BlockSpec((HALO, width), lambda i: (jnp.maximum(i * hb - 1, 0), 0))

    def nxt(width):
        return pl.BlockSpec((HALO, width), lambda i: (jnp.minimum((i + 1) * hb, n_hb - 1), 0))

    toks = [x2d, attn, rec, gates]
    widths = [d, d, d, 2 * d]
    return pl.pallas_call(
        functools.partial(_mffn_kernel, tiles_per_seq=tiles_per_seq, ffc=256),
        name="merge_ffn",
        grid=(t // tm,),
        in_specs=([main(wd) for wd in widths] + [prev(wd) for wd in widths] + [nxt(wd) for wd in widths]
                  + [_resident((d, d)), _resident((1, d)),
                     _resident((d, 2 * D_FF)), _resident((3, 2 * D_FF)), _resident((1, 2 * D_FF)),
                     _resident((D_FF, d)),
                     pl.BlockSpec((None, 6, d), lambda i: (mod_row(i // tiles_per_seq), 0, 0)),
                     _resident((1, d))]),
        out_specs=pl.BlockSpec((tm, d), lambda i: (i, 0)),
        out_shape=jax.ShapeDtypeStruct((t, d), F32),
        scratch_shapes=[pltpu.VMEM((tm + 2 * HALO, d), BF16),
                        pltpu.VMEM((tm, D_FF), BF16),
                        pltpu.VMEM((tm, d), F32)],
        compiler_params=_cparams(("arbitrary",)),
    )(*toks, *toks, *toks, w["w_o"], w["n2"], w["w_up"], w["conv_w"], w["conv_b"], w["w_down"], mods, w["fnorm"])


def _rope_tables(n_tokens):
    rows = n_tokens // GRID_W
    row = jnp.repeat(jnp.arange(rows, dtype=F32), GRID_W)
    colp = jnp.tile(jnp.arange(GRID_W, dtype=F32), rows)
    half = HEAD_DIM // 2
    inv_freq = 1.0 / (ROPE_THETA ** (jnp.arange(0, half, 2, dtype=F32) / half))
    ar = row[:, None] * inv_freq
    ac = colp[:, None] * inv_freq
    ang = jnp.concatenate([ar, ar, ac, ac], axis=-1)
    sign = jnp.asarray(np.tile(np.repeat(np.array([-1.0, 1.0], np.float32), 16), 2))
    cs = jnp.cos(ang)
    sn = jnp.sin(ang) * sign
    return jnp.tile(cs, (1, 2)), jnp.tile(sn, (1, 2))


def _group(x, mods, mod_row, w, ctx, rope_tabs, tq, pps, agroup, hps, tm_in, tm_ffn, emit_state):
    batch, seq_len, d = x.shape
    x2d = x.reshape(batch * seq_len, d)
    cs, sn = rope_tabs if rope_tabs is not None else (None, None)
    kv_t = emit_state
    q, k, v, rq, g, ri, rg, gates = _inproj(
        x2d, mods, mod_row, seq_len, w["n1"], w["w_in"], w["qg"], w["kg"], w["lbl"], cs, sn, tm=tm_in,
        kv_t=kv_t)
    ck, cv, s0 = ctx
    attn = _attention(q, k, v, ck, cv, batch, seq_len, tq, pps, agroup, kv_t)
    rec_out = _recurrence(rq, g, ri, rg, w["rgain"], s0, batch, seq_len, emit_state, hps)
    rec = rec_out[0]
    y = _merge_ffn(x2d, attn, rec, gates, w, mods, mod_row, seq_len, tm=tm_ffn)
    return y.reshape(batch, seq_len, d), k, v, rec_out[1:]


def kernel(x_prompt, x_sample, c, cache_k, cache_v, state_hgrn, c_ctx, ada_w, ada_b, norm1, norm2,
           w_in, q_norm, k_norm, hgrn_lb_logits, hgrn_norm, w_o, w_up, conv_w, conv_b, w_down, final_norm):
    bp, lp, d = x_prompt.shape
    bs, ls, _ = x_sample.shape
    n_ctx = cache_k.shape[2]

    cond = jnp.zeros((16, d), F32).at[0].set(c_ctx).at[1:1 + bs].set(c)
    mods = _mods(cond, ada_w[0], ada_b[0]).reshape(16, 6, d)

    w = dict(
        n1=norm1[0].reshape(1, d), n2=norm2[0].reshape(1, d), fnorm=final_norm.reshape(1, d),
        w_in=w_in[0].astype(BF16), w_o=w_o[0].astype(BF16),
        w_up=w_up[0].astype(BF16), w_down=w_down[0].astype(BF16),
        conv_w=conv_w[0], conv_b=conv_b[0].reshape(1, 2 * D_FF),
        qg=jnp.tile(q_norm[0], N_HEADS).reshape(1, 1024),
        kg=jnp.tile(k_norm[0], N_KV_HEADS).reshape(1, KV_W),
        lbl=hgrn_lb_logits.reshape(4, 1024),
        rgain=hgrn_norm[0].reshape(1, R_DK),
    )

    y_p, k_p, v_p, st = _group(x_prompt, mods, lambda b: 0, w, (None, None, None), None,
                               tq=lp, pps=8, agroup=8, hps=R_HEADS, tm_in=lp, tm_ffn=lp, emit_state=True)
    def feature_major(cache):
        return jnp.transpose(cache[:, 0], (0, 2, 3, 1)).reshape(bs, KV_W, n_ctx)

    ctx = (feature_major(cache_k), feature_major(cache_v),
           state_hgrn[:, 0].reshape(bs * 2 * R_HEADS, R_DK, R_DK))
    y_s, _, _, _ = _group(x_sample, mods, lambda b: b + 1, w, ctx, _rope_tables(ls),
                          tq=512, pps=4, agroup=1, hps=1, tm_in=512, tm_ffn=512, emit_state=False)

    new_state = st[0][:, None]
    def position_major(kvt):
        return jnp.transpose(kvt.reshape(bp, N_KV_HEADS, HEAD_DIM, lp), (0, 3, 1, 2))[:, None]

    new_k = position_major(k_p)
    new_v = position_major(v_p)
    return (y_p, y_s, new_k, new_v, new_state)
```

```python
import functools

import jax
import jax.numpy as jnp
import numpy as np
from jax import lax
from jax.experimental import pallas as pl
from jax.experimental.pallas import tpu as pltpu

F32 = jnp.float32
BF16 = jnp.bfloat16

D_MODEL = 1024
GRID_W = 64
HEAD_DIM = 64
N_HEADS = 16
N_KV_HEADS = 4
KV_W = N_KV_HEADS * HEAD_DIM
ROPE_THETA = 10000.0
R_DK = 128
R_HEADS = 8
CHUNK = 64
D_FF = 2816
EPS = 1e-6
LOG2E = 1.4426950408889634
IN_W = 8704

LANES = 128
HALO = 16
VT_ROWS = HEAD_DIM + 16
SM_ROWS = 32
PROJ_W = 512
SHIFT_SLACK = 100.0
VMEM_LIMIT = 56 * 1024 * 1024

_Q0, _K0, _V0, _RQ0, _RFF0, _RFB0, _RI0, _RG0, _ZA0, _ZR0 = (
    0, 1024, 1280, 1536, 2560, 3584, 4608, 5632, 6656, 7680)


def _sigmoid(x):
    return 0.5 * jnp.tanh(0.5 * x) + 0.5


def _cparams(sem):
    return pltpu.CompilerParams(dimension_semantics=sem, vmem_limit_bytes=VMEM_LIMIT)


def _resident(shape):
    nd = len(shape)
    return pl.BlockSpec(shape, lambda *_: (0,) * nd, pipeline_mode=pl.Buffered(1))


def _mod_kernel(c_ref, w_ref, b_ref, o_ref):
    c = c_ref[...]
    s = c * _sigmoid(c)
    o_ref[...] = jnp.dot(s.astype(BF16), w_ref[...].astype(BF16), preferred_element_type=F32) + b_ref[...]


def _mods(cond, ada_w, ada_b):
    rows, d = cond.shape
    n = ada_w.shape[1]
    tn = 1536
    return pl.pallas_call(
        _mod_kernel,
        name="mods",
        grid=(n // tn,),
        in_specs=[pl.BlockSpec((rows, d), lambda j: (0, 0)),
                  pl.BlockSpec((d, tn), lambda j: (0, j)),
                  pl.BlockSpec((1, tn), lambda j: (0, j))],
        out_specs=pl.BlockSpec((rows, tn), lambda j: (0, j)),
        out_shape=jax.ShapeDtypeStruct((rows, n), F32),
        compiler_params=_cparams(("arbitrary",)),
    )(cond, ada_w, ada_b.reshape(1, n))


def _inproj_kernel(*refs, rope, kv_t):
    if rope:
        (x_ref, mod_ref, n1_ref, w_ref, qg_ref, kg_ref, lbl_ref, cs_ref, sn_ref,
         q_ref, k_ref, v_ref, rq_ref, g_ref, ri_ref, rg_ref, gt_ref) = refs
    else:
        (x_ref, mod_ref, n1_ref, w_ref, qg_ref, kg_ref, lbl_ref,
         q_ref, k_ref, v_ref, rq_ref, g_ref, ri_ref, rg_ref, gt_ref) = refs
        cs_ref = sn_ref = None
    tm = x_ref.shape[0]

    x = x_ref[...]
    ms = jnp.mean(x * x, axis=-1, keepdims=True)
    h = x * lax.rsqrt(ms + EPS) * n1_ref[...]
    h = h * (1.0 + mod_ref[1:2, :]) + mod_ref[0:1, :]
    hb = h.astype(BF16)

    def proj(a, b):
        return jnp.dot(hb, w_ref[:, a:b], preferred_element_type=F32)

    low_head = lax.broadcasted_iota(jnp.int32, (tm, LANES), 1) < HEAD_DIM

    def headnorm(a, gain):
        sq = a * a
        s_lo = jnp.sum(jnp.where(low_head, sq, 0.0), axis=-1, keepdims=True)
        s_hi = jnp.sum(jnp.where(low_head, 0.0, sq), axis=-1, keepdims=True)
        ms = jnp.where(low_head, s_lo, s_hi) * (1.0 / HEAD_DIM)
        return a * lax.rsqrt(ms + EPS) * gain

    if rope:
        lane = lax.broadcasted_iota(jnp.int32, (tm, LANES), 1)
        first = (lane % 32) < 16
        cs = cs_ref[...]
        sn = sn_ref[...]

        def rot(a):
            sw = jnp.where(first, pltpu.roll(a, LANES - 16, 1), pltpu.roll(a, 16, 1))
            return a * cs + sw * sn
    else:
        def rot(a):
            return a

    scale = HEAD_DIM ** -0.5 * LOG2E
    for c0 in range(0, 1024, 512):
        acc = proj(_Q0 + c0, _Q0 + c0 + 512)
        for s in range(4):
            col = c0 + s * LANES
            a = headnorm(acc[:, s * LANES:(s + 1) * LANES], qg_ref[:, col:col + LANES] * scale)
            q_ref[:, col:col + LANES] = rot(a).astype(BF16)

    acc = proj(_K0, _K0 + 512)
    for s in range(2):
        col = s * LANES
        a = rot(headnorm(acc[:, col:col + LANES], kg_ref[:, col:col + LANES]))
        vv = acc[:, KV_W + col:KV_W + col + LANES]
        if kv_t:
            k_ref[col:col + LANES, :] = a.T
            v_ref[col:col + LANES, :] = vv.T
        else:
            k_ref[:, col:col + LANES] = a
            v_ref[:, col:col + LANES] = vv

    rscale = R_DK ** -0.5
    for c0 in range(0, 1024, PROJ_W):
        acc = proj(_RQ0 + c0, _RQ0 + c0 + PROJ_W)
        rq_ref[:, c0:c0 + PROJ_W] = (acc * _sigmoid(acc) * rscale).astype(BF16)

    l = lbl_ref[...]
    for d in range(2):
        a0 = l[2 * d:2 * d + 1, :]
        a1 = l[2 * d + 1:2 * d + 2, :]
        mx = jnp.maximum(a0, a1)
        e0 = jnp.exp(a0 - mx)
        e1 = jnp.exp(a1 - mx)
        lb = e0 / (e0 + e1)
        for c0 in range(0, 1024, PROJ_W):
            acc = proj(_RFF0 + d * 1024 + c0, _RFF0 + d * 1024 + c0 + PROJ_W)
            lbc = lb[:, c0:c0 + PROJ_W]
            g_ref[:, d * 1024 + c0:d * 1024 + c0 + PROJ_W] = lbc + (1.0 - lbc) * _sigmoid(acc)

    for c0 in range(0, 1024, PROJ_W):
        ri_ref[:, c0:c0 + PROJ_W] = proj(_RI0 + c0, _RI0 + c0 + PROJ_W).astype(BF16)
    for c0 in range(0, 1024, PROJ_W):
        acc = proj(_RG0 + c0, _RG0 + c0 + PROJ_W)
        rg_ref[:, c0:c0 + PROJ_W] = (acc * _sigmoid(acc)).astype(BF16)
    for c0 in range(0, 2048, PROJ_W):
        acc = proj(_ZA0 + c0, _ZA0 + c0 + PROJ_W)
        gt_ref[:, c0:c0 + PROJ_W] = _sigmoid(acc).astype(BF16)


def _inproj(x2d, mods, mod_row, seq_len, n1, w_in, qg, kg, lbl, cs, sn, tm, kv_t):
    t, d = x2d.shape
    tiles_per_seq = seq_len // tm
    rope = cs is not None

    def tok(w):
        return pl.BlockSpec((tm, w), lambda i: (i, 0))

    in_specs = [tok(d),
                pl.BlockSpec((None, 6, d), lambda i: (mod_row(i // tiles_per_seq), 0, 0)),
                _resident((1, d)),
                _resident((d, IN_W)),
                _resident((1, 1024)),
                _resident((1, KV_W)),
                _resident((4, 1024))]
    args = [x2d, mods, n1, w_in, qg, kg, lbl]
    if rope:
        in_specs += [pl.BlockSpec((tm, LANES), lambda i: (i % tiles_per_seq, 0))] * 2
        args += [cs, sn]
    out_shape = [jax.ShapeDtypeStruct((t, 1024), BF16),
                 jax.ShapeDtypeStruct((t, KV_W), F32),
                 jax.ShapeDtypeStruct((t, KV_W), F32),
                 jax.ShapeDtypeStruct((t, 1024), BF16),
                 jax.ShapeDtypeStruct((t, 2048), F32),
                 jax.ShapeDtypeStruct((t, 1024), BF16),
                 jax.ShapeDtypeStruct((t, 1024), BF16),
                 jax.ShapeDtypeStruct((t, 2048), BF16)]
    out_specs = [tok(s.shape[1]) for s in out_shape]
    if kv_t:
        kvt_shape = jax.ShapeDtypeStruct((t // seq_len, KV_W, seq_len), F32)
        kvt_spec = pl.BlockSpec((None, KV_W, tm), lambda i: (i // tiles_per_seq, 0, i % tiles_per_seq))
        out_shape[1:3] = [kvt_shape, kvt_shape]
        out_specs[1:3] = [kvt_spec, kvt_spec]
    return pl.pallas_call(
        functools.partial(_inproj_kernel, rope=rope, kv_t=kv_t),
        name="inproj_rope" if rope else "inproj",
        grid=(t // tm,),
        in_specs=in_specs, out_specs=out_specs, out_shape=out_shape,
        compiler_params=_cparams(("arbitrary",)),
    )(*args)


def _attn_kernel(*refs, n_lat, n_ctx, tq, pps, kchunk, group, kv_t):
    if n_ctx:
        q_ref, k_ref, v_ref, ck_ref, cv_ref, o_ref, kpad, vt, ref_scr, *p_bufs = refs
    else:
        q_ref, k_ref, v_ref, o_ref, kpad, vt, ref_scr, *p_bufs = refs
        ck_ref = cv_ref = None
    lk = n_lat + n_ctx
    first_step = jnp.logical_and(pl.program_id(1) == 0, pl.program_id(2) == 0)

    @pl.when(first_step)
    def _build():
        def place(dst, rows, ks):
            lane = lax.broadcasted_iota(jnp.int32, ks.shape, 1)
            lo = jnp.where(lane < HEAD_DIM, ks, 0.0)
            hi = jnp.where(lane >= HEAD_DIM, ks, 0.0)
            kpad[dst + 0, rows, :] = lo.astype(BF16)
            kpad[dst + 1, rows, :] = pltpu.roll(lo, HEAD_DIM, 1).astype(BF16)
            kpad[dst + 2, rows, :] = pltpu.roll(hi, HEAD_DIM, 1).astype(BF16)
            kpad[dst + 3, rows, :] = hi.astype(BF16)

        def place_v(s, cols_out, vtr):
            vt[2 * s, 0:HEAD_DIM, cols_out] = vtr[0:HEAD_DIM, :].astype(BF16)
            vt[2 * s + 1, 0:HEAD_DIM, cols_out] = vtr[HEAD_DIM:2 * HEAD_DIM, :].astype(BF16)

        ones_rows = jnp.where(lax.broadcasted_iota(jnp.int32, (VT_ROWS - HEAD_DIM, lk), 0) == 0,
                              1.0, 0.0).astype(BF16)
        for kvh in range(N_KV_HEADS):
            vt[kvh, HEAD_DIM:VT_ROWS, :] = ones_rows
        for s in range(2):
            cols = slice(s * LANES, (s + 1) * LANES)
            if kv_t:
                place(4 * s, slice(0, n_lat), k_ref[cols, :].T)
                place_v(s, slice(0, n_lat), v_ref[cols, :])
            else:
                place(4 * s, slice(0, n_lat), k_ref[:, cols])
                place_v(s, slice(0, n_lat), v_ref[:, cols].T)
            if n_ctx:
                place(4 * s, slice(n_lat, lk), ck_ref[cols, :].T)
                place_v(s, slice(n_lat, lk), cv_ref[cols, :])

    n_chunks = lk // kchunk
    pair0 = pl.program_id(2) * pps
    nt = (((1,), (1,)), ((), ()))
    units = [(j, parity) for j in range(pps) for parity in range(2)]

    def unit_operands(u):
        j, parity = units[u]
        kv = (pair0 + j) // 2
        return 2 * kv + parity, kv, q_ref[:, j * LANES:(j + 1) * LANES]

    def bcast_max(x8):
        return jnp.broadcast_to(jnp.max(x8, axis=0, keepdims=True), (8, tq))

    def exact_max():
        for u in range(len(units)):
            kidx, _, q2 = unit_operands(u)
            mx = jnp.full((8, tq), -jnp.inf, F32)
            for c in range(n_chunks):
                s = lax.dot_general(kpad[kidx, c * kchunk:(c + 1) * kchunk, :], q2, nt,
                                    preferred_element_type=F32)
                mx = jnp.maximum(mx, jnp.max(s.reshape(kchunk // 8, 8, tq), axis=0))
            ref_scr[u] = bcast_max(mx)

    def softmax_pass():
        worst = jnp.zeros((8, tq), F32)
        outs = []
        for u0 in range(0, len(units), group):
            us = range(u0, u0 + group)
            scores = [lax.dot_general(kpad[unit_operands(u)[0]], unit_operands(u)[2], nt,
                                      preferred_element_type=F32) for u in us]
            for u, s in zip(us, scores):
                p_scr = p_bufs[u % len(p_bufs)]
                r8 = ref_scr[u]
                mtot = jnp.full((8, tq), -jnp.inf, F32)
                for r0 in range(0, lk, SM_ROWS):
                    sb = s[r0:r0 + SM_ROWS, :].reshape(SM_ROWS // 8, 8, tq)
                    mtot = jnp.maximum(mtot, jnp.max(sb, axis=0))
                    p_scr[r0:r0 + SM_ROWS, :] = jnp.exp2(sb - r8).reshape(SM_ROWS, tq).astype(BF16)
                worst = jnp.maximum(worst, jnp.abs(bcast_max(mtot) - r8))
            for u in us:
                ot = jnp.dot(vt[unit_operands(u)[1]], p_bufs[u % len(p_bufs)][...],
                             preferred_element_type=F32)
                outs.append(ot[0:HEAD_DIM, :] * (1.0 / ot[HEAD_DIM:HEAD_DIM + 1, :]))
                if u % 2 == 1:
                    j = units[u][0]
                    o2t = jnp.concatenate(outs[-2:], axis=0)
                    o_ref[:, j * LANES:(j + 1) * LANES] = o2t.T.astype(BF16)
        return worst

    def attempt(state):
        n, _ = state

        @pl.when(n == 0)
        def _():
            ref_scr[...] = jnp.zeros(ref_scr.shape, F32)

        @pl.when(n == 1)
        def _():
            exact_max()

        worst = softmax_pass()
        return n + 1, (jnp.max(worst) > SHIFT_SLACK).astype(jnp.int32)

    lax.while_loop(lambda st: jnp.logical_or(st[0] == 0, jnp.logical_and(st[0] == 1, st[1] == 1)),
                   attempt, (jnp.int32(0), jnp.int32(0)))


def _attention(q, k, v, ck, cv, batch, seq_len, tq, pps, group, kv_t):
    t = q.shape[0]
    n_ctx = 0 if ck is None else ck.shape[2]
    lk = seq_len + n_ctx
    n_qt = seq_len // tq
    n_pp = (N_HEADS // 2) // pps
    kchunk = min(lk, 512)
    if kv_t:
        kv_spec = pl.BlockSpec((None, KV_W, seq_len), lambda b, i, p: (b, 0, 0))
    else:
        kv_spec = pl.BlockSpec((seq_len, KV_W), lambda b, i, p: (b, 0))
    in_specs = [pl.BlockSpec((tq, pps * LANES), lambda b, i, p: (b * n_qt + i, p)), kv_spec, kv_spec]
    args = [q, k, v]
    if n_ctx:
        in_specs += [pl.BlockSpec((None, KV_W, n_ctx), lambda b, i, p: (b, 0, 0))] * 2
        args += [ck, cv]
    return pl.pallas_call(
        functools.partial(_attn_kernel, n_lat=seq_len, n_ctx=n_ctx, tq=tq, pps=pps, kchunk=kchunk, group=group, kv_t=kv_t),
        name="attn_ctx" if n_ctx else "attn",
        grid=(batch, n_qt, n_pp),
        in_specs=in_specs,
        out_specs=pl.BlockSpec((tq, pps * LANES), lambda b, i, p: (b * n_qt + i, p)),
        out_shape=jax.ShapeDtypeStruct((t, 1024), BF16),
        scratch_shapes=[pltpu.VMEM((2 * N_KV_HEADS, lk, LANES), BF16),
                        pltpu.VMEM((N_KV_HEADS, VT_ROWS, lk), BF16),
                        pltpu.VMEM((2 * pps, 8, tq), F32)]
                       + [pltpu.VMEM((lk, tq), BF16)] * max(2, group),
        compiler_params=_cparams(("arbitrary", "arbitrary", "arbitrary")),
    )(*args)


def _rec_kernel(*refs, seq_len, has_s0, emit_state, hps):
    refs = list(refs)
    rq_ref, gf_ref, gb_ref, ri_ref, rg_ref, gain_ref = refs[:6]
    pos = 6
    s0f_ref = s0b_ref = sf_ref = sb_ref = None
    if has_s0:
        s0f_ref, s0b_ref = refs[pos:pos + 2]
        pos += 2
    rec_ref = refs[pos]
    pos += 1
    if emit_state:
        st_ref = refs[pos]
        sf_ref, sb_ref = st_ref.at[0], st_ref.at[1]
        pos += 1
    qd_scr, d_scr, u_scr, st_scr, a_scr = refs[pos:]

    nc = seq_len // CHUNK
    tpos = lax.broadcasted_iota(jnp.int32, (CHUNK, R_DK), 0)
    ti = lax.broadcasted_iota(jnp.int32, (CHUNK, CHUNK), 0)
    si = lax.broadcasted_iota(jnp.int32, (CHUNK, CHUNK), 1)
    nt = (((1,), (1,)), ((), ()))
    tn = (((0,), (0,)), ((), ()))
    gain = gain_ref[...]

    def cols(r, hh):
        return r.at[:, pl.ds(hh * R_DK, R_DK)]

    def rows(c):
        return pl.ds(pl.multiple_of(c * CHUNK, CHUNK), CHUNK)

    def chunk_cumsum(g, reverse):
        b = g
        for s in (1, 2, 4):
            if reverse:
                sh = pltpu.roll(b, CHUNK - s, 0)
                b = b + jnp.where(tpos < CHUNK - s, sh, 0.0)
            else:
                sh = pltpu.roll(b, s, 0)
                b = b + jnp.where(tpos >= s, sh, 0.0)
        for s in (8, 16, 32):
            if reverse:
                b = jnp.concatenate([b[:CHUNK - s] + b[s:], b[CHUNK - s:]], axis=0)
            else:
                b = jnp.concatenate([b[:s], b[s:] + b[:CHUNK - s]], axis=0)
        return b

    def ua_chunk(c, hh):
        r = rows(c)
        rq = cols(rq_ref, hh)[r, :].astype(F32)
        a = None
        ke = []
        for d, g_ref in enumerate((gf_ref, gb_ref)):
            f = cols(g_ref, hh)[r, :]
            kk = 1.0 - f
            g = jnp.log2(f)
            b = chunk_cumsum(g, reverse=(d == 1))
            tot = b[0:1, :] if d == 1 else b[CHUNK - 1:CHUNK, :]
            etot = jnp.broadcast_to(jnp.exp2(tot), (CHUNK, R_DK))
            eb = jnp.exp2(b)
            qd = (rq * eb).astype(BF16)
            kd = kk / eb
            ke.append((kd * etot).astype(BF16))
            qd_scr[hh, r, d * R_DK:(d + 1) * R_DK] = qd
            d_scr[hh, r, d * R_DK:(d + 1) * R_DK] = etot
            sc = lax.dot_general(qd, kd.astype(BF16), nt, preferred_element_type=F32)
            sc = jnp.where(si <= ti, sc, 0.0) if d == 0 else jnp.where(si >= ti, sc, 0.0)
            a = sc if a is None else a + sc
        a_scr[hh, r, :] = a.astype(BF16)
        u_scr[hh, c] = lax.dot_general(cols(ri_ref, hh)[r, :], jnp.concatenate(ke, axis=1), tn,
                                       preferred_element_type=F32)

    def ua_body(c, carry):
        for hh in range(hps):
            ua_chunk(c, hh)
        return carry

    lax.fori_loop(0, nc, ua_body, 0, unroll=min(nc, 16))

    for hh in range(hps):
        if has_s0:
            sf0 = s0f_ref[hh].T
            sb0 = s0b_ref[hh].T
        else:
            sf0 = jnp.zeros((R_DK, R_DK), F32)
            sb0 = jnp.zeros((R_DK, R_DK), F32)

        def scan_body(i, carry, hh=hh):
            sf, sb = carry
            cb = nc - 1 - i
            st_scr[hh, i, :, 0:R_DK] = sf.astype(BF16)
            st_scr[hh, cb, :, R_DK:2 * R_DK] = sb.astype(BF16)
            df = d_scr[hh, pl.ds(pl.multiple_of(i * CHUNK, CHUNK), 1), 0:R_DK]
            db = d_scr[hh, pl.ds(pl.multiple_of(cb * CHUNK, CHUNK), 1), R_DK:2 * R_DK]
            sf = df * sf + u_scr[hh, i, :, 0:R_DK]
            sb = db * sb + u_scr[hh, cb, :, R_DK:2 * R_DK]
            return sf, sb

        sf, sb = lax.fori_loop(0, nc, scan_body, (sf0, sb0), unroll=(nc <= 8))
        if emit_state:
            sf_ref[hh] = sf.T
            sb_ref[hh] = sb.T

    def o_body(c, carry):
        r = rows(c)
        for hh in range(hps):
            o = jnp.dot(a_scr[hh, r, :], cols(ri_ref, hh)[r, :], preferred_element_type=F32)
            o = o + lax.dot_general(qd_scr[hh, r, :], st_scr[hh, c], nt, preferred_element_type=F32)
            ms = jnp.mean(o * o, axis=-1, keepdims=True)
            y = o * lax.rsqrt(ms + EPS) * gain * cols(rg_ref, hh)[r, :].astype(F32)
            cols(rec_ref, hh)[r, :] = y.astype(BF16)
        return carry

    lax.fori_loop(0, nc, o_body, 0, unroll=min(nc, 32))


def _recurrence(rq, g, ri, rg, gain, s0, batch, seq_len, emit_state, hps):
    t = rq.shape[0]
    nc = seq_len // CHUNK
    has_s0 = s0 is not None
    hb = R_HEADS // hps

    def col(off):
        return pl.BlockSpec((seq_len, hps * R_DK), lambda b, h: (b, h + off))

    def state(nblk, off):
        return pl.BlockSpec((hps, R_DK, R_DK), lambda b, h: (b * nblk + off + h, 0, 0))

    in_specs = [col(0), col(0), col(hb), col(0), col(0), _resident((1, R_DK))]
    args = [rq, g, g, ri, rg, gain]
    if has_s0:
        in_specs += [state(2 * hb, 0), state(2 * hb, hb)]
        args += [s0, s0]
    out_shape = [jax.ShapeDtypeStruct((t, 1024), BF16)]
    out_specs = [col(0)]
    if emit_state:
        assert hps == R_HEADS
        out_shape += [jax.ShapeDtypeStruct((batch, 2, R_HEADS, R_DK, R_DK), F32)]
        out_specs += [pl.BlockSpec((None, 2, R_HEADS, R_DK, R_DK), lambda b, h: (b, 0, 0, 0, 0))]
    scratch = [pltpu.VMEM((hps, seq_len, 2 * R_DK), BF16),
               pltpu.VMEM((hps, seq_len, 2 * R_DK), F32),
               pltpu.VMEM((hps, nc, R_DK, 2 * R_DK), F32),
               pltpu.VMEM((hps, nc, R_DK, 2 * R_DK), BF16),
               pltpu.VMEM((hps, seq_len, CHUNK), BF16)]
    return pl.pallas_call(
        functools.partial(_rec_kernel, seq_len=seq_len, has_s0=has_s0, emit_state=emit_state, hps=hps),
        name="rec_s0" if has_s0 else "rec",
        grid=(batch, hb),
        in_specs=in_specs, out_specs=out_specs, out_shape=out_shape,
        scratch_shapes=scratch,
        compiler_params=_cparams(("arbitrary", "arbitrary")),
    )(*args)


def _mffn_kernel(x_ref, a_ref, r_ref, g_ref, xp_ref, ap_ref, rp_ref, gp_ref, xn_ref, an_ref, rn_ref, gn_ref,
                 wo_ref, n2_ref, wup_ref, cw_ref, cb_ref, wdn_ref, mod_ref, fn_ref,
                 y_ref, hbuf, act_scr, x1_scr, *, tiles_per_seq, ffc):
    tm = x_ref.shape[0]
    ts = pl.program_id(0) % tiles_per_seq

    def merge_rows(x, a, r, g):
        m = g[:, 0:D_MODEL].astype(F32) * a.astype(F32) + g[:, D_MODEL:2 * D_MODEL].astype(F32) * r.astype(F32)
        out = jnp.dot(m.astype(BF16), wo_ref[...], preferred_element_type=F32)
        x1 = x + mod_ref[2:3, :] * out
        ms = jnp.mean(x1 * x1, axis=-1, keepdims=True)
        h2 = x1 * lax.rsqrt(ms + EPS) * n2_ref[...]
        h2 = h2 * (1.0 + mod_ref[4:5, :]) + mod_ref[3:4, :]
        return x1, h2.astype(BF16)

    x1, h2 = merge_rows(x_ref[...], a_ref[...], r_ref[...], g_ref[...])
    x1_scr[...] = x1
    hbuf[HALO:HALO + tm, :] = h2
    _, hh = merge_rows(jnp.concatenate([xp_ref[...], xn_ref[...]], axis=0),
                       jnp.concatenate([ap_ref[...], an_ref[...]], axis=0),
                       jnp.concatenate([rp_ref[...], rn_ref[...]], axis=0),
                       jnp.concatenate([gp_ref[...], gn_ref[...]], axis=0))
    zero = jnp.zeros((HALO, D_MODEL), BF16)
    hbuf[0:HALO, :] = jnp.where(ts == 0, zero, hh[0:HALO, :])
    hbuf[HALO + tm:HALO + tm + HALO, :] = jnp.where(ts == tiles_per_seq - 1, zero, hh[HALO:2 * HALO, :])
    hb = hbuf[...]
    mrows = tm + 2 * HALO

    def conv(col):
        u = jnp.dot(hb, wup_ref[:, col:col + ffc], preferred_element_type=F32)
        up = pltpu.roll(u, 1, 0)[HALO:HALO + tm, :]
        un = pltpu.roll(u, mrows - 1, 0)[HALO:HALO + tm, :]
        uc = u[HALO:HALO + tm, :]
        w = cw_ref[:, col:col + ffc]
        return up * w[0:1, :] + uc * w[1:2, :] + un * w[2:3, :] + cb_ref[:, col:col + ffc]

    for c0 in range(0, D_FF, ffc):
        a = conv(c0)
        b = conv(D_FF + c0)
        act_scr[:, c0:c0 + ffc] = (a * _sigmoid(a) * b).astype(BF16)

    f = jnp.dot(act_scr[...], wdn_ref[...], preferred_element_type=F32)
    x2 = x1_scr[...] + mod_ref[5:6, :] * f
    ms = jnp.mean(x2 * x2, axis=-1, keepdims=True)
    y_ref[...] = x2 * lax.rsqrt(ms + EPS) * fn_ref[...]


def _merge_ffn(x2d, attn, rec, gates, w, mods, mod_row, seq_len, tm):
    t, d = x2d.shape
    tiles_per_seq = seq_len // tm
    hb = tm // HALO
    n_hb = t // HALO

    def main(width):
        return pl.BlockSpec((tm, width), lambda i: (i, 0))

    def prev(width):
        return pl.BlockSpec((HALO, width), lambda i: (jnp.maximum(i * hb - 1, 0), 0))

    def nxt(width):
        return pl.BlockSpec((HALO, width), lambda i: (jnp.minimum((i + 1) * hb, n_hb - 1), 0))

    toks = [x2d, attn, rec, gates]
    widths = [d, d, d, 2 * d]
    return pl.pallas_call(
        functools.partial(_mffn_kernel, tiles_per_seq=tiles_per_seq, ffc=256),
        name="merge_ffn",
        grid=(t // tm,),
        in_specs=([main(wd) for wd in widths] + [prev(wd) for wd in widths] + [nxt(wd) for wd in widths]
                  + [_resident((d, d)), _resident((1, d)),
                     _resident((d, 2 * D_FF)), _resident((3, 2 * D_FF)), _resident((1, 2 * D_FF)),
                     _resident((D_FF, d)),
                     pl.BlockSpec((None, 6, d), lambda i: (mod_row(i // tiles_per_seq), 0, 0)),
                     _resident((1, d))]),
        out_specs=pl.BlockSpec((tm, d), lambda i: (i, 0)),
        out_shape=jax.ShapeDtypeStruct((t, d), F32),
        scratch_shapes=[pltpu.VMEM((tm + 2 * HALO, d), BF16),
                        pltpu.VMEM((tm, D_FF), BF16),
                        pltpu.VMEM((tm, d), F32)],
        compiler_params=_cparams(("arbitrary",)),
    )(*toks, *toks, *toks, w["w_o"], w["n2"], w["w_up"], w["conv_w"], w["conv_b"], w["w_down"], mods, w["fnorm"])


def _rope_tables(n_tokens):
    rows = n_tokens // GRID_W
    row = jnp.repeat(jnp.arange(rows, dtype=F32), GRID_W)
    colp = jnp.tile(jnp.arange(GRID_W, dtype=F32), rows)
    half = HEAD_DIM // 2
    inv_freq = 1.0 / (ROPE_THETA ** (jnp.arange(0, half, 2, dtype=F32) / half))
    ar = row[:, None] * inv_freq
    ac = colp[:, None] * inv_freq
    ang = jnp.concatenate([ar, ar, ac, ac], axis=-1)
    sign = jnp.asarray(np.tile(np.repeat(np.array([-1.0, 1.0], np.float32), 16), 2))
    cs = jnp.cos(ang)
    sn = jnp.sin(ang) * sign
    return jnp.tile(cs, (1, 2)), jnp.tile(sn, (1, 2))


def _group(x, mods, mod_row, w, ctx, rope_tabs, tq, pps, agroup, hps, tm_in, tm_ffn, emit_state):
    batch, seq_len, d = x.shape
    x2d = x.reshape(batch * seq_len, d)
    cs, sn = rope_tabs if rope_tabs is not None else (None, None)
    kv_t = emit_state
    q, k, v, rq, g, ri, rg, gates = _inproj(
        x2d, mods, mod_row, seq_len, w["n1"], w["w_in"], w["qg"], w["kg"], w["lbl"], cs, sn, tm=tm_in,
        kv_t=kv_t)
    ck, cv, s0 = ctx
    attn = _attention(q, k, v, ck, cv, batch, seq_len, tq, pps, agroup, kv_t)
    rec_out = _recurrence(rq, g, ri, rg, w["rgain"], s0, batch, seq_len, emit_state, hps)
    rec = rec_out[0]
    y = _merge_ffn(x2d, attn, rec, gates, w, mods, mod_row, seq_len, tm=tm_ffn)
    return y.reshape(batch, seq_len, d), k, v, rec_out[1:]


def kernel(x_prompt, x_sample, c, cache_k, cache_v, state_hgrn, c_ctx, ada_w, ada_b, norm1, norm2,
           w_in, q_norm, k_norm, hgrn_lb_logits, hgrn_norm, w_o, w_up, conv_w, conv_b, w_down, final_norm):
    bp, lp, d = x_prompt.shape
    bs, ls, _ = x_sample.shape
    n_ctx = cache_k.shape[2]

    cond = jnp.zeros((16, d), F32).at[0].set(c_ctx).at[1:1 + bs].set(c)
    mods = _mods(cond, ada_w[0], ada_b[0]).reshape(16, 6, d)

    w = dict(
        n1=norm1[0].reshape(1, d), n2=norm2[0].reshape(1, d), fnorm=final_norm.reshape(1, d),
        w_in=w_in[0].astype(BF16), w_o=w_o[0].astype(BF16),
        w_up=w_up[0].astype(BF16), w_down=w_down[0].astype(BF16),
        conv_w=conv_w[0], conv_b=conv_b[0].reshape(1, 2 * D_FF),
        qg=jnp.tile(q_norm[0], N_HEADS).reshape(1, 1024),
        kg=jnp.tile(k_norm[0], N_KV_HEADS).reshape(1, KV_W),
        lbl=hgrn_lb_logits.reshape(4, 1024),
        rgain=hgrn_norm[0].reshape(1, R_DK),
    )

    y_p, k_p, v_p, st = _group(x_prompt, mods, lambda b: 0, w, (None, None, None), None,
                               tq=lp, pps=8, agroup=8, hps=R_HEADS, tm_in=lp, tm_ffn=lp, emit_state=True)
    def feature_major(cache):
        return jnp.transpose(cache[:, 0], (0, 2, 3, 1)).reshape(bs, KV_W, n_ctx)

    ctx = (feature_major(cache_k), feature_major(cache_v),
           state_hgrn[:, 0].reshape(bs * 2 * R_HEADS, R_DK, R_DK))
    y_s, _, _, _ = _group(x_sample, mods, lambda b: b + 1, w, ctx, _rope_tables(ls),
                          tq=512, pps=4, agroup=1, hps=1, tm_in=512, tm_ffn=512, emit_state=False)

    new_state = st[0][:, None]
    def position_major(kvt):
        return jnp.transpose(kvt.reshape(bp, N_KV_HEADS, HEAD_DIM, lp), (0, 3, 1, 2))[:, None]

    new_k = position_major(k_p)
    new_v = position_major(v_p)
    return (y_p, y_s, new_k, new_v, new_state)
```

```python
import functools

import jax
import jax.numpy as jnp
import numpy as np
from jax import lax
from jax.experimental import pallas as pl
from jax.experimental.pallas import tpu as pltpu

F32 = jnp.float32
BF16 = jnp.bfloat16

D_MODEL = 1024
GRID_W = 64
HEAD_DIM = 64
N_HEADS = 16
N_KV_HEADS = 4
KV_W = N_KV_HEADS * HEAD_DIM
ROPE_THETA = 10000.0
R_DK = 128
R_HEADS = 8
CHUNK = 64
D_FF = 2816
EPS = 1e-6
LOG2E = 1.4426950408889634
IN_W = 8704

LANES = 128
HALO = 16
VT_ROWS = HEAD_DIM + 16
SM_ROWS = 32
PROJ_W = 512
SHIFT_SLACK = 100.0
VMEM_LIMIT = 56 * 1024 * 1024

_Q0, _K0, _V0, _RQ0, _RFF0, _RFB0, _RI0, _RG0, _ZA0, _ZR0 = (
    0, 1024, 1280, 1536, 2560, 3584, 4608, 5632, 6656, 7680)


def _sigmoid(x):
    return 0.5 * jnp.tanh(0.5 * x) + 0.5


def _cparams(sem):
    return pltpu.CompilerParams(dimension_semantics=sem, vmem_limit_bytes=VMEM_LIMIT)


def _resident(shape):
    nd = len(shape)
    return pl.BlockSpec(shape, lambda *_: (0,) * nd, pipeline_mode=pl.Buffered(1))


def _mod_kernel(c_ref, w_ref, b_ref, o_ref):
    c = c_ref[...]
    s = c * _sigmoid(c)
    o_ref[...] = jnp.dot(s.astype(BF16), w_ref[...].astype(BF16), preferred_element_type=F32) + b_ref[...]


def _mods(cond, ada_w, ada_b):
    rows, d = cond.shape
    n = ada_w.shape[1]
    tn = 1536
    return pl.pallas_call(
        _mod_kernel,
        name="mods",
        grid=(n // tn,),
        in_specs=[pl.BlockSpec((rows, d), lambda j: (0, 0)),
                  pl.BlockSpec((d, tn), lambda j: (0, j)),
                  pl.BlockSpec((1, tn), lambda j: (0, j))],
        out_specs=pl.BlockSpec((rows, tn), lambda j: (0, j)),
        out_shape=jax.ShapeDtypeStruct((rows, n), F32),
        compiler_params=_cparams(("arbitrary",)),
    )(cond, ada_w, ada_b.reshape(1, n))


def _inproj_kernel(*refs, rope, kv_t):
    if rope:
        (x_ref, mod_ref, n1_ref, w_ref, qg_ref, kg_ref, lbl_ref, cs_ref, sn_ref,
         q_ref, k_ref, v_ref, rq_ref, g_ref, ri_ref, rg_ref, gt_ref) = refs
    else:
        (x_ref, mod_ref, n1_ref, w_ref, qg_ref, kg_ref, lbl_ref,
         q_ref, k_ref, v_ref, rq_ref, g_ref, ri_ref, rg_ref, gt_ref) = refs
        cs_ref = sn_ref = None
    tm = x_ref.shape[0]

    x = x_ref[...]
    ms = jnp.mean(x * x, axis=-1, keepdims=True)
    h = x * lax.rsqrt(ms + EPS) * n1_ref[...]
    h = h * (1.0 + mod_ref[1:2, :]) + mod_ref[0:1, :]
    hb = h.astype(BF16)

    def proj(a, b):
        return jnp.dot(hb, w_ref[:, a:b], preferred_element_type=F32)

    low_head = lax.broadcasted_iota(jnp.int32, (tm, LANES), 1) < HEAD_DIM

    def headnorm(a, gain):
        sq = a * a
        s_lo = jnp.sum(jnp.where(low_head, sq, 0.0), axis=-1, keepdims=True)
        s_hi = jnp.sum(jnp.where(low_head, 0.0, sq), axis=-1, keepdims=True)
        ms = jnp.where(low_head, s_lo, s_hi) * (1.0 / HEAD_DIM)
        return a * lax.rsqrt(ms + EPS) * gain

    if rope:
        lane = lax.broadcasted_iota(jnp.int32, (tm, LANES), 1)
        first = (lane % 32) < 16
        cs = cs_ref[...]
        sn = sn_ref[...]

        def rot(a):
            sw = jnp.where(first, pltpu.roll(a, LANES - 16, 1), pltpu.roll(a, 16, 1))
            return a * cs + sw * sn
    else:
        def rot(a):
            return a

    scale = HEAD_DIM ** -0.5 * LOG2E
    for c0 in range(0, 1024, 512):
        acc = proj(_Q0 + c0, _Q0 + c0 + 512)
        for s in range(4):
            col = c0 + s * LANES
            a = headnorm(acc[:, s * LANES:(s + 1) * LANES], qg_ref[:, col:col + LANES] * scale)
            q_ref[:, col:col + LANES] = rot(a).astype(BF16)

    acc = proj(_K0, _K0 + 512)
    for s in range(2):
        col = s * LANES
        a = rot(headnorm(acc[:, col:col + LANES], kg_ref[:, col:col + LANES]))
        vv = acc[:, KV_W + col:KV_W + col + LANES]
        if kv_t:
            k_ref[col:col + LANES, :] = a.T
            v_ref[col:col + LANES, :] = vv.T
        else:
            k_ref[:, col:col + LANES] = a
            v_ref[:, col:col + LANES] = vv

    rscale = R_DK ** -0.5
    for c0 in range(0, 1024, PROJ_W):
        acc = proj(_RQ0 + c0, _RQ0 + c0 + PROJ_W)
        rq_ref[:, c0:c0 + PROJ_W] = (acc * _sigmoid(acc) * rscale).astype(BF16)

    l = lbl_ref[...]
    for d in range(2):
        a0 = l[2 * d:2 * d + 1, :]
        a1 = l[2 * d + 1:2 * d + 2, :]
        mx = jnp.maximum(a0, a1)
        e0 = jnp.exp(a0 - mx)
        e1 = jnp.exp(a1 - mx)
        lb = e0 / (e0 + e1)
        for c0 in range(0, 1024, PROJ_W):
            acc = proj(_RFF0 + d * 1024 + c0, _RFF0 + d * 1024 + c0 + PROJ_W)
            lbc = lb[:, c0:c0 + PROJ_W]
            g_ref[:, d * 1024 + c0:d * 1024 + c0 + PROJ_W] = lbc + (1.0 - lbc) * _sigmoid(acc)

    for c0 in range(0, 1024, PROJ_W):
        ri_ref[:, c0:c0 + PROJ_W] = proj(_RI0 + c0, _RI0 + c0 + PROJ_W).astype(BF16)
    for c0 in range(0, 1024, PROJ_W):
        acc = proj(_RG0 + c0, _RG0 + c0 + PROJ_W)
        rg_ref[:, c0:c0 + PROJ_W] = (acc * _sigmoid(acc)).astype(BF16)
    for c0 in range(0, 2048, PROJ_W):
        acc = proj(_ZA0 + c0, _ZA0 + c0 + PROJ_W)
        gt_ref[:, c0:c0 + PROJ_W] = _sigmoid(acc).astype(BF16)


def _inproj(x2d, mods, mod_row, seq_len, n1, w_in, qg, kg, lbl, cs, sn, tm, kv_t):
    t, d = x2d.shape
    tiles_per_seq = seq_len // tm
    rope = cs is not None

    def tok(w):
        return pl.BlockSpec((tm, w), lambda i: (i, 0))

    in_specs = [tok(d),
                pl.BlockSpec((None, 6, d), lambda i: (mod_row(i // tiles_per_seq), 0, 0)),
                _resident((1, d)),
                _resident((d, IN_W)),
                _resident((1, 1024)),
                _resident((1, KV_W)),
                _resident((4, 1024))]
    args = [x2d, mods, n1, w_in, qg, kg, lbl]
    if rope:
        in_specs += [pl.BlockSpec((tm, LANES), lambda i: (i % tiles_per_seq, 0))] * 2
        args += [cs, sn]
    out_shape = [jax.ShapeDtypeStruct((t, 1024), BF16),
                 jax.ShapeDtypeStruct((t, KV_W), F32),
                 jax.ShapeDtypeStruct((t, KV_W), F32),
                 jax.ShapeDtypeStruct((t, 1024), BF16),
                 jax.ShapeDtypeStruct((t, 2048), F32),
                 jax.ShapeDtypeStruct((t, 1024), BF16),
                 jax.ShapeDtypeStruct((t, 1024), BF16),
                 jax.ShapeDtypeStruct((t, 2048), BF16)]
    out_specs = [tok(s.shape[1]) for s in out_shape]
    if kv_t:
        kvt_shape = jax.ShapeDtypeStruct((t // seq_len, KV_W, seq_len), F32)
        kvt_spec = pl.BlockSpec((None, KV_W, tm), lambda i: (i // tiles_per_seq, 0, i % tiles_per_seq))
        out_shape[1:3] = [kvt_shape, kvt_shape]
        out_specs[1:3] = [kvt_spec, kvt_spec]
    return pl.pallas_call(
        functools.partial(_inproj_kernel, rope=rope, kv_t=kv_t),
        name="inproj_rope" if rope else "inproj",
        grid=(t // tm,),
        in_specs=in_specs, out_specs=out_specs, out_shape=out_shape,
        compiler_params=_cparams(("arbitrary",)),
    )(*args)


def _attn_kernel(*refs, n_lat, n_ctx, tq, pps, kchunk, group, kv_t):
    if n_ctx:
        q_ref, k_ref, v_ref, ck_ref, cv_ref, o_ref, kpad, vt, ref_scr, *p_bufs = refs
    else:
        q_ref, k_ref, v_ref, o_ref, kpad, vt, ref_scr, *p_bufs = refs
        ck_ref = cv_ref = None
    lk = n_lat + n_ctx
    first_step = jnp.logical_and(pl.program_id(1) == 0, pl.program_id(2) == 0)

    @pl.when(first_step)
    def _build():
        def place(dst, rows, ks):
            lane = lax.broadcasted_iota(jnp.int32, ks.shape, 1)
            lo = jnp.where(lane < HEAD_DIM, ks, 0.0)
            hi = jnp.where(lane >= HEAD_DIM, ks, 0.0)
            kpad[dst + 0, rows, :] = lo.astype(BF16)
            kpad[dst + 1, rows, :] = pltpu.roll(lo, HEAD_DIM, 1).astype(BF16)
            kpad[dst + 2, rows, :] = pltpu.roll(hi, HEAD_DIM, 1).astype(BF16)
            kpad[dst + 3, rows, :] = hi.astype(BF16)

        def place_v(s, cols_out, vtr):
            vt[2 * s, 0:HEAD_DIM, cols_out] = vtr[0:HEAD_DIM, :].astype(BF16)
            vt[2 * s + 1, 0:HEAD_DIM, cols_out] = vtr[HEAD_DIM:2 * HEAD_DIM, :].astype(BF16)

        ones_rows = jnp.where(lax.broadcasted_iota(jnp.int32, (VT_ROWS - HEAD_DIM, lk), 0) == 0,
                              1.0, 0.0).astype(BF16)
        for kvh in range(N_KV_HEADS):
            vt[kvh, HEAD_DIM:VT_ROWS, :] = ones_rows
        for s in range(2):
            cols = slice(s * LANES, (s + 1) * LANES)
            if kv_t:
                place(4 * s, slice(0, n_lat), k_ref[cols, :].T)
                place_v(s, slice(0, n_lat), v_ref[cols, :])
            else:
                place(4 * s, slice(0, n_lat), k_ref[:, cols])
                place_v(s, slice(0, n_lat), v_ref[:, cols].T)
            if n_ctx:
                place(4 * s, slice(n_lat, lk), ck_ref[cols, :].T)
                place_v(s, slice(n_lat, lk), cv_ref[cols, :])

    n_chunks = lk // kchunk
    pair0 = pl.program_id(2) * pps
    nt = (((1,), (1,)), ((), ()))
    units = [(j, parity) for j in range(pps) for parity in range(2)]

    def unit_operands(u):
        j, parity = units[u]
        kv = (pair0 + j) // 2
        return 2 * kv + parity, kv, q_ref[:, j * LANES:(j + 1) * LANES]

    def bcast_max(x8):
        return jnp.broadcast_to(jnp.max(x8, axis=0, keepdims=True), (8, tq))

    def exact_max():
        for u in range(len(units)):
            kidx, _, q2 = unit_operands(u)
            mx = jnp.full((8, tq), -jnp.inf, F32)
            for c in range(n_chunks):
                s = lax.dot_general(kpad[kidx, c * kchunk:(c + 1) * kchunk, :], q2, nt,
                                    preferred_element_type=F32)
                mx = jnp.maximum(mx, jnp.max(s.reshape(kchunk // 8, 8, tq), axis=0))
            ref_scr[u] = bcast_max(mx)

    def softmax_pass():
        worst = jnp.zeros((8, tq), F32)
        outs = []
        for u0 in range(0, len(units), group):
            us = range(u0, u0 + group)
            scores = [lax.dot_general(kpad[unit_operands(u)[0]], unit_operands(u)[2], nt,
                                      preferred_element_type=F32) for u in us]
            for u, s in zip(us, scores):
                p_scr = p_bufs[u % len(p_bufs)]
                r8 = ref_scr[u]
                mtot = jnp.full((8, tq), -jnp.inf, F32)
                for r0 in range(0, lk, SM_ROWS):
                    sb = s[r0:r0 + SM_ROWS, :].reshape(SM_ROWS // 8, 8, tq)
                    mtot = jnp.maximum(mtot, jnp.max(sb, axis=0))
                    p_scr[r0:r0 + SM_ROWS, :] = jnp.exp2(sb - r8).reshape(SM_ROWS, tq).astype(BF16)
                worst = jnp.maximum(worst, jnp.abs(bcast_max(mtot) - r8))
            for u in us:
                ot = jnp.dot(vt[unit_operands(u)[1]], p_bufs[u % len(p_bufs)][...],
                             preferred_element_type=F32)
                outs.append(ot[0:HEAD_DIM, :] * (1.0 / ot[HEAD_DIM:HEAD_DIM + 1, :]))
                if u % 2 == 1:
                    j = units[u][0]
                    o2t = jnp.concatenate(outs[-2:], axis=0)
                    o_ref[:, j * LANES:(j + 1) * LANES] = o2t.T.astype(BF16)
        return worst

    def attempt(state):
        n, _ = state

        @pl.when(n == 0)
        def _():
            ref_scr[...] = jnp.zeros(ref_scr.shape, F32)

        @pl.when(n == 1)
        def _():
            exact_max()

        worst = softmax_pass()
        return n + 1, (jnp.max(worst) > SHIFT_SLACK).astype(jnp.int32)

    lax.while_loop(lambda st: jnp.logical_or(st[0] == 0, jnp.logical_and(st[0] == 1, st[1] == 1)),
                   attempt, (jnp.int32(0), jnp.int32(0)))


def _attention(q, k, v, ck, cv, batch, seq_len, tq, pps, group, kv_t):
    t = q.shape[0]
    n_ctx = 0 if ck is None else ck.shape[2]
    lk = seq_len + n_ctx
    n_qt = seq_len // tq
    n_pp = (N_HEADS // 2) // pps
    kchunk = min(lk, 512)
    if kv_t:
        kv_spec = pl.BlockSpec((None, KV_W, seq_len), lambda b, i, p: (b, 0, 0))
    else:
        kv_spec = pl.BlockSpec((seq_len, KV_W), lambda b, i, p: (b, 0))
    in_specs = [pl.BlockSpec((tq, pps * LANES), lambda b, i, p: (b * n_qt + i, p)), kv_spec, kv_spec]
    args = [q, k, v]
    if n_ctx:
        in_specs += [pl.BlockSpec((None, KV_W, n_ctx), lambda b, i, p: (b, 0, 0))] * 2
        args += [ck, cv]
    return pl.pallas_call(
        functools.partial(_attn_kernel, n_lat=seq_len, n_ctx=n_ctx, tq=tq, pps=pps, kchunk=kchunk, group=group, kv_t=kv_t),
        name="attn_ctx" if n_ctx else "attn",
        grid=(batch, n_qt, n_pp),
        in_specs=in_specs,
        out_specs=pl.BlockSpec((tq, pps * LANES), lambda b, i, p: (b * n_qt + i, p)),
        out_shape=jax.ShapeDtypeStruct((t, 1024), BF16),
        scratch_shapes=[pltpu.VMEM((2 * N_KV_HEADS, lk, LANES), BF16),
                        pltpu.VMEM((N_KV_HEADS, VT_ROWS, lk), BF16),
                        pltpu.VMEM((2 * pps, 8, tq), F32)]
                       + [pltpu.VMEM((lk, tq), BF16)] * max(2, group),
        compiler_params=_cparams(("arbitrary", "arbitrary", "arbitrary")),
    )(*args)


def _rec_kernel(*refs, seq_len, has_s0, emit_state, hps):
    refs = list(refs)
    rq_ref, gf_ref, gb_ref, ri_ref, rg_ref, gain_ref = refs[:6]
    pos = 6
    s0f_ref = s0b_ref = sf_ref = sb_ref = None
    if has_s0:
        s0f_ref, s0b_ref = refs[pos:pos + 2]
        pos += 2
    rec_ref = refs[pos]
    pos += 1
    if emit_state:
        st_ref = refs[pos]
        sf_ref, sb_ref = st_ref.at[0], st_ref.at[1]
        pos += 1
    qd_scr, d_scr, u_scr, st_scr, a_scr = refs[pos:]

    nc = seq_len // CHUNK
    tpos = lax.broadcasted_iota(jnp.int32, (CHUNK, R_DK), 0)
    ti = lax.broadcasted_iota(jnp.int32, (CHUNK, CHUNK), 0)
    si = lax.broadcasted_iota(jnp.int32, (CHUNK, CHUNK), 1)
    nt = (((1,), (1,)), ((), ()))
    tn = (((0,), (0,)), ((), ()))
    gain = gain_ref[...]

    def cols(r, hh):
        return r.at[:, pl.ds(hh * R_DK, R_DK)]

    def rows(c):
        return pl.ds(pl.multiple_of(c * CHUNK, CHUNK), CHUNK)

    def chunk_cumsum(g, reverse):
        b = g
        for s in (1, 2, 4):
            if reverse:
                sh = pltpu.roll(b, CHUNK - s, 0)
                b = b + jnp.where(tpos < CHUNK - s, sh, 0.0)
            else:
                sh = pltpu.roll(b, s, 0)
                b = b + jnp.where(tpos >= s, sh, 0.0)
        for s in (8, 16, 32):
            if reverse:
                b = jnp.concatenate([b[:CHUNK - s] + b[s:], b[CHUNK - s:]], axis=0)
            else:
                b = jnp.concatenate([b[:s], b[s:] + b[:CHUNK - s]], axis=0)
        return b

    def ua_chunk(c, hh):
        r = rows(c)
        rq = cols(rq_ref, hh)[r, :].astype(F32)
        a = None
        ke = []
        for d, g_ref in enumerate((gf_ref, gb_ref)):
            f = cols(g_ref, hh)[r, :]
            kk = 1.0 - f
            g = jnp.log2(f)
            b = chunk_cumsum(g, reverse=(d == 1))
            tot = b[0:1, :] if d == 1 else b[CHUNK - 1:CHUNK, :]
            etot = jnp.broadcast_to(jnp.exp2(tot), (CHUNK, R_DK))
            eb = jnp.exp2(b)
            qd = (rq * eb).astype(BF16)
            kd = kk / eb
            ke.append((kd * etot).astype(BF16))
            qd_scr[hh, r, d * R_DK:(d + 1) * R_DK] = qd
            d_scr[hh, r, d * R_DK:(d + 1) * R_DK] = etot
            sc = lax.dot_general(qd, kd.astype(BF16), nt, preferred_element_type=F32)
            sc = jnp.where(si <= ti, sc, 0.0) if d == 0 else jnp.where(si >= ti, sc, 0.0)
            a = sc if a is None else a + sc
        a_scr[hh, r, :] = a.astype(BF16)
        u_scr[hh, c] = lax.dot_general(cols(ri_ref, hh)[r, :], jnp.concatenate(ke, axis=1), tn,
                                       preferred_element_type=F32)

    def ua_body(c, carry):
        for hh in range(hps):
            ua_chunk(c, hh)
        return carry

    lax.fori_loop(0, nc, ua_body, 0, unroll=min(nc, 32))

    for hh in range(hps):
        if has_s0:
            sf0 = s0f_ref[hh].T
            sb0 = s0b_ref[hh].T
        else:
            sf0 = jnp.zeros((R_DK, R_DK), F32)
            sb0 = jnp.zeros((R_DK, R_DK), F32)

        def scan_body(i, carry, hh=hh):
            sf, sb = carry
            cb = nc - 1 - i
            st_scr[hh, i, :, 0:R_DK] = sf.astype(BF16)
            st_scr[hh, cb, :, R_DK:2 * R_DK] = sb.astype(BF16)
            df = d_scr[hh, pl.ds(pl.multiple_of(i * CHUNK, CHUNK), 1), 0:R_DK]
            db = d_scr[hh, pl.ds(pl.multiple_of(cb * CHUNK, CHUNK), 1), R_DK:2 * R_DK]
            sf = df * sf + u_scr[hh, i, :, 0:R_DK]
            sb = db * sb + u_scr[hh, cb, :, R_DK:2 * R_DK]
            return sf, sb

        sf, sb = lax.fori_loop(0, nc, scan_body, (sf0, sb0), unroll=(nc <= 8))
        if emit_state:
            sf_ref[hh] = sf.T
            sb_ref[hh] = sb.T

    def o_body(c, carry):
        r = rows(c)
        for hh in range(hps):
            o = jnp.dot(a_scr[hh, r, :], cols(ri_ref, hh)[r, :], preferred_element_type=F32)
            o = o + lax.dot_general(qd_scr[hh, r, :], st_scr[hh, c], nt, preferred_element_type=F32)
            ms = jnp.mean(o * o, axis=-1, keepdims=True)
            y = o * lax.rsqrt(ms + EPS) * gain * cols(rg_ref, hh)[r, :].astype(F32)
            cols(rec_ref, hh)[r, :] = y.astype(BF16)
        return carry

    lax.fori_loop(0, nc, o_body, 0, unroll=min(nc, 32))


def _recurrence(rq, g, ri, rg, gain, s0, batch, seq_len, emit_state, hps):
    t = rq.shape[0]
    nc = seq_len // CHUNK
    has_s0 = s0 is not None
    hb = R_HEADS // hps

    def col(off):
        return pl.BlockSpec((seq_len, hps * R_DK), lambda b, h: (b, h + off))

    def state(nblk, off):
        return pl.BlockSpec((hps, R_DK, R_DK), lambda b, h: (b * nblk + off + h, 0, 0))

    in_specs = [col(0), col(0), col(hb), col(0), col(0), _resident((1, R_DK))]
    args = [rq, g, g, ri, rg, gain]
    if has_s0:
        in_specs += [state(2 * hb, 0), state(2 * hb, hb)]
        args += [s0, s0]
    out_shape = [jax.ShapeDtypeStruct((t, 1024), BF16)]
    out_specs = [col(0)]
    if emit_state:
        assert hps == R_HEADS
        out_shape += [jax.ShapeDtypeStruct((batch, 2, R_HEADS, R_DK, R_DK), F32)]
        out_specs += [pl.BlockSpec((None, 2, R_HEADS, R_DK, R_DK), lambda b, h: (b, 0, 0, 0, 0))]
    scratch = [pltpu.VMEM((hps, seq_len, 2 * R_DK), BF16),
               pltpu.VMEM((hps, seq_len, 2 * R_DK), F32),
               pltpu.VMEM((hps, nc, R_DK, 2 * R_DK), F32),
               pltpu.VMEM((hps, nc, R_DK, 2 * R_DK), BF16),
               pltpu.VMEM((hps, seq_len, CHUNK), BF16)]
    return pl.pallas_call(
        functools.partial(_rec_kernel, seq_len=seq_len, has_s0=has_s0, emit_state=emit_state, hps=hps),
        name="rec_s0" if has_s0 else "rec",
        grid=(batch, hb),
        in_specs=in_specs, out_specs=out_specs, out_shape=out_shape,
        scratch_shapes=scratch,
        compiler_params=_cparams(("arbitrary", "arbitrary")),
    )(*args)


def _mffn_kernel(x_ref, a_ref, r_ref, g_ref, xp_ref, ap_ref, rp_ref, gp_ref, xn_ref, an_ref, rn_ref, gn_ref,
                 wo_ref, n2_ref, wup_ref, cw_ref, cb_ref, wdn_ref, mod_ref, fn_ref,
                 y_ref, hbuf, act_scr, x1_scr, *, tiles_per_seq, ffc):
    tm = x_ref.shape[0]
    ts = pl.program_id(0) % tiles_per_seq

    def merge_rows(x, a, r, g):
        m = g[:, 0:D_MODEL].astype(F32) * a.astype(F32) + g[:, D_MODEL:2 * D_MODEL].astype(F32) * r.astype(F32)
        out = jnp.dot(m.astype(BF16), wo_ref[...], preferred_element_type=F32)
        x1 = x + mod_ref[2:3, :] * out
        ms = jnp.mean(x1 * x1, axis=-1, keepdims=True)
        h2 = x1 * lax.rsqrt(ms + EPS) * n2_ref[...]
        h2 = h2 * (1.0 + mod_ref[4:5, :]) + mod_ref[3:4, :]
        return x1, h2.astype(BF16)

    x1, h2 = merge_rows(x_ref[...], a_ref[...], r_ref[...], g_ref[...])
    x1_scr[...] = x1
    hbuf[HALO:HALO + tm, :] = h2
    _, hh = merge_rows(jnp.concatenate([xp_ref[...], xn_ref[...]], axis=0),
                       jnp.concatenate([ap_ref[...], an_ref[...]], axis=0),
                       jnp.concatenate([rp_ref[...], rn_ref[...]], axis=0),
                       jnp.concatenate([gp_ref[...], gn_ref[...]], axis=0))
    zero = jnp.zeros((HALO, D_MODEL), BF16)
    hbuf[0:HALO, :] = jnp.where(ts == 0, zero, hh[0:HALO, :])
    hbuf[HALO + tm:HALO + tm + HALO, :] = jnp.where(ts == tiles_per_seq - 1, zero, hh[HALO:2 * HALO, :])
    hb = hbuf[...]
    mrows = tm + 2 * HALO

    def conv(col):
        u = jnp.dot(hb, wup_ref[:, col:col + ffc], preferred_element_type=F32)
        up = pltpu.roll(u, 1, 0)[HALO:HALO + tm, :]
        un = pltpu.roll(u, mrows - 1, 0)[HALO:HALO + tm, :]
        uc = u[HALO:HALO + tm, :]
        w = cw_ref[:, col:col + ffc]
        return up * w[0:1, :] + uc * w[1:2, :] + un * w[2:3, :] + cb_ref[:, col:col + ffc]

    for c0 in range(0, D_FF, ffc):
        a = conv(c0)
        b = conv(D_FF + c0)
        act_scr[:, c0:c0 + ffc] = (a * _sigmoid(a) * b).astype(BF16)

    f = jnp.dot(act_scr[...], wdn_ref[...], preferred_element_type=F32)
    x2 = x1_scr[...] + mod_ref[5:6, :] * f
    ms = jnp.mean(x2 * x2, axis=-1, keepdims=True)
    y_ref[...] = x2 * lax.rsqrt(ms + EPS) * fn_ref[...]


def _merge_ffn(x2d, attn, rec, gates, w, mods, mod_row, seq_len, tm):
    t, d = x2d.shape
    tiles_per_seq = seq_len // tm
    hb = tm // HALO
    n_hb = t // HALO

    def main(width):
        return pl.BlockSpec((tm, width), lambda i: (i, 0))

    def prev(width):
        return pl.BlockSpec((HALO, width), lambda i: (jnp.maximum(i * hb - 1, 0), 0))

    def nxt(width):
        return pl.BlockSpec((HALO, width), lambda i: (jnp.minimum((i + 1) * hb, n_hb - 1), 0))

    toks = [x2d, attn, rec, gates]
    widths = [d, d, d, 2 * d]
    return pl.pallas_call(
        functools.partial(_mffn_kernel, tiles_per_seq=tiles_per_seq, ffc=256),
        name="merge_ffn",
        grid=(t // tm,),
        in_specs=([main(wd) for wd in widths] + [prev(wd) for wd in widths] + [nxt(wd) for wd in widths]
                  + [_resident((d, d)), _resident((1, d)),
                     _resident((d, 2 * D_FF)), _resident((3, 2 * D_FF)), _resident((1, 2 * D_FF)),
                     _resident((D_FF, d)),
                     pl.BlockSpec((None, 6, d), lambda i: (mod_row(i // tiles_per_seq), 0, 0)),
                     _resident((1, d))]),
        out_specs=pl.BlockSpec((tm, d), lambda i: (i, 0)),
        out_shape=jax.ShapeDtypeStruct((t, d), F32),
        scratch_shapes=[pltpu.VMEM((tm + 2 * HALO, d), BF16),
                        pltpu.VMEM((tm, D_FF), BF16),
                        pltpu.VMEM((tm, d), F32)],
        compiler_params=_cparams(("arbitrary",)),
    )(*toks, *toks, *toks, w["w_o"], w["n2"], w["w_up"], w["conv_w"], w["conv_b"], w["w_down"], mods, w["fnorm"])


def _rope_tables(n_tokens):
    rows = n_tokens // GRID_W
    row = jnp.repeat(jnp.arange(rows, dtype=F32), GRID_W)
    colp = jnp.tile(jnp.arange(GRID_W, dtype=F32), rows)
    half = HEAD_DIM // 2
    inv_freq = 1.0 / (ROPE_THETA ** (jnp.arange(0, half, 2, dtype=F32) / half))
    ar = row[:, None] * inv_freq
    ac = colp[:, None] * inv_freq
    ang = jnp.concatenate([ar, ar, ac, ac], axis=-1)
    sign = jnp.asarray(np.tile(np.repeat(np.array([-1.0, 1.0], np.float32), 16), 2))
    cs = jnp.cos(ang)
    sn = jnp.sin(ang) * sign
    return jnp.tile(cs, (1, 2)), jnp.tile(sn, (1, 2))


def _group(x, mods, mod_row, w, ctx, rope_tabs, tq, pps, agroup, hps, tm_in, tm_ffn, emit_state):
    batch, seq_len, d = x.shape
    x2d = x.reshape(batch * seq_len, d)
    cs, sn = rope_tabs if rope_tabs is not None else (None, None)
    kv_t = emit_state
    q, k, v, rq, g, ri, rg, gates = _inproj(
        x2d, mods, mod_row, seq_len, w["n1"], w["w_in"], w["qg"], w["kg"], w["lbl"], cs, sn, tm=tm_in,
        kv_t=kv_t)
    ck, cv, s0 = ctx
    attn = _attention(q, k, v, ck, cv, batch, seq_len, tq, pps, agroup, kv_t)
    rec_out = _recurrence(rq, g, ri, rg, w["rgain"], s0, batch, seq_len, emit_state, hps)
    rec = rec_out[0]
    y = _merge_ffn(x2d, attn, rec, gates, w, mods, mod_row, seq_len, tm=tm_ffn)
    return y.reshape(batch, seq_len, d), k, v, rec_out[1:]


def kernel(x_prompt, x_sample, c, cache_k, cache_v, state_hgrn, c_ctx, ada_w, ada_b, norm1, norm2,
           w_in, q_norm, k_norm, hgrn_lb_logits, hgrn_norm, w_o, w_up, conv_w, conv_b, w_down, final_norm):
    bp, lp, d = x_prompt.shape
    bs, ls, _ = x_sample.shape
    n_ctx = cache_k.shape[2]

    cond = jnp.zeros((16, d), F32).at[0].set(c_ctx).at[1:1 + bs].set(c)
    mods = _mods(cond, ada_w[0], ada_b[0]).reshape(16, 6, d)

    w = dict(
        n1=norm1[0].reshape(1, d), n2=norm2[0].reshape(1, d), fnorm=final_norm.reshape(1, d),
        w_in=w_in[0].astype(BF16), w_o=w_o[0].astype(BF16),
        w_up=w_up[0].astype(BF16), w_down=w_down[0].astype(BF16),
        conv_w=conv_w[0], conv_b=conv_b[0].reshape(1, 2 * D_FF),
        qg=jnp.tile(q_norm[0], N_HEADS).reshape(1, 1024),
        kg=jnp.tile(k_norm[0], N_KV_HEADS).reshape(1, KV_W),
        lbl=hgrn_lb_logits.reshape(4, 1024),
        rgain=hgrn_norm[0].reshape(1, R_DK),
    )

    y_p, k_p, v_p, st = _group(x_prompt, mods, lambda b: 0, w, (None, None, None), None,
                               tq=lp, pps=8, agroup=8, hps=R_HEADS, tm_in=lp, tm_ffn=lp, emit_state=True)
    def feature_major(cache):
        return jnp.transpose(cache[:, 0], (0, 2, 3, 1)).reshape(bs, KV_W, n_ctx)

    ctx = (feature_major(cache_k), feature_major(cache_v),
           state_hgrn[:, 0].reshape(bs * 2 * R_HEADS, R_DK, R_DK))
    y_s, _, _, _ = _group(x_sample, mods, lambda b: b + 1, w, ctx, _rope_tables(ls),
                          tq=1024, pps=2, agroup=1, hps=1, tm_in=512, tm_ffn=512, emit_state=False)

    new_state = st[0][:, None]
    def position_major(kvt):
        return jnp.transpose(kvt.reshape(bp, N_KV_HEADS, HEAD_DIM, lp), (0, 3, 1, 2))[:, None]

    new_k = position_major(k_p)
    new_v = position_major(v_p)
    return (y_p, y_s, new_k, new_v, new_state)
```

```python
import functools

import jax
import jax.numpy as jnp
import numpy as np
from jax import lax
from jax.experimental import pallas as pl
from jax.experimental.pallas import tpu as pltpu

F32 = jnp.float32
BF16 = jnp.bfloat16

D_MODEL = 1024
GRID_W = 64
HEAD_DIM = 64
N_HEADS = 16
N_KV_HEADS = 4
KV_W = N_KV_HEADS * HEAD_DIM
ROPE_THETA = 10000.0
R_DK = 128
R_HEADS = 8
CHUNK = 64
D_FF = 2816
EPS = 1e-6
LOG2E = 1.4426950408889634
IN_W = 8704

LANES = 128
HALO = 16
VT_ROWS = HEAD_DIM + 16
SM_ROWS = 32
PROJ_W = 512
SHIFT_SLACK = 100.0
VMEM_LIMIT = 56 * 1024 * 1024

_Q0, _K0, _V0, _RQ0, _RFF0, _RFB0, _RI0, _RG0, _ZA0, _ZR0 = (
    0, 1024, 1280, 1536, 2560, 3584, 4608, 5632, 6656, 7680)


def _sigmoid(x):
    return 0.5 * jnp.tanh(0.5 * x) + 0.5


def _cparams(sem):
    return pltpu.CompilerParams(dimension_semantics=sem, vmem_limit_bytes=VMEM_LIMIT)


def _resident(shape):
    nd = len(shape)
    return pl.BlockSpec(shape, lambda *_: (0,) * nd, pipeline_mode=pl.Buffered(1))


def _mod_kernel(c_ref, w_ref, b_ref, o_ref):
    c = c_ref[...]
    s = c * _sigmoid(c)
    o_ref[...] = jnp.dot(s.astype(BF16), w_ref[...].astype(BF16), preferred_element_type=F32) + b_ref[...]


def _mods(cond, ada_w, ada_b):
    rows, d = cond.shape
    n = ada_w.shape[1]
    tn = 1536
    return pl.pallas_call(
        _mod_kernel,
        name="mods",
        grid=(n // tn,),
        in_specs=[pl.BlockSpec((rows, d), lambda j: (0, 0)),
                  pl.BlockSpec((d, tn), lambda j: (0, j)),
                  pl.BlockSpec((1, tn), lambda j: (0, j))],
        out_specs=pl.BlockSpec((rows, tn), lambda j: (0, j)),
        out_shape=jax.ShapeDtypeStruct((rows, n), F32),
        compiler_params=_cparams(("arbitrary",)),
    )(cond, ada_w, ada_b.reshape(1, n))


def _inproj_kernel(*refs, rope, kv_t):
    if rope:
        (x_ref, mod_ref, n1_ref, w_ref, qg_ref, kg_ref, lbl_ref, cs_ref, sn_ref,
         q_ref, k_ref, v_ref, rq_ref, g_ref, ri_ref, rg_ref, gt_ref) = refs
    else:
        (x_ref, mod_ref, n1_ref, w_ref, qg_ref, kg_ref, lbl_ref,
         q_ref, k_ref, v_ref, rq_ref, g_ref, ri_ref, rg_ref, gt_ref) = refs
        cs_ref = sn_ref = None
    tm = x_ref.shape[0]

    x = x_ref[...]
    ms = jnp.mean(x * x, axis=-1, keepdims=True)
    h = x * lax.rsqrt(ms + EPS) * n1_ref[...]
    h = h * (1.0 + mod_ref[1:2, :]) + mod_ref[0:1, :]
    hb = h.astype(BF16)

    def proj(a, b):
        return jnp.dot(hb, w_ref[:, a:b], preferred_element_type=F32)

    low_head = lax.broadcasted_iota(jnp.int32, (tm, LANES), 1) < HEAD_DIM

    def headnorm(a, gain):
        sq = a * a
        s_lo = jnp.sum(jnp.where(low_head, sq, 0.0), axis=-1, keepdims=True)
        s_hi = jnp.sum(jnp.where(low_head, 0.0, sq), axis=-1, keepdims=True)
        ms = jnp.where(low_head, s_lo, s_hi) * (1.0 / HEAD_DIM)
        return a * lax.rsqrt(ms + EPS) * gain

    if rope:
        lane = lax.broadcasted_iota(jnp.int32, (tm, LANES), 1)
        first = (lane % 32) < 16
        cs = cs_ref[...]
        sn = sn_ref[...]

        def rot(a):
            sw = jnp.where(first, pltpu.roll(a, LANES - 16, 1), pltpu.roll(a, 16, 1))
            return a * cs + sw * sn
    else:
        def rot(a):
            return a

    scale = HEAD_DIM ** -0.5 * LOG2E
    for c0 in range(0, 1024, 512):
        acc = proj(_Q0 + c0, _Q0 + c0 + 512)
        for s in range(4):
            col = c0 + s * LANES
            a = headnorm(acc[:, s * LANES:(s + 1) * LANES], qg_ref[:, col:col + LANES] * scale)
            q_ref[:, col:col + LANES] = rot(a).astype(BF16)

    acc = proj(_K0, _K0 + 512)
    for s in range(2):
        col = s * LANES
        a = rot(headnorm(acc[:, col:col + LANES], kg_ref[:, col:col + LANES]))
        vv = acc[:, KV_W + col:KV_W + col + LANES]
        if kv_t:
            k_ref[col:col + LANES, :] = a.T
            v_ref[col:col + LANES, :] = vv.T
        else:
            k_ref[:, col:col + LANES] = a
            v_ref[:, col:col + LANES] = vv

    rscale = R_DK ** -0.5
    for c0 in range(0, 1024, PROJ_W):
        acc = proj(_RQ0 + c0, _RQ0 + c0 + PROJ_W)
        rq_ref[:, c0:c0 + PROJ_W] = (acc * _sigmoid(acc) * rscale).astype(BF16)

    l = lbl_ref[...]
    for d in range(2):
        a0 = l[2 * d:2 * d + 1, :]
        a1 = l[2 * d + 1:2 * d + 2, :]
        mx = jnp.maximum(a0, a1)
        e0 = jnp.exp(a0 - mx)
        e1 = jnp.exp(a1 - mx)
        lb = e0 / (e0 + e1)
        for c0 in range(0, 1024, PROJ_W):
            acc = proj(_RFF0 + d * 1024 + c0, _RFF0 + d * 1024 + c0 + PROJ_W)
            lbc = lb[:, c0:c0 + PROJ_W]
            g_ref[:, d * 1024 + c0:d * 1024 + c0 + PROJ_W] = lbc + (1.0 - lbc) * _sigmoid(acc)

    for c0 in range(0, 1024, PROJ_W):
        ri_ref[:, c0:c0 + PROJ_W] = proj(_RI0 + c0, _RI0 + c0 + PROJ_W).astype(BF16)
    for c0 in range(0, 1024, PROJ_W):
        acc = proj(_RG0 + c0, _RG0 + c0 + PROJ_W)
        rg_ref[:, c0:c0 + PROJ_W] = (acc * _sigmoid(acc)).astype(BF16)
    for c0 in range(0, 2048, PROJ_W):
        acc = proj(_ZA0 + c0, _ZA0 + c0 + PROJ_W)
        gt_ref[:, c0:c0 + PROJ_W] = _sigmoid(acc).astype(BF16)


def _inproj(x2d, mods, mod_row, seq_len, n1, w_in, qg, kg, lbl, cs, sn, tm, kv_t):
    t, d = x2d.shape
    tiles_per_seq = seq_len // tm
    rope = cs is not None

    def tok(w):
        return pl.BlockSpec((tm, w), lambda i: (i, 0))

    in_specs = [tok(d),
                pl.BlockSpec((None, 6, d), lambda i: (mod_row(i // tiles_per_seq), 0, 0)),
                _resident((1, d)),
                _resident((d, IN_W)),
                _resident((1, 1024)),
                _resident((1, KV_W)),
                _resident((4, 1024))]
    args = [x2d, mods, n1, w_in, qg, kg, lbl]
    if rope:
        in_specs += [pl.BlockSpec((tm, LANES), lambda i: (i % tiles_per_seq, 0))] * 2
        args += [cs, sn]
    out_shape = [jax.ShapeDtypeStruct((t, 1024), BF16),
                 jax.ShapeDtypeStruct((t, KV_W), F32),
                 jax.ShapeDtypeStruct((t, KV_W), F32),
                 jax.ShapeDtypeStruct((t, 1024), BF16),
                 jax.ShapeDtypeStruct((t, 2048), F32),
                 jax.ShapeDtypeStruct((t, 1024), BF16),
                 jax.ShapeDtypeStruct((t, 1024), BF16),
                 jax.ShapeDtypeStruct((t, 2048), BF16)]
    out_specs = [tok(s.shape[1]) for s in out_shape]
    if kv_t:
        kvt_shape = jax.ShapeDtypeStruct((t // seq_len, KV_W, seq_len), F32)
        kvt_spec = pl.BlockSpec((None, KV_W, tm), lambda i: (i // tiles_per_seq, 0, i % tiles_per_seq))
        out_shape[1:3] = [kvt_shape, kvt_shape]
        out_specs[1:3] = [kvt_spec, kvt_spec]
    return pl.pallas_call(
        functools.partial(_inproj_kernel, rope=rope, kv_t=kv_t),
        name="inproj_rope" if rope else "inproj",
        grid=(t // tm,),
        in_specs=in_specs, out_specs=out_specs, out_shape=out_shape,
        compiler_params=_cparams(("arbitrary",)),
    )(*args)


def _attn_kernel(*refs, n_lat, n_ctx, tq, pps, kchunk, group, kv_t):
    if n_ctx:
        q_ref, k_ref, v_ref, ck_ref, cv_ref, o_ref, kpad, vt, ref_scr, *p_bufs = refs
    else:
        q_ref, k_ref, v_ref, o_ref, kpad, vt, ref_scr, *p_bufs = refs
        ck_ref = cv_ref = None
    lk = n_lat + n_ctx
    first_step = jnp.logical_and(pl.program_id(1) == 0, pl.program_id(2) == 0)

    @pl.when(first_step)
    def _build():
        def place(dst, rows, ks):
            lane = lax.broadcasted_iota(jnp.int32, ks.shape, 1)
            lo = jnp.where(lane < HEAD_DIM, ks, 0.0)
            hi = jnp.where(lane >= HEAD_DIM, ks, 0.0)
            kpad[dst + 0, rows, :] = lo.astype(BF16)
            kpad[dst + 1, rows, :] = pltpu.roll(lo, HEAD_DIM, 1).astype(BF16)
            kpad[dst + 2, rows, :] = pltpu.roll(hi, HEAD_DIM, 1).astype(BF16)
            kpad[dst + 3, rows, :] = hi.astype(BF16)

        def place_v(s, cols_out, vtr):
            vt[2 * s, 0:HEAD_DIM, cols_out] = vtr[0:HEAD_DIM, :].astype(BF16)
            vt[2 * s + 1, 0:HEAD_DIM, cols_out] = vtr[HEAD_DIM:2 * HEAD_DIM, :].astype(BF16)

        ones_rows = jnp.where(lax.broadcasted_iota(jnp.int32, (VT_ROWS - HEAD_DIM, lk), 0) == 0,
                              1.0, 0.0).astype(BF16)
        for kvh in range(N_KV_HEADS):
            vt[kvh, HEAD_DIM:VT_ROWS, :] = ones_rows
        for s in range(2):
            cols = slice(s * LANES, (s + 1) * LANES)
            if kv_t:
                place(4 * s, slice(0, n_lat), k_ref[cols, :].T)
                place_v(s, slice(0, n_lat), v_ref[cols, :])
            else:
                place(4 * s, slice(0, n_lat), k_ref[:, cols])
                place_v(s, slice(0, n_lat), v_ref[:, cols].T)
            if n_ctx:
                place(4 * s, slice(n_lat, lk), ck_ref[cols, :].T)
                place_v(s, slice(n_lat, lk), cv_ref[cols, :])

    n_chunks = lk // kchunk
    pair0 = pl.program_id(2) * pps
    nt = (((1,), (1,)), ((), ()))
    units = [(j, parity) for j in range(pps) for parity in range(2)]

    def unit_operands(u):
        j, parity = units[u]
        kv = (pair0 + j) // 2
        return 2 * kv + parity, kv, q_ref[:, j * LANES:(j + 1) * LANES]

    def bcast_max(x8):
        return jnp.broadcast_to(jnp.max(x8, axis=0, keepdims=True), (8, tq))

    def exact_max():
        for u in range(len(units)):
            kidx, _, q2 = unit_operands(u)
            mx = jnp.full((8, tq), -jnp.inf, F32)
            for c in range(n_chunks):
                s = lax.dot_general(kpad[kidx, c * kchunk:(c + 1) * kchunk, :], q2, nt,
                                    preferred_element_type=F32)
                mx = jnp.maximum(mx, jnp.max(s.reshape(kchunk // 8, 8, tq), axis=0))
            ref_scr[u] = bcast_max(mx)

    def softmax_pass():
        worst = jnp.zeros((8, tq), F32)
        outs = []
        for u0 in range(0, len(units), group):
            us = range(u0, u0 + group)
            scores = [lax.dot_general(kpad[unit_operands(u)[0]], unit_operands(u)[2], nt,
                                      preferred_element_type=F32) for u in us]
            for u, s in zip(us, scores):
                p_scr = p_bufs[u % len(p_bufs)]
                r8 = ref_scr[u]
                mtot = jnp.full((8, tq), -jnp.inf, F32)
                for r0 in range(0, lk, SM_ROWS):
                    sb = s[r0:r0 + SM_ROWS, :].reshape(SM_ROWS // 8, 8, tq)
                    mtot = jnp.maximum(mtot, jnp.max(sb, axis=0))
                    p_scr[r0:r0 + SM_ROWS, :] = jnp.exp2(sb - r8).reshape(SM_ROWS, tq).astype(BF16)
                worst = jnp.maximum(worst, jnp.abs(bcast_max(mtot) - r8))
            for u in us:
                ot = jnp.dot(vt[unit_operands(u)[1]], p_bufs[u % len(p_bufs)][...],
                             preferred_element_type=F32)
                outs.append(ot[0:HEAD_DIM, :] * (1.0 / ot[HEAD_DIM:HEAD_DIM + 1, :]))
                if u % 2 == 1:
                    j = units[u][0]
                    o2t = jnp.concatenate(outs[-2:], axis=0)
                    o_ref[:, j * LANES:(j + 1) * LANES] = o2t.T.astype(BF16)
        return worst

    def attempt(state):
        n, _ = state

        @pl.when(n == 0)
        def _():
            ref_scr[...] = jnp.zeros(ref_scr.shape, F32)

        @pl.when(n == 1)
        def _():
            exact_max()

        worst = softmax_pass()
        return n + 1, (jnp.max(worst) > SHIFT_SLACK).astype(jnp.int32)

    lax.while_loop(lambda st: jnp.logical_or(st[0] == 0, jnp.logical_and(st[0] == 1, st[1] == 1)),
                   attempt, (jnp.int32(0), jnp.int32(0)))


def _attention(q, k, v, ck, cv, batch, seq_len, tq, pps, group, kv_t):
    t = q.shape[0]
    n_ctx = 0 if ck is None else ck.shape[2]
    lk = seq_len + n_ctx
    n_qt = seq_len // tq
    n_pp = (N_HEADS // 2) // pps
    kchunk = min(lk, 512)
    if kv_t:
        kv_spec = pl.BlockSpec((None, KV_W, seq_len), lambda b, i, p: (b, 0, 0))
    else:
        kv_spec = pl.BlockSpec((seq_len, KV_W), lambda b, i, p: (b, 0))
    in_specs = [pl.BlockSpec((tq, pps * LANES), lambda b, i, p: (b * n_qt + i, p)), kv_spec, kv_spec]
    args = [q, k, v]
    if n_ctx:
        in_specs += [pl.BlockSpec((None, KV_W, n_ctx), lambda b, i, p: (b, 0, 0))] * 2
        args += [ck, cv]
    return pl.pallas_call(
        functools.partial(_attn_kernel, n_lat=seq_len, n_ctx=n_ctx, tq=tq, pps=pps, kchunk=kchunk, group=group, kv_t=kv_t),
        name="attn_ctx" if n_ctx else "attn",
        grid=(batch, n_qt, n_pp),
        in_specs=in_specs,
        out_specs=pl.BlockSpec((tq, pps * LANES), lambda b, i, p: (b * n_qt + i, p)),
        out_shape=jax.ShapeDtypeStruct((t, 1024), BF16),
        scratch_shapes=[pltpu.VMEM((2 * N_KV_HEADS, lk, LANES), BF16),
                        pltpu.VMEM((N_KV_HEADS, VT_ROWS, lk), BF16),
                        pltpu.VMEM((2 * pps, 8, tq), F32)]
                       + [pltpu.VMEM((lk, tq), BF16)] * max(2, group),
        compiler_params=_cparams(("arbitrary", "arbitrary", "arbitrary")),
    )(*args)


def _rec_kernel(*refs, seq_len, has_s0, emit_state, hps):
    refs = list(refs)
    rq_ref, gf_ref, gb_ref, ri_ref, rg_ref, gain_ref = refs[:6]
    pos = 6
    s0f_ref = s0b_ref = sf_ref = sb_ref = None
    if has_s0:
        s0f_ref, s0b_ref = refs[pos:pos + 2]
        pos += 2
    rec_ref = refs[pos]
    pos += 1
    if emit_state:
        st_ref = refs[pos]
        sf_ref, sb_ref = st_ref.at[0], st_ref.at[1]
        pos += 1
    qd_scr, d_scr, u_scr, st_scr, a_scr = refs[pos:]

    nc = seq_len // CHUNK
    tpos = lax.broadcasted_iota(jnp.int32, (CHUNK, R_DK), 0)
    ti = lax.broadcasted_iota(jnp.int32, (CHUNK, CHUNK), 0)
    si = lax.broadcasted_iota(jnp.int32, (CHUNK, CHUNK), 1)
    nt = (((1,), (1,)), ((), ()))
    tn = (((0,), (0,)), ((), ()))
    gain = gain_ref[...]

    def cols(r, hh):
        return r.at[:, pl.ds(hh * R_DK, R_DK)]

    def rows(c):
        return pl.ds(pl.multiple_of(c * CHUNK, CHUNK), CHUNK)

    def chunk_cumsum(g, reverse):
        b = g
        for s in (1, 2, 4):
            if reverse:
                sh = pltpu.roll(b, CHUNK - s, 0)
                b = b + jnp.where(tpos < CHUNK - s, sh, 0.0)
            else:
                sh = pltpu.roll(b, s, 0)
                b = b + jnp.where(tpos >= s, sh, 0.0)
        for s in (8, 16, 32):
            if reverse:
                b = jnp.concatenate([b[:CHUNK - s] + b[s:], b[CHUNK - s:]], axis=0)
            else:
                b = jnp.concatenate([b[:s], b[s:] + b[:CHUNK - s]], axis=0)
        return b

    def ua_chunk(c, hh):
        r = rows(c)
        rq = cols(rq_ref, hh)[r, :].astype(F32)
        a = None
        ke = []
        for d, g_ref in enumerate((gf_ref, gb_ref)):
            f = cols(g_ref, hh)[r, :]
            kk = 1.0 - f
            g = jnp.log2(f)
            b = chunk_cumsum(g, reverse=(d == 1))
            tot = b[0:1, :] if d == 1 else b[CHUNK - 1:CHUNK, :]
            etot = jnp.broadcast_to(jnp.exp2(tot), (CHUNK, R_DK))
            eb = jnp.exp2(b)
            qd = (rq * eb).astype(BF16)
            kd = kk / eb
            ke.append((kd * etot).astype(BF16))
            qd_scr[hh, r, d * R_DK:(d + 1) * R_DK] = qd
            d_scr[hh, r, d * R_DK:(d + 1) * R_DK] = etot
            sc = lax.dot_general(qd, kd.astype(BF16), nt, preferred_element_type=F32)
            sc = jnp.where(si <= ti, sc, 0.0) if d == 0 else jnp.where(si >= ti, sc, 0.0)
            a = sc if a is None else a + sc
        a_scr[hh, r, :] = a.astype(BF16)
        u_scr[hh, c] = lax.dot_general(cols(ri_ref, hh)[r, :], jnp.concatenate(ke, axis=1), tn,
                                       preferred_element_type=F32)

    def ua_body(c, carry):
        for hh in range(hps):
            ua_chunk(c, hh)
        return carry

    lax.fori_loop(0, nc, ua_body, 0, unroll=min(nc, 32))

    for hh in range(hps):
        if has_s0:
            sf0 = s0f_ref[hh].T
            sb0 = s0b_ref[hh].T
        else:
            sf0 = jnp.zeros((R_DK, R_DK), F32)
            sb0 = jnp.zeros((R_DK, R_DK), F32)

        def scan_body(i, carry, hh=hh):
            sf, sb = carry
            cb = nc - 1 - i
            st_scr[hh, i, :, 0:R_DK] = sf.astype(BF16)
            st_scr[hh, cb, :, R_DK:2 * R_DK] = sb.astype(BF16)
            df = d_scr[hh, pl.ds(pl.multiple_of(i * CHUNK, CHUNK), 1), 0:R_DK]
            db = d_scr[hh, pl.ds(pl.multiple_of(cb * CHUNK, CHUNK), 1), R_DK:2 * R_DK]
            sf = df * sf + u_scr[hh, i, :, 0:R_DK]
            sb = db * sb + u_scr[hh, cb, :, R_DK:2 * R_DK]
            return sf, sb

        sf, sb = lax.fori_loop(0, nc, scan_body, (sf0, sb0), unroll=(nc <= 8))
        if emit_state:
            sf_ref[hh] = sf.T
            sb_ref[hh] = sb.T

    def o_body(c, carry):
        r = rows(c)
        for hh in range(hps):
            o = jnp.dot(a_scr[hh, r, :], cols(ri_ref, hh)[r, :], preferred_element_type=F32)
            o = o + lax.dot_general(qd_scr[hh, r, :], st_scr[hh, c], nt, preferred_element_type=F32)
            ms = jnp.mean(o * o, axis=-1, keepdims=True)
            y = o * lax.rsqrt(ms + EPS) * gain * cols(rg_ref, hh)[r, :].astype(F32)
            cols(rec_ref, hh)[r, :] = y.astype(BF16)
        return carry

    lax.fori_loop(0, nc, o_body, 0, unroll=min(nc, 32))


def _recurrence(rq, g, ri, rg, gain, s0, batch, seq_len, emit_state, hps):
    t = rq.shape[0]
    nc = seq_len // CHUNK
    has_s0 = s0 is not None
    hb = R_HEADS // hps

    def col(off):
        return pl.BlockSpec((seq_len, hps * R_DK), lambda b, h: (b, h + off))

    def state(nblk, off):
        return pl.BlockSpec((hps, R_DK, R_DK), lambda b, h: (b * nblk + off + h, 0, 0))

    in_specs = [col(0), col(0), col(hb), col(0), col(0), _resident((1, R_DK))]
    args = [rq, g, g, ri, rg, gain]
    if has_s0:
        in_specs += [state(2 * hb, 0), state(2 * hb, hb)]
        args += [s0, s0]
    out_shape = [jax.ShapeDtypeStruct((t, 1024), BF16)]
    out_specs = [col(0)]
    if emit_state:
        assert hps == R_HEADS
        out_shape += [jax.ShapeDtypeStruct((batch, 2, R_HEADS, R_DK, R_DK), F32)]
        out_specs += [pl.BlockSpec((None, 2, R_HEADS, R_DK, R_DK), lambda b, h: (b, 0, 0, 0, 0))]
    scratch = [pltpu.VMEM((hps, seq_len, 2 * R_DK), BF16),
               pltpu.VMEM((hps, seq_len, 2 * R_DK), F32),
               pltpu.VMEM((hps, nc, R_DK, 2 * R_DK), F32),
               pltpu.VMEM((hps, nc, R_DK, 2 * R_DK), BF16),
               pltpu.VMEM((hps, seq_len, CHUNK), BF16)]
    return pl.pallas_call(
        functools.partial(_rec_kernel, seq_len=seq_len, has_s0=has_s0, emit_state=emit_state, hps=hps),
        name="rec_s0" if has_s0 else "rec",
        grid=(batch, hb),
        in_specs=in_specs, out_specs=out_specs, out_shape=out_shape,
        scratch_shapes=scratch,
        compiler_params=_cparams(("arbitrary", "arbitrary")),
    )(*args)


def _mffn_kernel(x_ref, a_ref, r_ref, g_ref, xp_ref, ap_ref, rp_ref, gp_ref, xn_ref, an_ref, rn_ref, gn_ref,
                 wo_ref, n2_ref, wup_ref, cw_ref, cb_ref, wdn_ref, mod_ref, fn_ref,
                 y_ref, hbuf, act_scr, x1_scr, *, tiles_per_seq, ffc):
    tm = x_ref.shape[0]
    ts = pl.program_id(0) % tiles_per_seq

    def merge_rows(x, a, r, g):
        m = g[:, 0:D_MODEL].astype(F32) * a.astype(F32) + g[:, D_MODEL:2 * D_MODEL].astype(F32) * r.astype(F32)
        out = jnp.dot(m.astype(BF16), wo_ref[...], preferred_element_type=F32)
        x1 = x + mod_ref[2:3, :] * out
        ms = jnp.mean(x1 * x1, axis=-1, keepdims=True)
        h2 = x1 * lax.rsqrt(ms + EPS) * n2_ref[...]
        h2 = h2 * (1.0 + mod_ref[4:5, :]) + mod_ref[3:4, :]
        return x1, h2.astype(BF16)

    x1, h2 = merge_rows(x_ref[...], a_ref[...], r_ref[...], g_ref[...])
    x1_scr[...] = x1
    hbuf[HALO:HALO + tm, :] = h2
    _, hh = merge_rows(jnp.concatenate([xp_ref[...], xn_ref[...]], axis=0),
                       jnp.concatenate([ap_ref[...], an_ref[...]], axis=0),
                       jnp.concatenate([rp_ref[...], rn_ref[...]], axis=0),
                       jnp.concatenate([gp_ref[...], gn_ref[...]], axis=0))
    zero = jnp.zeros((HALO, D_MODEL), BF16)
    hbuf[0:HALO, :] = jnp.where(ts == 0, zero, hh[0:HALO, :])
    hbuf[HALO + tm:HALO + tm + HALO, :] = jnp.where(ts == tiles_per_seq - 1, zero, hh[HALO:2 * HALO, :])
    hb = hbuf[...]
    mrows = tm + 2 * HALO

    def conv(col):
        u = jnp.dot(hb, wup_ref[:, col:col + ffc], preferred_element_type=F32)
        up = pltpu.roll(u, 1, 0)[HALO:HALO + tm, :]
        un = pltpu.roll(u, mrows - 1, 0)[HALO:HALO + tm, :]
        uc = u[HALO:HALO + tm, :]
        w = cw_ref[:, col:col + ffc]
        return up * w[0:1, :] + uc * w[1:2, :] + un * w[2:3, :] + cb_ref[:, col:col + ffc]

    for c0 in range(0, D_FF, ffc):
        a = conv(c0)
        b = conv(D_FF + c0)
        act_scr[:, c0:c0 + ffc] = (a * _sigmoid(a) * b).astype(BF16)

    f = jnp.dot(act_scr[...], wdn_ref[...], preferred_element_type=F32)
    x2 = x1_scr[...] + mod_ref[5:6, :] * f
    ms = jnp.mean(x2 * x2, axis=-1, keepdims=True)
    y_ref[...] = x2 * lax.rsqrt(ms + EPS) * fn_ref[...]


def _merge_ffn(x2d, attn, rec, gates, w, mods, mod_row, seq_len, tm):
    t, d = x2d.shape
    tiles_per_seq = seq_len // tm
    hb = tm // HALO
    n_hb = t // HALO

    def main(width):
        return pl.BlockSpec((tm, width), lambda i: (i, 0))

    def prev(width):
        return pl.BlockSpec((HALO, width), lambda i: (jnp.maximum(i * hb - 1, 0), 0))

    def nxt(width):
        return pl.BlockSpec((HALO, width), lambda i: (jnp.minimum((i + 1) * hb, n_hb - 1), 0))

    toks = [x2d, attn, rec, gates]
    widths = [d, d, d, 2 * d]
    return pl.pallas_call(
        functools.partial(_mffn_kernel, tiles_per_seq=tiles_per_seq, ffc=256),
        name="merge_ffn",
        grid=(t // tm,),
        in_specs=([main(wd) for wd in widths] + [prev(wd) for wd in widths] + [nxt(wd) for wd in widths]
                  + [_resident((d, d)), _resident((1, d)),
                     _resident((d, 2 * D_FF)), _resident((3, 2 * D_FF)), _resident((1, 2 * D_FF)),
                     _resident((D_FF, d)),
                     pl.BlockSpec((None, 6, d), lambda i: (mod_row(i // tiles_per_seq), 0, 0)),
                     _resident((1, d))]),
        out_specs=pl.BlockSpec((tm, d), lambda i: (i, 0)),
        out_shape=jax.ShapeDtypeStruct((t, d), F32),
        scratch_shapes=[pltpu.VMEM((tm + 2 * HALO, d), BF16),
                        pltpu.VMEM((tm, D_FF), BF16),
                        pltpu.VMEM((tm, d), F32)],
        compiler_params=_cparams(("arbitrary",)),
    )(*toks, *toks, *toks, w["w_o"], w["n2"], w["w_up"], w["conv_w"], w["conv_b"], w["w_down"], mods, w["fnorm"])


def _rope_tables(n_tokens):
    rows = n_tokens // GRID_W
    row = jnp.repeat(jnp.arange(rows, dtype=F32), GRID_W)
    colp = jnp.tile(jnp.arange(GRID_W, dtype=F32), rows)
    half = HEAD_DIM // 2
    inv_freq = 1.0 / (ROPE_THETA ** (jnp.arange(0, half, 2, dtype=F32) / half))
    ar = row[:, None] * inv_freq
    ac = colp[:, None] * inv_freq
    ang = jnp.concatenate([ar, ar, ac, ac], axis=-1)
    sign = jnp.asarray(np.tile(np.repeat(np.array([-1.0, 1.0], np.float32), 16), 2))
    cs = jnp.cos(ang)
    sn = jnp.sin(ang) * sign
    return jnp.tile(cs, (1, 2)), jnp.tile(sn, (1, 2))


def _group(x, mods, mod_row, w, ctx, rope_tabs, tq, pps, agroup, hps, tm_in, tm_ffn, emit_state):
    batch, seq_len, d = x.shape
    x2d = x.reshape(batch * seq_len, d)
    cs, sn = rope_tabs if rope_tabs is not None else (None, None)
    kv_t = emit_state
    q, k, v, rq, g, ri, rg, gates = _inproj(
        x2d, mods, mod_row, seq_len, w["n1"], w["w_in"], w["qg"], w["kg"], w["lbl"], cs, sn, tm=tm_in,
        kv_t=kv_t)
    ck, cv, s0 = ctx
    attn = _attention(q, k, v, ck, cv, batch, seq_len, tq, pps, agroup, kv_t)
    rec_out = _recurrence(rq, g, ri, rg, w["rgain"], s0, batch, seq_len, emit_state, hps)
    rec = rec_out[0]
    y = _merge_ffn(x2d, attn, rec, gates, w, mods, mod_row, seq_len, tm=tm_ffn)
    return y.reshape(batch, seq_len, d), k, v, rec_out[1:]


def kernel(x_prompt, x_sample, c, cache_k, cache_v, state_hgrn, c_ctx, ada_w, ada_b, norm1, norm2,
           w_in, q_norm, k_norm, hgrn_lb_logits, hgrn_norm, w_o, w_up, conv_w, conv_b, w_down, final_norm):
    bp, lp, d = x_prompt.shape
    bs, ls, _ = x_sample.shape
    n_ctx = cache_k.shape[2]

    cond = jnp.zeros((16, d), F32).at[0].set(c_ctx).at[1:1 + bs].set(c)
    mods = _mods(cond, ada_w[0], ada_b[0]).reshape(16, 6, d)

    w = dict(
        n1=norm1[0].reshape(1, d), n2=norm2[0].reshape(1, d), fnorm=final_norm.reshape(1, d),
        w_in=w_in[0].astype(BF16), w_o=w_o[0].astype(BF16),
        w_up=w_up[0].astype(BF16), w_down=w_down[0].astype(BF16),
        conv_w=conv_w[0], conv_b=conv_b[0].reshape(1, 2 * D_FF),
        qg=jnp.tile(q_norm[0], N_HEADS).reshape(1, 1024),
        kg=jnp.tile(k_norm[0], N_KV_HEADS).reshape(1, KV_W),
        lbl=hgrn_lb_logits.reshape(4, 1024),
        rgain=hgrn_norm[0].reshape(1, R_DK),
    )

    y_p, k_p, v_p, st = _group(x_prompt, mods, lambda b: 0, w, (None, None, None), None,
                               tq=lp, pps=8, agroup=8, hps=R_HEADS, tm_in=lp, tm_ffn=lp, emit_state=True)
    def feature_major(cache):
        return jnp.transpose(cache[:, 0], (0, 2, 3, 1)).reshape(bs, KV_W, n_ctx)

    ctx = (feature_major(cache_k), feature_major(cache_v),
           state_hgrn[:, 0].reshape(bs * 2 * R_HEADS, R_DK, R_DK))
    y_s, _, _, _ = _group(x_sample, mods, lambda b: b + 1, w, ctx, _rope_tables(ls),
                          tq=2048, pps=1, agroup=1, hps=1, tm_in=512, tm_ffn=512, emit_state=False)

    new_state = st[0][:, None]
    def position_major(kvt):
        return jnp.transpose(kvt.reshape(bp, N_KV_HEADS, HEAD_DIM, lp), (0, 3, 1, 2))[:, None]

    new_k = position_major(k_p)
    new_v = position_major(v_p)
    return (y_p, y_s, new_k, new_v, new_state)
```

```python
import functools

import jax
import jax.numpy as jnp
import numpy as np
from jax import lax
from jax.experimental import pallas as pl
from jax.experimental.pallas import tpu as pltpu

F32 = jnp.float32
BF16 = jnp.bfloat16

D_MODEL = 1024
GRID_W = 64
HEAD_DIM = 64
N_HEADS = 16
N_KV_HEADS = 4
KV_W = N_KV_HEADS * HEAD_DIM
ROPE_THETA = 10000.0
R_DK = 128
R_HEADS = 8
CHUNK = 64
D_FF = 2816
EPS = 1e-6
LOG2E = 1.4426950408889634
IN_W = 8704

LANES = 128
HALO = 16
VT_ROWS = HEAD_DIM + 16
SM_ROWS = 32
PROJ_W = 512
SHIFT_SLACK = 100.0
VMEM_LIMIT = 56 * 1024 * 1024

_Q0, _K0, _V0, _RQ0, _RFF0, _RFB0, _RI0, _RG0, _ZA0, _ZR0 = (
    0, 1024, 1280, 1536, 2560, 3584, 4608, 5632, 6656, 7680)


def _sigmoid(x):
    return 0.5 * jnp.tanh(0.5 * x) + 0.5


def _cparams(sem):
    return pltpu.CompilerParams(dimension_semantics=sem, vmem_limit_bytes=VMEM_LIMIT)


def _resident(shape):
    nd = len(shape)
    return pl.BlockSpec(shape, lambda *_: (0,) * nd, pipeline_mode=pl.Buffered(1))


def _mod_kernel(c_ref, w_ref, b_ref, o_ref):
    c = c_ref[...]
    s = c * _sigmoid(c)
    o_ref[...] = jnp.dot(s.astype(BF16), w_ref[...].astype(BF16), preferred_element_type=F32) + b_ref[...]


def _mods(cond, ada_w, ada_b):
    rows, d = cond.shape
    n = ada_w.shape[1]
    tn = 1536
    return pl.pallas_call(
        _mod_kernel,
        name="mods",
        grid=(n // tn,),
        in_specs=[pl.BlockSpec((rows, d), lambda j: (0, 0)),
                  pl.BlockSpec((d, tn), lambda j: (0, j)),
                  pl.BlockSpec((1, tn), lambda j: (0, j))],
        out_specs=pl.BlockSpec((rows, tn), lambda j: (0, j)),
        out_shape=jax.ShapeDtypeStruct((rows, n), F32),
        compiler_params=_cparams(("arbitrary",)),
    )(cond, ada_w, ada_b.reshape(1, n))


def _inproj_kernel(*refs, rope, kv_t):
    if rope:
        (x_ref, mod_ref, n1_ref, w_ref, qg_ref, kg_ref, lbl_ref, cs_ref, sn_ref,
         q_ref, k_ref, v_ref, rq_ref, g_ref, ri_ref, rg_ref, gt_ref) = refs
    else:
        (x_ref, mod_ref, n1_ref, w_ref, qg_ref, kg_ref, lbl_ref,
         q_ref, k_ref, v_ref, rq_ref, g_ref, ri_ref, rg_ref, gt_ref) = refs
        cs_ref = sn_ref = None
    tm = x_ref.shape[0]

    x = x_ref[...]
    ms = jnp.mean(x * x, axis=-1, keepdims=True)
    h = x * lax.rsqrt(ms + EPS) * n1_ref[...]
    h = h * (1.0 + mod_ref[1:2, :]) + mod_ref[0:1, :]
    hb = h.astype(BF16)

    def proj(a, b):
        return jnp.dot(hb, w_ref[:, a:b], preferred_element_type=F32)

    low_head = lax.broadcasted_iota(jnp.int32, (tm, LANES), 1) < HEAD_DIM

    def headnorm(a, gain):
        sq = a * a
        s_lo = jnp.sum(jnp.where(low_head, sq, 0.0), axis=-1, keepdims=True)
        s_hi = jnp.sum(jnp.where(low_head, 0.0, sq), axis=-1, keepdims=True)
        ms = jnp.where(low_head, s_lo, s_hi) * (1.0 / HEAD_DIM)
        return a * lax.rsqrt(ms + EPS) * gain

    if rope:
        lane = lax.broadcasted_iota(jnp.int32, (tm, LANES), 1)
        first = (lane % 32) < 16
        cs = cs_ref[...]
        sn = sn_ref[...]

        def rot(a):
            sw = jnp.where(first, pltpu.roll(a, LANES - 16, 1), pltpu.roll(a, 16, 1))
            return a * cs + sw * sn
    else:
        def rot(a):
            return a

    scale = HEAD_DIM ** -0.5 * LOG2E
    for c0 in range(0, 1024, 512):
        acc = proj(_Q0 + c0, _Q0 + c0 + 512)
        for s in range(4):
            col = c0 + s * LANES
            a = headnorm(acc[:, s * LANES:(s + 1) * LANES], qg_ref[:, col:col + LANES] * scale)
            q_ref[:, col:col + LANES] = rot(a).astype(BF16)

    acc = proj(_K0, _K0 + 512)
    for s in range(2):
        col = s * LANES
        a = rot(headnorm(acc[:, col:col + LANES], kg_ref[:, col:col + LANES]))
        vv = acc[:, KV_W + col:KV_W + col + LANES]
        if kv_t:
            k_ref[col:col + LANES, :] = a.T
            v_ref[col:col + LANES, :] = vv.T
        else:
            k_ref[:, col:col + LANES] = a
            v_ref[:, col:col + LANES] = vv

    rscale = R_DK ** -0.5
    for c0 in range(0, 1024, PROJ_W):
        acc = proj(_RQ0 + c0, _RQ0 + c0 + PROJ_W)
        rq_ref[:, c0:c0 + PROJ_W] = (acc * _sigmoid(acc) * rscale).astype(BF16)

    l = lbl_ref[...]
    for d in range(2):
        a0 = l[2 * d:2 * d + 1, :]
        a1 = l[2 * d + 1:2 * d + 2, :]
        mx = jnp.maximum(a0, a1)
        e0 = jnp.exp(a0 - mx)
        e1 = jnp.exp(a1 - mx)
        lb = e0 / (e0 + e1)
        for c0 in range(0, 1024, PROJ_W):
            acc = proj(_RFF0 + d * 1024 + c0, _RFF0 + d * 1024 + c0 + PROJ_W)
            lbc = lb[:, c0:c0 + PROJ_W]
            g_ref[:, d * 1024 + c0:d * 1024 + c0 + PROJ_W] = lbc + (1.0 - lbc) * _sigmoid(acc)

    for c0 in range(0, 1024, PROJ_W):
        ri_ref[:, c0:c0 + PROJ_W] = proj(_RI0 + c0, _RI0 + c0 + PROJ_W).astype(BF16)
    for c0 in range(0, 1024, PROJ_W):
        acc = proj(_RG0 + c0, _RG0 + c0 + PROJ_W)
        rg_ref[:, c0:c0 + PROJ_W] = (acc * _sigmoid(acc)).astype(BF16)
    for c0 in range(0, 2048, PROJ_W):
        acc = proj(_ZA0 + c0, _ZA0 + c0 + PROJ_W)
        gt_ref[:, c0:c0 + PROJ_W] = _sigmoid(acc).astype(BF16)


def _inproj(x2d, mods, mod_row, seq_len, n1, w_in, qg, kg, lbl, cs, sn, tm, kv_t):
    t, d = x2d.shape
    tiles_per_seq = seq_len // tm
    rope = cs is not None

    def tok(w):
        return pl.BlockSpec((tm, w), lambda i: (i, 0))

    in_specs = [tok(d),
                pl.BlockSpec((None, 6, d), lambda i: (mod_row(i // tiles_per_seq), 0, 0)),
                _resident((1, d)),
                _resident((d, IN_W)),
                _resident((1, 1024)),
                _resident((1, KV_W)),
                _resident((4, 1024))]
    args = [x2d, mods, n1, w_in, qg, kg, lbl]
    if rope:
        in_specs += [pl.BlockSpec((tm, LANES), lambda i: (i % tiles_per_seq, 0))] * 2
        args += [cs, sn]
    out_shape = [jax.ShapeDtypeStruct((t, 1024), BF16),
                 jax.ShapeDtypeStruct((t, KV_W), F32),
                 jax.ShapeDtypeStruct((t, KV_W), F32),
                 jax.ShapeDtypeStruct((t, 1024), BF16),
                 jax.ShapeDtypeStruct((t, 2048), F32),
                 jax.ShapeDtypeStruct((t, 1024), BF16),
                 jax.ShapeDtypeStruct((t, 1024), BF16),
                 jax.ShapeDtypeStruct((t, 2048), BF16)]
    out_specs = [tok(s.shape[1]) for s in out_shape]
    if kv_t:
        kvt_shape = jax.ShapeDtypeStruct((t // seq_len, KV_W, seq_len), F32)
        kvt_spec = pl.BlockSpec((None, KV_W, tm), lambda i: (i // tiles_per_seq, 0, i % tiles_per_seq))
        out_shape[1:3] = [kvt_shape, kvt_shape]
        out_specs[1:3] = [kvt_spec, kvt_spec]
    return pl.pallas_call(
        functools.partial(_inproj_kernel, rope=rope, kv_t=kv_t),
        name="inproj_rope" if rope else "inproj",
        grid=(t // tm,),
        in_specs=in_specs, out_specs=out_specs, out_shape=out_shape,
        compiler_params=_cparams(("arbitrary",)),
    )(*args)


def _attn_kernel(*refs, n_lat, n_ctx, tq, pps, kchunk, group, kv_t):
    if n_ctx:
        q_ref, k_ref, v_ref, ck_ref, cv_ref, o_ref, kpad, vt, ref_scr, *p_bufs = refs
    else:
        q_ref, k_ref, v_ref, o_ref, kpad, vt, ref_scr, *p_bufs = refs
        ck_ref = cv_ref = None
    lk = n_lat + n_ctx
    first_step = jnp.logical_and(pl.program_id(1) == 0, pl.program_id(2) == 0)

    @pl.when(first_step)
    def _build():
        def place(dst, rows, ks):
            lane = lax.broadcasted_iota(jnp.int32, ks.shape, 1)
            lo = jnp.where(lane < HEAD_DIM, ks, 0.0)
            hi = jnp.where(lane >= HEAD_DIM, ks, 0.0)
            kpad[dst + 0, rows, :] = lo.astype(BF16)
            kpad[dst + 1, rows, :] = pltpu.roll(lo, HEAD_DIM, 1).astype(BF16)
            kpad[dst + 2, rows, :] = pltpu.roll(hi, HEAD_DIM, 1).astype(BF16)
            kpad[dst + 3, rows, :] = hi.astype(BF16)

        def place_v(s, cols_out, vtr):
            vt[2 * s, 0:HEAD_DIM, cols_out] = vtr[0:HEAD_DIM, :].astype(BF16)
            vt[2 * s + 1, 0:HEAD_DIM, cols_out] = vtr[HEAD_DIM:2 * HEAD_DIM, :].astype(BF16)

        ones_rows = jnp.where(lax.broadcasted_iota(jnp.int32, (VT_ROWS - HEAD_DIM, lk), 0) == 0,
                              1.0, 0.0).astype(BF16)
        for kvh in range(N_KV_HEADS):
            vt[kvh, HEAD_DIM:VT_ROWS, :] = ones_rows
        for s in range(2):
            cols = slice(s * LANES, (s + 1) * LANES)
            if kv_t:
                place(4 * s, slice(0, n_lat), k_ref[cols, :].T)
                place_v(s, slice(0, n_lat), v_ref[cols, :])
            else:
                place(4 * s, slice(0, n_lat), k_ref[:, cols])
                place_v(s, slice(0, n_lat), v_ref[:, cols].T)
            if n_ctx:
                place(4 * s, slice(n_lat, lk), ck_ref[cols, :].T)
                place_v(s, slice(n_lat, lk), cv_ref[cols, :])

    n_chunks = lk // kchunk
    pair0 = pl.program_id(2) * pps
    nt = (((1,), (1,)), ((), ()))
    units = [(j, parity) for j in range(pps) for parity in range(2)]

    def unit_operands(u):
        j, parity = units[u]
        kv = (pair0 + j) // 2
        return 2 * kv + parity, kv, q_ref[:, j * LANES:(j + 1) * LANES]

    def bcast_max(x8):
        return jnp.broadcast_to(jnp.max(x8, axis=0, keepdims=True), (8, tq))

    def exact_max():
        for u in range(len(units)):
            kidx, _, q2 = unit_operands(u)
            mx = jnp.full((8, tq), -jnp.inf, F32)
            for c in range(n_chunks):
                s = lax.dot_general(kpad[kidx, c * kchunk:(c + 1) * kchunk, :], q2, nt,
                                    preferred_element_type=F32)
                mx = jnp.maximum(mx, jnp.max(s.reshape(kchunk // 8, 8, tq), axis=0))
            ref_scr[u] = bcast_max(mx)

    def softmax_pass():
        worst = jnp.zeros((8, tq), F32)
        outs = []
        for u0 in range(0, len(units), group):
            us = range(u0, u0 + group)
            scores = [lax.dot_general(kpad[unit_operands(u)[0]], unit_operands(u)[2], nt,
                                      preferred_element_type=F32) for u in us]
            for u, s in zip(us, scores):
                p_scr = p_bufs[u % len(p_bufs)]
                r8 = ref_scr[u]
                mtot = jnp.full((8, tq), -jnp.inf, F32)
                for r0 in range(0, lk, SM_ROWS):
                    sb = s[r0:r0 + SM_ROWS, :].reshape(SM_ROWS // 8, 8, tq)
                    mtot = jnp.maximum(mtot, jnp.max(sb, axis=0))
                    p_scr[r0:r0 + SM_ROWS, :] = jnp.exp2(sb - r8).reshape(SM_ROWS, tq).astype(BF16)
                worst = jnp.maximum(worst, jnp.abs(bcast_max(mtot) - r8))
            for u in us:
                ot = jnp.dot(vt[unit_operands(u)[1]], p_bufs[u % len(p_bufs)][...],
                             preferred_element_type=F32)
                outs.append(ot[0:HEAD_DIM, :] * (1.0 / ot[HEAD_DIM:HEAD_DIM + 1, :]))
                if u % 2 == 1:
                    j = units[u][0]
                    o2t = jnp.concatenate(outs[-2:], axis=0)
                    o_ref[:, j * LANES:(j + 1) * LANES] = o2t.T.astype(BF16)
        return worst

    def attempt(state):
        n, _ = state

        @pl.when(n == 0)
        def _():
            ref_scr[...] = jnp.zeros(ref_scr.shape, F32)

        @pl.when(n == 1)
        def _():
            exact_max()

        worst = softmax_pass()
        return n + 1, (jnp.max(worst) > SHIFT_SLACK).astype(jnp.int32)

    lax.while_loop(lambda st: jnp.logical_or(st[0] == 0, jnp.logical_and(st[0] == 1, st[1] == 1)),
                   attempt, (jnp.int32(0), jnp.int32(0)))


def _attention(q, k, v, ck, cv, batch, seq_len, tq, pps, group, kv_t):
    t = q.shape[0]
    n_ctx = 0 if ck is None else ck.shape[2]
    lk = seq_len + n_ctx
    n_qt = seq_len // tq
    n_pp = (N_HEADS // 2) // pps
    kchunk = min(lk, 512)
    if kv_t:
        kv_spec = pl.BlockSpec((None, KV_W, seq_len), lambda b, i, p: (b, 0, 0))
    else:
        kv_spec = pl.BlockSpec((seq_len, KV_W), lambda b, i, p: (b, 0))
    in_specs = [pl.BlockSpec((tq, pps * LANES), lambda b, i, p: (b * n_qt + i, p)), kv_spec, kv_spec]
    args = [q, k, v]
    if n_ctx:
        in_specs += [pl.BlockSpec((None, KV_W, n_ctx), lambda b, i, p: (b, 0, 0))] * 2
        args += [ck, cv]
    return pl.pallas_call(
        functools.partial(_attn_kernel, n_lat=seq_len, n_ctx=n_ctx, tq=tq, pps=pps, kchunk=kchunk, group=group, kv_t=kv_t),
        name="attn_ctx" if n_ctx else "attn",
        grid=(batch, n_qt, n_pp),
        in_specs=in_specs,
        out_specs=pl.BlockSpec((tq, pps * LANES), lambda b, i, p: (b * n_qt + i, p)),
        out_shape=jax.ShapeDtypeStruct((t, 1024), BF16),
        scratch_shapes=[pltpu.VMEM((2 * N_KV_HEADS, lk, LANES), BF16),
                        pltpu.VMEM((N_KV_HEADS, VT_ROWS, lk), BF16),
                        pltpu.VMEM((2 * pps, 8, tq), F32)]
                       + [pltpu.VMEM((lk, tq), BF16)] * max(2, group),
        compiler_params=_cparams(("arbitrary", "arbitrary", "arbitrary")),
    )(*args)


def _rec_kernel(*refs, seq_len, has_s0, emit_state, hps):
    refs = list(refs)
    rq_ref, gf_ref, gb_ref, ri_ref, rg_ref, gain_ref = refs[:6]
    pos = 6
    s0f_ref = s0b_ref = sf_ref = sb_ref = None
    if has_s0:
        s0f_ref, s0b_ref = refs[pos:pos + 2]
        pos += 2
    rec_ref = refs[pos]
    pos += 1
    if emit_state:
        st_ref = refs[pos]
        sf_ref, sb_ref = st_ref.at[0], st_ref.at[1]
        pos += 1
    qd_scr, d_scr, u_scr, st_scr, a_scr = refs[pos:]

    nc = seq_len // CHUNK
    tpos = lax.broadcasted_iota(jnp.int32, (CHUNK, R_DK), 0)
    ti = lax.broadcasted_iota(jnp.int32, (CHUNK, CHUNK), 0)
    si = lax.broadcasted_iota(jnp.int32, (CHUNK, CHUNK), 1)
    nt = (((1,), (1,)), ((), ()))
    tn = (((0,), (0,)), ((), ()))
    gain = gain_ref[...]

    def cols(r, hh):
        return r.at[:, pl.ds(hh * R_DK, R_DK)]

    def rows(c):
        return pl.ds(pl.multiple_of(c * CHUNK, CHUNK), CHUNK)

    def chunk_cumsum(g, reverse):
        b = g
        for s in (1, 2, 4):
            if reverse:
                sh = pltpu.roll(b, CHUNK - s, 0)
                b = b + jnp.where(tpos < CHUNK - s, sh, 0.0)
            else:
                sh = pltpu.roll(b, s, 0)
                b = b + jnp.where(tpos >= s, sh, 0.0)
        for s in (8, 16, 32):
            if reverse:
                b = jnp.concatenate([b[:CHUNK - s] + b[s:], b[CHUNK - s:]], axis=0)
            else:
                b = jnp.concatenate([b[:s], b[s:] + b[:CHUNK - s]], axis=0)
        return b

    def ua_chunk(c, hh):
        r = rows(c)
        rq = cols(rq_ref, hh)[r, :].astype(F32)
        a = None
        ke = []
        for d, g_ref in enumerate((gf_ref, gb_ref)):
            f = cols(g_ref, hh)[r, :]
            kk = 1.0 - f
            g = jnp.log2(f)
            b = chunk_cumsum(g, reverse=(d == 1))
            tot = b[0:1, :] if d == 1 else b[CHUNK - 1:CHUNK, :]
            etot = jnp.broadcast_to(jnp.exp2(tot), (CHUNK, R_DK))
            eb = jnp.exp2(b)
            qd = (rq * eb).astype(BF16)
            kd = kk / eb
            ke.append((kd * etot).astype(BF16))
            qd_scr[hh, r, d * R_DK:(d + 1) * R_DK] = qd
            d_scr[hh, r, d * R_DK:(d + 1) * R_DK] = etot
            sc = lax.dot_general(qd, kd.astype(BF16), nt, preferred_element_type=F32)
            sc = jnp.where(si <= ti, sc, 0.0) if d == 0 else jnp.where(si >= ti, sc, 0.0)
            a = sc if a is None else a + sc
        a_scr[hh, r, :] = a.astype(BF16)
        u_scr[hh, c] = lax.dot_general(cols(ri_ref, hh)[r, :], jnp.concatenate(ke, axis=1), tn,
                                       preferred_element_type=F32)

    def ua_body(c, carry):
        for hh in range(hps):
            ua_chunk(c, hh)
        return carry

    lax.fori_loop(0, nc, ua_body, 0, unroll=min(nc, 32))

    for hh in range(hps):
        if has_s0:
            sf0 = s0f_ref[hh].T
            sb0 = s0b_ref[hh].T
        else:
            sf0 = jnp.zeros((R_DK, R_DK), F32)
            sb0 = jnp.zeros((R_DK, R_DK), F32)

        def scan_body(i, carry, hh=hh):
            sf, sb = carry
            cb = nc - 1 - i
            st_scr[hh, i, :, 0:R_DK] = sf.astype(BF16)
            st_scr[hh, cb, :, R_DK:2 * R_DK] = sb.astype(BF16)
            df = d_scr[hh, pl.ds(pl.multiple_of(i * CHUNK, CHUNK), 1), 0:R_DK]
            db = d_scr[hh, pl.ds(pl.multiple_of(cb * CHUNK, CHUNK), 1), R_DK:2 * R_DK]
            sf = df * sf + u_scr[hh, i, :, 0:R_DK]
            sb = db * sb + u_scr[hh, cb, :, R_DK:2 * R_DK]
            return sf, sb

        sf, sb = lax.fori_loop(0, nc, scan_body, (sf0, sb0), unroll=(nc <= 8))
        if emit_state:
            sf_ref[hh] = sf.T
            sb_ref[hh] = sb.T

    def o_body(c, carry):
        r = rows(c)
        for hh in range(hps):
            o = jnp.dot(a_scr[hh, r, :], cols(ri_ref, hh)[r, :], preferred_element_type=F32)
            o = o + lax.dot_general(qd_scr[hh, r, :], st_scr[hh, c], nt, preferred_element_type=F32)
            ms = jnp.mean(o * o, axis=-1, keepdims=True)
            y = o * lax.rsqrt(ms + EPS) * gain * cols(rg_ref, hh)[r, :].astype(F32)
            cols(rec_ref, hh)[r, :] = y.astype(BF16)
        return carry

    lax.fori_loop(0, nc, o_body, 0, unroll=min(nc, 32))


def _recurrence(rq, g, ri, rg, gain, s0, batch, seq_len, emit_state, hps):
    t = rq.shape[0]
    nc = seq_len // CHUNK
    has_s0 = s0 is not None
    hb = R_HEADS // hps

    def col(off):
        return pl.BlockSpec((seq_len, hps * R_DK), lambda b, h: (b, h + off))

    def state(nblk, off):
        return pl.BlockSpec((hps, R_DK, R_DK), lambda b, h: (b * nblk + off + h, 0, 0))

    in_specs = [col(0), col(0), col(hb), col(0), col(0), _resident((1, R_DK))]
    args = [rq, g, g, ri, rg, gain]
    if has_s0:
        in_specs += [state(2 * hb, 0), state(2 * hb, hb)]
        args += [s0, s0]
    out_shape = [jax.ShapeDtypeStruct((t, 1024), BF16)]
    out_specs = [col(0)]
    if emit_state:
        assert hps == R_HEADS
        out_shape += [jax.ShapeDtypeStruct((batch, 2, R_HEADS, R_DK, R_DK), F32)]
        out_specs += [pl.BlockSpec((None, 2, R_HEADS, R_DK, R_DK), lambda b, h: (b, 0, 0, 0, 0))]
    scratch = [pltpu.VMEM((hps, seq_len, 2 * R_DK), BF16),
               pltpu.VMEM((hps, seq_len, 2 * R_DK), F32),
               pltpu.VMEM((hps, nc, R_DK, 2 * R_DK), F32),
               pltpu.VMEM((hps, nc, R_DK, 2 * R_DK), BF16),
               pltpu.VMEM((hps, seq_len, CHUNK), BF16)]
    return pl.pallas_call(
        functools.partial(_rec_kernel, seq_len=seq_len, has_s0=has_s0, emit_state=emit_state, hps=hps),
        name="rec_s0" if has_s0 else "rec",
        grid=(batch, hb),
        in_specs=in_specs, out_specs=out_specs, out_shape=out_shape,
        scratch_shapes=scratch,
        compiler_params=_cparams(("arbitrary", "arbitrary")),
    )(*args)


def _mffn_kernel(x_ref, a_ref, r_ref, g_ref, xp_ref, ap_ref, rp_ref, gp_ref, xn_ref, an_ref, rn_ref, gn_ref,
                 wo_ref, n2_ref, wup_ref, cw_ref, cb_ref, wdn_ref, mod_ref, fn_ref,
                 y_ref, hbuf, act_scr, x1_scr, *, tiles_per_seq, ffc):
    tm = x_ref.shape[0]
    ts = pl.program_id(0) % tiles_per_seq

    def merge_rows(x, a, r, g):
        m = g[:, 0:D_MODEL].astype(F32) * a.astype(F32) + g[:, D_MODEL:2 * D_MODEL].astype(F32) * r.astype(F32)
        out = jnp.dot(m.astype(BF16), wo_ref[...], preferred_element_type=F32)
        x1 = x + mod_ref[2:3, :] * out
        ms = jnp.mean(x1 * x1, axis=-1, keepdims=True)
        h2 = x1 * lax.rsqrt(ms + EPS) * n2_ref[...]
        h2 = h2 * (1.0 + mod_ref[4:5, :]) + mod_ref[3:4, :]
        return x1, h2.astype(BF16)

    x1, h2 = merge_rows(x_ref[...], a_ref[...], r_ref[...], g_ref[...])
    x1_scr[...] = x1
    hbuf[HALO:HALO + tm, :] = h2
    _, hh = merge_rows(jnp.concatenate([xp_ref[...], xn_ref[...]], axis=0),
                       jnp.concatenate([ap_ref[...], an_ref[...]], axis=0),
                       jnp.concatenate([rp_ref[...], rn_ref[...]], axis=0),
                       jnp.concatenate([gp_ref[...], gn_ref[...]], axis=0))
    zero = jnp.zeros((HALO, D_MODEL), BF16)
    hbuf[0:HALO, :] = jnp.where(ts == 0, zero, hh[0:HALO, :])
    hbuf[HALO + tm:HALO + tm + HALO, :] = jnp.where(ts == tiles_per_seq - 1, zero, hh[HALO:2 * HALO, :])
    hb = hbuf[...]
    mrows = tm + 2 * HALO

    def conv(col):
        u = jnp.dot(hb, wup_ref[:, col:col + ffc], preferred_element_type=F32)
        up = pltpu.roll(u, 1, 0)[HALO:HALO + tm, :]
        un = pltpu.roll(u, mrows - 1, 0)[HALO:HALO + tm, :]
        uc = u[HALO:HALO + tm, :]
        w = cw_ref[:, col:col + ffc]
        return up * w[0:1, :] + uc * w[1:2, :] + un * w[2:3, :] + cb_ref[:, col:col + ffc]

    for c0 in range(0, D_FF, ffc):
        a = conv(c0)
        b = conv(D_FF + c0)
        act_scr[:, c0:c0 + ffc] = (a * _sigmoid(a) * b).astype(BF16)

    f = jnp.dot(act_scr[...], wdn_ref[...], preferred_element_type=F32)
    x2 = x1_scr[...] + mod_ref[5:6, :] * f
    ms = jnp.mean(x2 * x2, axis=-1, keepdims=True)
    y_ref[...] = x2 * lax.rsqrt(ms + EPS) * fn_ref[...]


def _merge_ffn(x2d, attn, rec, gates, w, mods, mod_row, seq_len, tm):
    t, d = x2d.shape
    tiles_per_seq = seq_len // tm
    hb = tm // HALO
    n_hb = t // HALO

    def main(width):
        return pl.BlockSpec((tm, width), lambda i: (i, 0))

    def prev(width):
        return pl.BlockSpec((HALO, width), lambda i: (jnp.maximum(i * hb - 1, 0), 0))

    def nxt(width):
        return pl.BlockSpec((HALO, width), lambda i: (jnp.minimum((i + 1) * hb, n_hb - 1), 0))

    toks = [x2d, attn, rec, gates]
    widths = [d, d, d, 2 * d]
    return pl.pallas_call(
        functools.partial(_mffn_kernel, tiles_per_seq=tiles_per_seq, ffc=256),
        name="merge_ffn",
        grid=(t // tm,),
        in_specs=([main(wd) for wd in widths] + [prev(wd) for wd in widths] + [nxt(wd) for wd in widths]
                  + [_resident((d, d)), _resident((1, d)),
                     _resident((d, 2 * D_FF)), _resident((3, 2 * D_FF)), _resident((1, 2 * D_FF)),
                     _resident((D_FF, d)),
                     pl.BlockSpec((None, 6, d), lambda i: (mod_row(i // tiles_per_seq), 0, 0)),
                     _resident((1, d))]),
        out_specs=pl.BlockSpec((tm, d), lambda i: (i, 0)),
        out_shape=jax.ShapeDtypeStruct((t, d), F32),
        scratch_shapes=[pltpu.VMEM((tm + 2 * HALO, d), BF16),
                        pltpu.VMEM((tm, D_FF), BF16),
                        pltpu.VMEM((tm, d), F32)],
        compiler_params=_cparams(("arbitrary",)),
    )(*toks, *toks, *toks, w["w_o"], w["n2"], w["w_up"], w["conv_w"], w["conv_b"], w["w_down"], mods, w["fnorm"])


def _rope_tables(n_tokens):
    rows = n_tokens // GRID_W
    row = jnp.repeat(jnp.arange(rows, dtype=F32), GRID_W)
    colp = jnp.tile(jnp.arange(GRID_W, dtype=F32), rows)
    half = HEAD_DIM // 2
    inv_freq = 1.0 / (ROPE_THETA ** (jnp.arange(0, half, 2, dtype=F32) / half))
    ar = row[:, None] * inv_freq
    ac = colp[:, None] * inv_freq
    ang = jnp.concatenate([ar, ar, ac, ac], axis=-1)
    sign = jnp.asarray(np.tile(np.repeat(np.array([-1.0, 1.0], np.float32), 16), 2))
    cs = jnp.cos(ang)
    sn = jnp.sin(ang) * sign
    return jnp.tile(cs, (1, 2)), jnp.tile(sn, (1, 2))


def _group(x, mods, mod_row, w, ctx, rope_tabs, tq, pps, agroup, hps, tm_in, tm_ffn, emit_state):
    batch, seq_len, d = x.shape
    x2d = x.reshape(batch * seq_len, d)
    cs, sn = rope_tabs if rope_tabs is not None else (None, None)
    kv_t = emit_state
    q, k, v, rq, g, ri, rg, gates = _inproj(
        x2d, mods, mod_row, seq_len, w["n1"], w["w_in"], w["qg"], w["kg"], w["lbl"], cs, sn, tm=tm_in,
        kv_t=kv_t)
    ck, cv, s0 = ctx
    attn = _attention(q, k, v, ck, cv, batch, seq_len, tq, pps, agroup, kv_t)
    rec_out = _recurrence(rq, g, ri, rg, w["rgain"], s0, batch, seq_len, emit_state, hps)
    rec = rec_out[0]
    y = _merge_ffn(x2d, attn, rec, gates, w, mods, mod_row, seq_len, tm=tm_ffn)
    return y.reshape(batch, seq_len, d), k, v, rec_out[1:]


def kernel(x_prompt, x_sample, c, cache_k, cache_v, state_hgrn, c_ctx, ada_w, ada_b, norm1, norm2,
           w_in, q_norm, k_norm, hgrn_lb_logits, hgrn_norm, w_o, w_up, conv_w, conv_b, w_down, final_norm):
    bp, lp, d = x_prompt.shape
    bs, ls, _ = x_sample.shape
    n_ctx = cache_k.shape[2]

    cond = jnp.zeros((16, d), F32).at[0].set(c_ctx).at[1:1 + bs].set(c)
    mods = _mods(cond, ada_w[0], ada_b[0]).reshape(16, 6, d)

    w = dict(
        n1=norm1[0].reshape(1, d), n2=norm2[0].reshape(1, d), fnorm=final_norm.reshape(1, d),
        w_in=w_in[0].astype(BF16), w_o=w_o[0].astype(BF16),
        w_up=w_up[0].astype(BF16), w_down=w_down[0].astype(BF16),
        conv_w=conv_w[0], conv_b=conv_b[0].reshape(1, 2 * D_FF),
        qg=jnp.tile(q_norm[0], N_HEADS).reshape(1, 1024),
        kg=jnp.tile(k_norm[0], N_KV_HEADS).reshape(1, KV_W),
        lbl=hgrn_lb_logits.reshape(4, 1024),
        rgain=hgrn_norm[0].reshape(1, R_DK),
    )

    y_p, k_p, v_p, st = _group(x_prompt, mods, lambda b: 0, w, (None, None, None), None,
                               tq=lp, pps=8, agroup=8, hps=R_HEADS, tm_in=lp, tm_ffn=lp, emit_state=True)
    def feature_major(cache):
        return jnp.transpose(cache[:, 0], (0, 2, 3, 1)).reshape(bs, KV_W, n_ctx)

    ctx = (feature_major(cache_k), feature_major(cache_v),
           state_hgrn[:, 0].reshape(bs * 2 * R_HEADS, R_DK, R_DK))
    y_s, _, _, _ = _group(x_sample, mods, lambda b: b + 1, w, ctx, _rope_tables(ls),
                          tq=2048, pps=1, agroup=1, hps=2, tm_in=512, tm_ffn=512, emit_state=False)

    new_state = st[0][:, None]
    def position_major(kvt):
        return jnp.transpose(kvt.reshape(bp, N_KV_HEADS, HEAD_DIM, lp), (0, 3, 1, 2))[:, None]

    new_k = position_major(k_p)
    new_v = position_major(v_p)
    return (y_p, y_s, new_k, new_v, new_state)
```
